```python
import math
import jax
import jax.numpy as jnp
from jax import lax
import numpy as np

D_MODEL = 2048
BATCH = 4
SEQ = 2048
DEPTH = 2

GRID_W = 64
CTX_LEN = 256
Q_BLOCK = 128
NORM_EPS = 1e-6
ROPE_BASE = 10000.0
FOURIER_W = D_MODEL // 2
FOURIER_GROUPS = 4
FOURIER_GROUP_W = FOURIER_W // FOURIER_GROUPS
DIFF_HEAD_DIM = 128
DIFF_HEADS = (D_MODEL - FOURIER_W) // (2 * DIFF_HEAD_DIM)
DIFF_QK_W = DIFF_HEADS * 2 * DIFF_HEAD_DIM
DIFF_V_W = DIFF_HEADS * 2 * DIFF_HEAD_DIM
EVEN_IN_W = FOURIER_W + 2 * DIFF_QK_W + DIFF_V_W
EVEN_OUT_W = FOURIER_W + DIFF_V_W
MLA_HEADS = D_MODEL // 128
MLA_Q_LORA = 512
MLA_KV_LORA = 512
MLA_NOPE = 128
MLA_ROPE = 64
MLA_V = 128
MLA_SCALE = (MLA_NOPE + MLA_ROPE) ** -0.5
FFN_DENSE = 5632
N_EXPERTS = 8
TOP_K = 2
FFN_EXPERT = 7168
ADA_INIT = 0.5
N_EVEN = (DEPTH + 1) // 2
N_ODD = DEPTH // 2

kernel_name = 'hybrid_fourier_diffattn_mla_moe_dit'


def rms_norm(x, g):
    xf = x.astype(jnp.float32)
    y = xf * lax.rsqrt(jnp.mean(xf * xf, axis=-1, keepdims=True) + NORM_EPS)
    return (y * g.astype(jnp.float32)).astype(x.dtype)


def modulate(x, shift, scale):
    return x * (1 + scale) + shift


def _rotate(u, pos):
    a = u.shape[-1]
    inv = ROPE_BASE ** (-jnp.arange(0, a, 2, dtype=jnp.float32) / a)
    ang = pos.astype(jnp.float32)[:, None] * inv[None, :]
    ang = ang.reshape((pos.shape[0],) + (1,) * (u.ndim - 3) + (a // 2,))
    cos, sin = jnp.cos(ang), jnp.sin(ang)
    u1, u2 = jnp.split(u, 2, axis=-1)
    return jnp.concatenate([u1 * cos - u2 * sin, u2 * cos + u1 * sin], axis=-1)


def axial_rope(x, row_pos, col_pos):
    xf = x.astype(jnp.float32)
    half = x.shape[-1] // 2
    out = jnp.concatenate([_rotate(xf[..., :half], row_pos), _rotate(xf[..., half:], col_pos)], axis=-1)
    return out.astype(x.dtype)


def map_query_blocks(fn, qs):
    n = qs[0].shape[1]
    nb = n // Q_BLOCK

    def split(a):
        return jnp.moveaxis(a.reshape((a.shape[0], nb, Q_BLOCK) + a.shape[2:]), 1, 0)

    out = lax.map(lambda blk: fn(*blk), tuple(split(a) for a in qs))
    out = jnp.moveaxis(out, 0, 1)
    return out.reshape((out.shape[0], n) + out.shape[3:])


def fourier_mix(f):
    b, n, _ = f.shape
    u = f.astype(jnp.float32).reshape(b, n, FOURIER_GROUPS, FOURIER_GROUP_W)
    y = jnp.fft.fft2(u, axes=(1, 3), norm='ortho').real
    return y.reshape(b, n, FOURIER_W).astype(f.dtype)


def diff_attn_block(q, k, v, lam):
    s = jnp.einsum('bqhcd,bkhcd->bhcqk', q, k, preferred_element_type=jnp.float32) * (DIFF_HEAD_DIM ** -0.5)
    p = jax.nn.softmax(s, axis=-1)
    a = p[:, :, 0] - lam * p[:, :, 1]
    return jnp.einsum('bhqk,bkhe->bqhe', a.astype(v.dtype), v)


def even_mixer(h, hc, w_in, w_out, lam_vec, subln_g, lam_init, row_pos, col_pos, ctx_out):
    def heads_qk(t):
        return t.reshape(t.shape[0], t.shape[1], DIFF_HEADS, 2, DIFF_HEAD_DIM)

    def heads_v(t):
        return t.reshape(t.shape[0], t.shape[1], DIFF_HEADS, 2 * DIFF_HEAD_DIM)

    lv = lam_vec.astype(jnp.float32)
    lam = jnp.exp(jnp.sum(lv[0] * lv[1])) - jnp.exp(jnp.sum(lv[2] * lv[3])) + lam_init

    def finish(f, o):
        o = rms_norm(o, subln_g) * (1.0 - lam_init)
        o = o.reshape(o.shape[0], o.shape[1], DIFF_V_W)
        return jnp.concatenate([fourier_mix(f), o], axis=-1) @ w_out

    f_l, q_l, k_l, v_l = jnp.split(h @ w_in, [FOURIER_W, FOURIER_W + DIFF_QK_W, FOURIER_W + 2 * DIFF_QK_W], axis=-1)
    k_c, v_c = jnp.split(hc @ w_in[:, FOURIER_W + DIFF_QK_W:], [DIFF_QK_W], axis=-1)
    q_l = axial_rope(heads_qk(q_l), row_pos, col_pos)
    k_l = axial_rope(heads_qk(k_l), row_pos, col_pos)
    k_c, v_c = heads_qk(k_c), heads_v(v_c)
    k_all = jnp.concatenate([k_c, k_l], axis=1)
    v_all = jnp.concatenate([v_c, heads_v(v_l)], axis=1)
    o_l = map_query_blocks(lambda qb: diff_attn_block(qb, k_all, v_all, lam), (q_l,))
    y_l = finish(f_l, o_l)
    y_c = None
    if ctx_out:
        f_c, q_c = jnp.split(hc @ w_in[:, :FOURIER_W + DIFF_QK_W], [FOURIER_W], axis=-1)
        o_c = map_query_blocks(lambda qb: diff_attn_block(qb, k_c, v_c, lam), (heads_qk(q_c),))
        y_c = finish(f_c, o_c)
    return y_l, y_c


def mla_block(qn, qr, kn, kr, v):
    s = (jnp.einsum('bqhd,bkhd->bhqk', qn, kn, preferred_element_type=jnp.float32)
         + jnp.einsum('bqhr,bkr->bhqk', qr, kr, preferred_element_type=jnp.float32)) * MLA_SCALE
    p = jax.nn.softmax(s, axis=-1)
    return jnp.einsum('bhqk,bkhd->bqhd', p.astype(v.dtype), v)


def mla_mixer(h, hc, w_dq, q_norm_g, w_uq, w_dkv, kv_norm_g, w_ukv, w_o, row_pos, col_pos, ctx_out):
    def queries(u):
        cq = rms_norm(u @ w_dq, q_norm_g)
        q = (cq @ w_uq).reshape(u.shape[0], u.shape[1], MLA_HEADS, MLA_NOPE + MLA_ROPE)
        return q[..., :MLA_NOPE], q[..., MLA_NOPE:]

    def keys_values(u):
        ckv = u @ w_dkv
        c_kv = rms_norm(ckv[..., :MLA_KV_LORA], kv_norm_g)
        kv = (c_kv @ w_ukv).reshape(u.shape[0], u.shape[1], MLA_HEADS, MLA_NOPE + MLA_V)
        return kv[..., :MLA_NOPE], ckv[..., MLA_KV_LORA:], kv[..., MLA_NOPE:]

    def out(o):
        return o.reshape(o.shape[0], o.shape[1], MLA_HEADS * MLA_V) @ w_o

    qn, qr = queries(h)
    qr = axial_rope(qr, row_pos, col_pos)
    kn_l, kr_l, v_l = keys_values(h)
    kr_l = axial_rope(kr_l, row_pos, col_pos)
    kn_c, kr_c, v_c = keys_values(hc)
    kn_all = jnp.concatenate([kn_c, kn_l], axis=1)
    kr_all = jnp.concatenate([kr_c, kr_l], axis=1)
    v_all = jnp.concatenate([v_c, v_l], axis=1)
    y_l = out(map_query_blocks(lambda a, b: mla_block(a, b, kn_all, kr_all, v_all), (qn, qr)))
    y_c = None
    if ctx_out:
        qn_c, qr_c = queries(hc)
        y_c = out(map_query_blocks(lambda a, b: mla_block(a, b, kn_c, kr_c, v_c), (qn_c, qr_c)))
    return y_l, y_c


def swiglu(u, wg, wu, wd):
    return (jax.nn.silu(u @ wg) * (u @ wu)) @ wd


def moe_ffn(u, router, wg, wu, wd):
    b, n, d = u.shape
    flat = u.reshape(b * n, d)
    logits = jnp.dot(flat, router, preferred_element_type=jnp.float32)
    top_vals, top_idx = lax.top_k(logits, TOP_K)
    top_w = jax.nn.softmax(top_vals, axis=-1)
    gate = jnp.sum(jax.nn.one_hot(top_idx, N_EXPERTS, dtype=jnp.float32) * top_w[..., None], axis=1).astype(u.dtype)
    out = jnp.zeros_like(flat)
    for e in range(N_EXPERTS):
        out = out + gate[:, e:e + 1] * swiglu(flat, wg[e], wu[e], wd[e])
    return out.reshape(b, n, d)


def setup_inputs(seed: int = 0) -> dict:
    key = jax.random.key(seed)
    ks = iter(jax.random.split(key, 40))
    D = D_MODEL

    def nrm(shape, scale):
        return jax.random.normal(next(ks), shape, jnp.float32) * scale

    def gain(shape):
        return 1.0 + nrm(shape, 0.01)

    return {
        'x': nrm((BATCH, SEQ, D), 1.0),
        'c': nrm((BATCH, D), 1.0),
        'ctx': nrm((BATCH, CTX_LEN, D), 1.0),
        'c_ctx': nrm((D,), 1.0),
        'ada_w': nrm((DEPTH, D, 6 * D), ADA_INIT * D ** -0.5),
        'ada_b': nrm((DEPTH, 6 * D), 0.02),
        'norm1_g': gain((DEPTH, D)),
        'norm2_g': gain((DEPTH, D)),
        'ev_w_in': nrm((N_EVEN, D, EVEN_IN_W), D ** -0.5),
        'ev_w_out': nrm((N_EVEN, EVEN_OUT_W, D), EVEN_OUT_W ** -0.5),
        'ev_lambda': nrm((N_EVEN, 4, DIFF_HEAD_DIM), 0.1),
        'ev_subln_g': gain((N_EVEN, 2 * DIFF_HEAD_DIM)),
        'od_w_dq': nrm((N_ODD, D, MLA_Q_LORA), D ** -0.5),
        'od_q_norm_g': gain((N_ODD, MLA_Q_LORA)),
        'od_w_uq': nrm((N_ODD, MLA_Q_LORA, MLA_HEADS * (MLA_NOPE + MLA_ROPE)), MLA_Q_LORA ** -0.5),
        'od_w_dkv': nrm((N_ODD, D, MLA_KV_LORA + MLA_ROPE), D ** -0.5),
        'od_kv_norm_g': gain((N_ODD, MLA_KV_LORA)),
        'od_w_ukv': nrm((N_ODD, MLA_KV_LORA, MLA_HEADS * (MLA_NOPE + MLA_V)), MLA_KV_LORA ** -0.5),
        'od_w_o': nrm((N_ODD, MLA_HEADS * MLA_V, D), (MLA_HEADS * MLA_V) ** -0.5),
        'ffn_w_gate': nrm((N_EVEN, D, FFN_DENSE), D ** -0.5),
        'ffn_w_up': nrm((N_EVEN, D, FFN_DENSE), D ** -0.5),
        'ffn_w_down': nrm((N_EVEN, FFN_DENSE, D), FFN_DENSE ** -0.5),
        'moe_router': nrm((N_ODD, D, N_EXPERTS), D ** -0.5),
        'moe_w_gate': nrm((N_ODD, N_EXPERTS, D, FFN_EXPERT), D ** -0.5),
        'moe_w_up': nrm((N_ODD, N_EXPERTS, D, FFN_EXPERT), D ** -0.5),
        'moe_w_down': nrm((N_ODD, N_EXPERTS, FFN_EXPERT, D), FFN_EXPERT ** -0.5),
        'final_norm_g': gain((D,)),
    }


def reference(x, c, ctx, c_ctx, ada_w, ada_b, norm1_g, norm2_g, ev_w_in, ev_w_out, ev_lambda, ev_subln_g,
              od_w_dq, od_q_norm_g, od_w_uq, od_w_dkv, od_kv_norm_g, od_w_ukv, od_w_o,
              ffn_w_gate, ffn_w_up, ffn_w_down, moe_router, moe_w_gate, moe_w_up, moe_w_down, final_norm_g):
    n_lat = x.shape[1]
    rows = n_lat // GRID_W
    row_pos = jnp.repeat(jnp.arange(rows, dtype=jnp.int32), GRID_W)
    col_pos = jnp.tile(jnp.arange(GRID_W, dtype=jnp.int32), rows)
    s_c = jax.nn.silu(c)
    s_cc = jax.nn.silu(c_ctx)

    def channel_mixer(i, j, u):
        if i % 2 == 0:
            return swiglu(u, ffn_w_gate[j], ffn_w_up[j], ffn_w_down[j])
        return moe_ffn(u, moe_router[j], moe_w_gate[j], moe_w_up[j], moe_w_down[j])

    x_l, x_c = x, ctx
    for i in range(DEPTH):
        last = i == DEPTH - 1
        j = i // 2
        mod_l = jnp.split((s_c @ ada_w[i] + ada_b[i])[:, None, :], 6, axis=-1)
        mod_c = jnp.split(s_cc @ ada_w[i] + ada_b[i], 6, axis=-1)
        h = modulate(rms_norm(x_l, norm1_g[i]), mod_l[0], mod_l[1])
        hc = modulate(rms_norm(x_c, norm1_g[i]), mod_c[0], mod_c[1])
        if i % 2 == 0:
            lam_init = 0.8 - 0.6 * math.exp(-0.3 * i)
            y, yc = even_mixer(h, hc, ev_w_in[j], ev_w_out[j], ev_lambda[j], ev_subln_g[j], lam_init,
                               row_pos, col_pos, not last)
        else:
            y, yc = mla_mixer(h, hc, od_w_dq[j], od_q_norm_g[j], od_w_uq[j], od_w_dkv[j], od_kv_norm_g[j],
                              od_w_ukv[j], od_w_o[j], row_pos, col_pos, not last)
        x_l = x_l + mod_l[2] * y
        h2 = modulate(rms_norm(x_l, norm2_g[i]), mod_l[3], mod_l[4])
        x_l = x_l + mod_l[5] * channel_mixer(i, j, h2)
        if not last:
            x_c = x_c + mod_c[2] * yc
            h2c = modulate(rms_norm(x_c, norm2_g[i]), mod_c[3], mod_c[4])
            x_c = x_c + mod_c[5] * channel_mixer(i, j, h2c)
    return rms_norm(x_l, final_norm_g)
```

```python
import functools
import math

import numpy as np
import jax
import jax.numpy as jnp
from jax import lax
from jax.experimental import pallas as pl
from jax.experimental.pallas import tpu as pltpu

F32 = jnp.float32
BF16 = jnp.bfloat16

GRID_W = 64
NORM_EPS = 1e-6
ROPE_BASE = 10000.0
FOURIER_GROUPS = 4
DIFF_HEAD_DIM = 128
MLA_NOPE = 128
MLA_ROPE = 64
MLA_V = 128
N_EXPERTS = 8
TOP_K = 2

LANES = 128
MOD_ROWS = 8
VMEM_LIMIT = 56 * 1024 * 1024
MOE_BLOCK = 1024
MOE_SUB = 256


def _cparams(n_axes):
    return pltpu.CompilerParams(dimension_semantics=("arbitrary",) * n_axes,
                                vmem_limit_bytes=VMEM_LIMIT)


def _rms(x, g):
    return x * lax.rsqrt(jnp.mean(x * x, axis=-1, keepdims=True) + NORM_EPS) * g


def _norm_mod(x, g, shift, scale):
    return _rms(x, g) * (1.0 + scale) + shift


def _silu(x):
    return x * (1.0 / (1.0 + jnp.exp(-x)))


def _dot(a, b):
    return jnp.dot(a, b, preferred_element_type=F32)


def _dot_nt(a, b):
    return lax.dot_general(a, b, (((1,), (1,)), ((), ())), preferred_element_type=F32)


def _rope_tables(seq, extra_rows, chunk, scale):
    n = np.arange(seq)
    row, col = n // GRID_W, n % GRID_W
    lane = np.arange(LANES)
    a = 2 * chunk
    inv = ROPE_BASE ** (-np.arange(0, a, 2, dtype=np.float64) / a)
    used = lane < 4 * chunk
    freq = inv[lane % chunk]
    pos = np.where(lane[None, :] < 2 * chunk, row[:, None], col[:, None]).astype(np.float64)
    ang = pos * freq[None, :]
    first = (lane // chunk) % 2 == 0
    cos = np.where(used[None, :], np.cos(ang), 0.0)
    sin = np.where(used[None, :], np.sin(ang), 0.0)
    s1 = np.where(first[None, :], -sin, 0.0)
    s2 = np.where(first[None, :], 0.0, sin)
    ident = np.zeros((3, extra_rows, LANES))
    ident[0] = used[None, :].astype(np.float64)
    tab = np.concatenate([np.stack([cos, s1, s2]), ident], axis=1) * scale
    return tab.astype(np.float32)


def _apply_rope(x, tab_ref, chunk):
    return (x * tab_ref[0] + pltpu.roll(x, LANES - chunk, 1) * tab_ref[1]
            + pltpu.roll(x, chunk, 1) * tab_ref[2])


def _dft_cos_sin(n):
    k = np.arange(n)
    ang = 2.0 * np.pi * ((k[:, None] * k[None, :]) % n) / n
    return np.cos(ang), np.sin(ang)


def _ada_kernel(s_ref, w_ref, b_ref, o_ref):
    s = _silu(s_ref[...]).astype(BF16)
    o_ref[0] = _dot(s, w_ref[0].astype(BF16)) + b_ref[0]


def _ada(cond, ada_w, ada_b):
    depth, d, n = ada_w.shape
    tn = 1024
    return pl.pallas_call(
        _ada_kernel,
        grid=(depth, n // tn),
        in_specs=[pl.BlockSpec((MOD_ROWS, d), lambda i, j: (0, 0)),
                  pl.BlockSpec((1, d, tn), lambda i, j: (i, 0, j)),
                  pl.BlockSpec((1, 1, tn), lambda i, j: (i, 0, j))],
        out_specs=pl.BlockSpec((1, MOD_ROWS, tn), lambda i, j: (i, 0, j)),
        out_shape=jax.ShapeDtypeStruct((depth, MOD_ROWS, n), F32),
        compiler_params=_cparams(2),
        name="ada",
    )(cond, ada_w, ada_b.reshape(depth, 1, n))


def _inproj_kernel(x_ref, g_ref, mod_ref, w_ref, tab_ref, o_ref, h_ref):
    j = pl.program_id(1)
    quarter = pl.num_programs(1) // 4

    @pl.when(j == 0)
    def _():
        m = mod_ref[0]
        h_ref[...] = _norm_mod(x_ref[...], g_ref[...], m[0:1], m[1:2]).astype(BF16)

    res = _dot(h_ref[...], w_ref[...].astype(BF16))
    is_rope = jnp.logical_and(j >= quarter, j < 3 * quarter)

    @pl.when(is_rope)
    def _():
        for c in range(res.shape[1] // LANES):
            sl = slice(c * LANES, (c + 1) * LANES)
            o_ref[:, sl] = _apply_rope(res[:, sl], tab_ref.at[0], DIFF_HEAD_DIM // 4).astype(BF16)

    @pl.when(jnp.logical_not(is_rope))
    def _():
        o_ref[...] = res.astype(BF16)


def _inproj(x, g, mod, w, tabs, dims, tm, tn):
    m_rows, d = x.shape
    seq, lat = dims["seq"], dims["lat"]
    n = w.shape[1]
    nj = n // tn
    assert nj % 4 == 0
    lat_tiles, seq_tiles = lat // tm, seq // tm

    def sel(i):
        return jnp.minimum(i // seq_tiles, dims["batch"])

    def tab_map(i, j):
        return (jnp.where(j >= nj // 2, 1, 0), 0, jnp.where(i < lat_tiles, i % seq_tiles, seq_tiles), 0)

    return pl.pallas_call(
        _inproj_kernel,
        grid=(m_rows // tm, nj),
        in_specs=[pl.BlockSpec((tm, d), lambda i, j: (i, 0)),
                  pl.BlockSpec((1, d), lambda i, j: (0, 0)),
                  pl.BlockSpec((1, 6, d), lambda i, j: (sel(i), 0, 0)),
                  pl.BlockSpec((d, tn), lambda i, j: (0, j)),
                  pl.BlockSpec((1, 3, tm, LANES), tab_map)],
        out_specs=pl.BlockSpec((tm, tn), lambda i, j: (i, j)),
        out_shape=jax.ShapeDtypeStruct((m_rows, n), BF16),
        scratch_shapes=[pltpu.VMEM((tm, d), BF16)],
        compiler_params=_cparams(2),
        name="ev_inproj",
    )(x, g.reshape(1, d), mod, w, tabs)


def _diff_attn_kernel(lam_ref, q_ref, kc_ref, kl_ref, vc_ref, vl_ref, g_ref, o_ref, k_scr, v_scr,
                      *, lam_init, n_lat_tiles, n_ctx):
    i = pl.program_id(2)
    hd = DIFF_HEAD_DIM

    @pl.when(i == 0)
    def _():
        k_scr[0:n_ctx] = kc_ref[...]
        k_scr[n_ctx:] = kl_ref[...]
        v_scr[0:n_ctx] = vc_ref[...]
        v_scr[n_ctx:] = vl_ref[...]

    lv = lam_ref[...]
    lam = (jnp.exp(jnp.sum(lv[0:1] * lv[1:2], axis=-1, keepdims=True))
           - jnp.exp(jnp.sum(lv[2:3] * lv[3:4], axis=-1, keepdims=True)) + lam_init)
    q = q_ref[...]

    def attend(n_keys):
        e, inv = [], []
        for c in range(2):
            s = _dot_nt(q[:, c * hd:(c + 1) * hd], k_scr[0:n_keys, c * hd:(c + 1) * hd])
            ex = jnp.exp(s - jnp.max(s, axis=-1, keepdims=True))
            e.append(ex)
            inv.append(1.0 / jnp.sum(ex, axis=-1, keepdims=True))
        a = (e[0] * inv[0] - e[1] * (lam * inv[1])).astype(BF16)
        o = _dot(a, v_scr[0:n_keys])
        o_ref[...] = (_rms(o, g_ref[...]) * (1.0 - lam_init)).astype(BF16)

    @pl.when(i < n_lat_tiles)
    def _():
        attend(k_scr.shape[0])

    @pl.when(i >= n_lat_tiles)
    def _():
        attend(n_ctx)


def _diff_attn(qkvf, lam_vec, subln_g, lam_init, dims, tq):
    m_rows = qkvf.shape[0]
    batch, seq, ctx = dims["batch"], dims["seq"], dims["ctx"]
    hw = 2 * DIFF_HEAD_DIM
    width = qkvf.shape[1] // 4
    heads = width // hw
    nq = seq // tq
    assert ctx == tq
    ctx_blk0 = batch * seq // ctx
    qcol, kcol, vcol = width // hw, 2 * width // hw, 3 * width // hw

    def q_map(b, h, i):
        return (jnp.where(i < nq, b * nq + i, ctx_blk0 + b), qcol + h)

    def o_map(b, h, i):
        return (jnp.where(i < nq, b * nq + i, ctx_blk0 + b), h)

    kern = functools.partial(_diff_attn_kernel, lam_init=lam_init, n_lat_tiles=nq, n_ctx=ctx)
    return pl.pallas_call(
        kern,
        grid=(batch, heads, nq + 1),
        in_specs=[pl.BlockSpec((4, DIFF_HEAD_DIM), lambda b, h, i: (0, 0)),
                  pl.BlockSpec((tq, hw), q_map),
                  pl.BlockSpec((ctx, hw), lambda b, h, i: (ctx_blk0 + b, kcol + h)),
                  pl.BlockSpec((seq, hw), lambda b, h, i: (b, kcol + h)),
                  pl.BlockSpec((ctx, hw), lambda b, h, i: (ctx_blk0 + b, vcol + h)),
                  pl.BlockSpec((seq, hw), lambda b, h, i: (b, vcol + h)),
                  pl.BlockSpec((1, hw), lambda b, h, i: (0, 0))],
        out_specs=pl.BlockSpec((tq, hw), o_map),
        out_shape=jax.ShapeDtypeStruct((m_rows, width), BF16),
        scratch_shapes=[pltpu.VMEM((ctx + seq, hw), BF16), pltpu.VMEM((ctx + seq, hw), BF16)],
        compiler_params=_cparams(3),
        name="ev_diff_attn",
    )(lam_vec, qkvf, qkvf, qkvf, qkvf, qkvf, subln_g.reshape(1, hw))


def _fourier_kernel(u_ref, cc_ref, sc_ref, cs_ref, *rest, n, norm):
    o_ref, ab_ref = rest[-2], rest[-1]
    u = u_ref[...]
    ab_ref[0:n] = _dot(u, cc_ref[...]).astype(BF16)
    ab_ref[n:] = _dot(u, sc_ref[...]).astype(BF16)
    o_ref[...] = (_dot(cs_ref[...], ab_ref[...]) * norm).astype(BF16)


def _fourier(qkvf, prev, n, row_blk0, batch, width, cc, sc, cs, name):
    m_rows = qkvf.shape[0]
    gw = width // FOURIER_GROUPS
    kern = functools.partial(_fourier_kernel, n=n, norm=1.0 / math.sqrt(n * gw))
    in_specs = [pl.BlockSpec((n, gw), lambda b, g: (row_blk0 + b, g)),
                pl.BlockSpec((gw, gw), lambda b, g: (0, 0)),
                pl.BlockSpec((gw, gw), lambda b, g: (0, 0)),
                pl.BlockSpec((n, 2 * n), lambda b, g: (0, 0), pipeline_mode=pl.Buffered(1))]
    args = [qkvf, cc, sc, cs]
    aliases = {}
    if prev is not None:
        in_specs.append(pl.BlockSpec(memory_space=pl.ANY))
        args.append(prev)
        aliases = {4: 0}
    return pl.pallas_call(
        kern,
        grid=(batch, FOURIER_GROUPS),
        in_specs=in_specs,
        out_specs=pl.BlockSpec((n, gw), lambda b, g: (row_blk0 + b, g)),
        out_shape=jax.ShapeDtypeStruct((m_rows, width), BF16),
        scratch_shapes=[pltpu.VMEM((2 * n, gw), BF16)],
        input_output_aliases=aliases,
        compiler_params=_cparams(2),
        name=name,
    )(*args)


def _mm_res_kernel(*refs, n_a, gate_idx):
    a_refs, w_refs = refs[:n_a], refs[n_a:2 * n_a]
    x_ref, mod_ref, o_ref = refs[2 * n_a:]
    acc = _dot(a_refs[0][...], w_refs[0][...].astype(BF16))
    for a_ref, w_ref in zip(a_refs[1:], w_refs[1:]):
        acc = acc + _dot(a_ref[...], w_ref[...].astype(BF16))
    m = mod_ref[0]
    o_ref[...] = x_ref[...] + m[gate_idx:gate_idx + 1] * acc


def _mm_res(a_list, w, x, mod, gate_idx, dims, rows, tm, tn, name):
    d = w.shape[1]
    seq_tiles = dims["seq"] // tm
    n_a = len(a_list)

    def sel(i):
        return jnp.minimum(i // seq_tiles, dims["batch"])

    in_specs, w_args, k0 = [], [], 0
    for a in a_list:
        in_specs.append(pl.BlockSpec((tm, a.shape[1]), lambda i, j: (i, 0)))
    for a in a_list:
        ka = a.shape[1]
        assert k0 % ka == 0
        in_specs.append(pl.BlockSpec((ka, tn), lambda i, j, kb=k0 // ka: (kb, j)))
        w_args.append(w)
        k0 += ka
    assert k0 == w.shape[0]
    in_specs += [pl.BlockSpec((tm, tn), lambda i, j: (i, j)),
                 pl.BlockSpec((1, 6, tn), lambda i, j: (sel(i), 0, j))]
    return pl.pallas_call(
        functools.partial(_mm_res_kernel, n_a=n_a, gate_idx=gate_idx),
        grid=(rows // tm, d // tn),
        in_specs=in_specs,
        out_specs=pl.BlockSpec((tm, tn), lambda i, j: (i, j)),
        out_shape=jax.ShapeDtypeStruct((rows, d), F32),
        compiler_params=_cparams(2),
        name=name,
    )(*a_list, *w_args, x, mod)


def _glu_kernel(x_ref, g_ref, mod_ref, wg_ref, wu_ref, o_ref, h_ref):
    @pl.when(pl.program_id(1) == 0)
    def _():
        m = mod_ref[0]
        h_ref[...] = _norm_mod(x_ref[...], g_ref[...], m[3:4], m[4:5]).astype(BF16)

    h = h_ref[...]
    gate = _dot(h, wg_ref[...].astype(BF16))
    up = _dot(h, wu_ref[...].astype(BF16))
    o_ref[...] = (_silu(gate) * up).astype(BF16)


def _glu(x, g, mod, wg, wu, dims, tm, tf):
    m_rows, d = x.shape
    f = wg.shape[1]
    seq_tiles = dims["seq"] // tm

    def sel(i):
        return jnp.minimum(i // seq_tiles, dims["batch"])

    return pl.pallas_call(
        _glu_kernel,
        grid=(m_rows // tm, f // tf),
        in_specs=[pl.BlockSpec((tm, d), lambda i, j: (i, 0)),
                  pl.BlockSpec((1, d), lambda i, j: (0, 0)),
                  pl.BlockSpec((1, 6, d), lambda i, j: (sel(i), 0, 0)),
                  pl.BlockSpec((d, tf), lambda i, j: (0, j)),
                  pl.BlockSpec((d, tf), lambda i, j: (0, j))],
        out_specs=pl.BlockSpec((tm, tf), lambda i, j: (i, j)),
        out_shape=jax.ShapeDtypeStruct((m_rows, f), BF16),
        scratch_shapes=[pltpu.VMEM((tm, d), BF16)],
        compiler_params=_cparams(2),
        name="ffn_glu",
    )(x, g.reshape(1, d), mod, wg, wu)


def _mla_proj_kernel(x_ref, g_ref, mod_ref, wd_ref, qg_ref, kvg_ref, wuq_ref, wukv_ref,
                     tq_ref, tk_ref, q_ref, kv_ref, kr_ref, *, n_lat_tiles, q_lora, kv_lora, heads):
    i = pl.program_id(0)
    m = mod_ref[0]
    h = _norm_mod(x_ref[...], g_ref[...], m[0:1], m[1:2]).astype(BF16)
    t = _dot(h, wd_ref[...])
    ckv = _rms(t[:, q_lora:q_lora + kv_lora], kvg_ref[...]).astype(BF16)
    kv_ref[...] = _dot(ckv, wukv_ref[...]).astype(BF16)
    kr = t[:, q_lora + kv_lora:]
    kr_ref[...] = _apply_rope(kr, tk_ref, MLA_ROPE // 4).astype(BF16)

    @pl.when(i < n_lat_tiles)
    def _():
        cq = _rms(t[:, :q_lora], qg_ref[...]).astype(BF16)
        q = _dot(cq, wuq_ref[...])
        scale = (MLA_NOPE + MLA_ROPE) ** -0.5
        for hh in range(heads):
            c0 = hh * 2 * LANES
            q_ref[:, c0:c0 + LANES] = (q[:, c0:c0 + LANES] * scale).astype(BF16)
            q_ref[:, c0 + LANES:c0 + 2 * LANES] = _apply_rope(
                q[:, c0 + LANES:c0 + 2 * LANES], tq_ref, MLA_ROPE // 4).astype(BF16)


def _mla_proj(x, g, mod, wd, qg, kvg, wuq, wukv, tab_q, tab_k, dims, tm):
    m_rows, d = x.shape
    seq, lat = dims["seq"], dims["lat"]
    q_lora, kv_lora = qg.shape[0], kvg.shape[0]
    heads = wukv.shape[1] // (MLA_NOPE + MLA_V)
    lat_tiles, seq_tiles = lat // tm, seq // tm

    def sel(i):
        return jnp.minimum(i // seq_tiles, dims["batch"])

    def tab_map(i):
        return (0, jnp.where(i < lat_tiles, i % seq_tiles, seq_tiles), 0)

    kern = functools.partial(_mla_proj_kernel, n_lat_tiles=lat_tiles, q_lora=q_lora, kv_lora=kv_lora,
                             heads=heads)
    const = lambda i: (0, 0)
    return pl.pallas_call(
        kern,
        grid=(m_rows // tm,),
        in_specs=[pl.BlockSpec((tm, d), lambda i: (i, 0)),
                  pl.BlockSpec((1, d), const),
                  pl.BlockSpec((1, 6, d), lambda i: (sel(i), 0, 0)),
                  pl.BlockSpec(wd.shape, const),
                  pl.BlockSpec((1, q_lora), const),
                  pl.BlockSpec((1, kv_lora), const),
                  pl.BlockSpec(wuq.shape, const),
                  pl.BlockSpec(wukv.shape, const),
                  pl.BlockSpec((3, tm, LANES), tab_map),
                  pl.BlockSpec((3, tm, LANES), tab_map)],
        out_specs=[pl.BlockSpec((tm, wuq.shape[1]), lambda i: (jnp.minimum(i, lat_tiles - 1), 0)),
                   pl.BlockSpec((tm, wukv.shape[1]), lambda i: (i, 0)),
                   pl.BlockSpec((tm, LANES), lambda i: (i, 0))],
        out_shape=[jax.ShapeDtypeStruct((lat, wuq.shape[1]), BF16),
                   jax.ShapeDtypeStruct((m_rows, wukv.shape[1]), BF16),
                   jax.ShapeDtypeStruct((m_rows, LANES), BF16)],
        compiler_params=_cparams(1),
        name="od_mla_proj",
    )(x, g.reshape(1, d), mod, wd, qg.reshape(1, q_lora), kvg.reshape(1, kv_lora), wuq, wukv, tab_q, tab_k)


def _mla_attn_kernel(q_ref, knc_ref, knl_ref, krc_ref, krl_ref, vc_ref, vl_ref, o_ref, k_scr, v_scr,
                     *, n_ctx):
    @pl.when(pl.program_id(2) == 0)
    def _():
        k_scr[0:n_ctx, 0:LANES] = knc_ref[...]
        k_scr[n_ctx:, 0:LANES] = knl_ref[...]
        k_scr[0:n_ctx, LANES:] = krc_ref[...]
        k_scr[n_ctx:, LANES:] = krl_ref[...]
        v_scr[0:n_ctx] = vc_ref[...]
        v_scr[n_ctx:] = vl_ref[...]

    s = _dot_nt(q_ref[...], k_scr[...])
    e = jnp.exp(s - jnp.max(s, axis=-1, keepdims=True))
    inv = 1.0 / jnp.sum(e, axis=-1, keepdims=True)
    o_ref[...] = (_dot(e.astype(BF16), v_scr[...]) * inv).astype(BF16)


def _mla_attn(q, kv, kr, dims, tq):
    batch, seq, ctx, lat = dims["batch"], dims["seq"], dims["ctx"], dims["lat"]
    heads = q.shape[1] // (2 * LANES)
    nq = seq // tq
    ctx_blk0 = lat // ctx
    return pl.pallas_call(
        functools.partial(_mla_attn_kernel, n_ctx=ctx),
        grid=(batch, heads, nq),
        in_specs=[pl.BlockSpec((tq, 2 * LANES), lambda b, h, i: (b * nq + i, h)),
                  pl.BlockSpec((ctx, LANES), lambda b, h, i: (ctx_blk0 + b, 2 * h)),
                  pl.BlockSpec((seq, LANES), lambda b, h, i: (b, 2 * h)),
                  pl.BlockSpec((ctx, LANES), lambda b, h, i: (ctx_blk0 + b, 0)),
                  pl.BlockSpec((seq, LANES), lambda b, h, i: (b, 0)),
                  pl.BlockSpec((ctx, LANES), lambda b, h, i: (ctx_blk0 + b, 2 * h + 1)),
                  pl.BlockSpec((seq, LANES), lambda b, h, i: (b, 2 * h + 1))],
        out_specs=pl.BlockSpec((tq, LANES), lambda b, h, i: (b * nq + i, h)),
        out_shape=jax.ShapeDtypeStruct((lat, heads * LANES), BF16),
        scratch_shapes=[pltpu.VMEM((ctx + seq, 2 * LANES), BF16), pltpu.VMEM((ctx + seq, LANES), BF16)],
        compiler_params=_cparams(3),
        name="od_mla_attn",
    )(q, kv, kv, kr, kr, kv, kv)


def _router_kernel(x_ref, g_ref, mod_ref, r_ref, h_ref, idx_ref, w_ref):
    m = mod_ref[0]
    h = _norm_mod(x_ref[...], g_ref[...], m[3:4], m[4:5])
    h_ref[...] = h
    r = r_ref[...]
    h_hi = h.astype(BF16)
    h_lo = (h - h_hi.astype(F32)).astype(BF16)
    r_hi = r.astype(BF16)
    r_lo = (r - r_hi.astype(F32)).astype(BF16)
    logits = _dot(h_hi, r_hi) + (_dot(h_lo, r_hi) + _dot(h_hi, r_lo))
    lane = lax.broadcasted_iota(jnp.int32, logits.shape, 1)
    lane_f = lane.astype(F32)
    neg = jnp.float32(-jnp.inf)
    logits = jnp.where(lane < N_EXPERTS, logits, neg)
    m1 = jnp.max(logits, axis=-1, keepdims=True)
    i1 = jnp.min(jnp.where(logits == m1, lane_f, float(LANES)), axis=-1, keepdims=True)
    rest = jnp.where(lane_f == i1, neg, logits)
    m2 = jnp.max(rest, axis=-1, keepdims=True)
    i2 = jnp.min(jnp.where(rest == m2, lane_f, float(LANES)), axis=-1, keepdims=True)
    e2 = jnp.exp(m2 - m1)
    w1 = 1.0 / (1.0 + e2)
    w2 = e2 / (1.0 + e2)
    idx_ref[...] = jnp.where(lane == 0, i1, jnp.where(lane == 1, i2, 0.0)).astype(jnp.int32)
    w_ref[...] = jnp.where(lane == 0, w1, jnp.where(lane == 1, w2, 0.0))


def _router(x, g, mod, router_pad, dims, tm):
    lat, d = x.shape
    seq_tiles = dims["seq"] // tm
    return pl.pallas_call(
        _router_kernel,
        grid=(lat // tm,),
        in_specs=[pl.BlockSpec((tm, d), lambda i: (i, 0)),
                  pl.BlockSpec((1, d), lambda i: (0, 0)),
                  pl.BlockSpec((1, 6, d), lambda i: (i // seq_tiles, 0, 0)),
                  pl.BlockSpec((d, LANES), lambda i: (0, 0))],
        out_specs=[pl.BlockSpec((tm, d), lambda i: (i, 0)),
                   pl.BlockSpec((tm, LANES), lambda i: (i, 0)),
                   pl.BlockSpec((tm, LANES), lambda i: (i, 0))],
        out_shape=[jax.ShapeDtypeStruct((lat, d), F32),
                   jax.ShapeDtypeStruct((lat, LANES), jnp.int32),
                   jax.ShapeDtypeStruct((lat, LANES), F32)],
        compiler_params=_cparams(1),
        name="moe_router",
    )(x, g.reshape(1, d), mod, router_pad)


def _moe_plan(top_idx, n_items):
    e_flat = top_idx.reshape(-1)
    n_assign = e_flat.shape[0]
    onehot = (e_flat[:, None] == jnp.arange(N_EXPERTS, dtype=jnp.int32)[None, :]).astype(jnp.int32)
    csum = jnp.cumsum(onehot, axis=0)
    counts = csum[-1]
    rank = jnp.sum(csum * onehot, axis=1) - 1
    blocks = (counts + MOE_BLOCK - 1) // MOE_BLOCK
    blk_end = jnp.cumsum(blocks)
    blk_start = blk_end - blocks
    total = blk_end[-1]
    dest = blk_start[e_flat] * MOE_BLOCK + rank
    row_tok = jnp.zeros((n_items * MOE_BLOCK,), jnp.int32).at[dest].set(
        jnp.arange(n_assign, dtype=jnp.int32) // TOP_K)
    p = jnp.arange(n_items, dtype=jnp.int32)
    pc = jnp.minimum(p, total - 1)
    item_e = jnp.sum((pc[:, None] >= blk_end[None, :]).astype(jnp.int32), axis=1)
    rows_left = counts[item_e] - (pc - blk_start[item_e]) * MOE_BLOCK
    nact = jnp.clip((rows_left + MOE_SUB - 1) // MOE_SUB, 0, MOE_BLOCK // MOE_SUB)
    nact = jnp.where(p < total, nact, 0).astype(jnp.int32)
    out_blk = jnp.where(p < total, p, n_items).astype(jnp.int32)
    return dest.astype(jnp.int32), row_tok, item_e.astype(jnp.int32), pc.astype(jnp.int32), out_blk, nact


def _gather_kernel(tok_ref, nact_ref, h_ref, o_ref, sem):
    p = pl.program_id(0)
    n = nact_ref[p] * MOE_SUB
    base = p * MOE_BLOCK

    def start(r, c):
        t = tok_ref[base + r]
        pltpu.make_async_copy(h_ref.at[pl.ds(t, 1)], o_ref.at[pl.ds(base + r, 1)], sem).start()
        return c

    lax.fori_loop(0, n, start, 0)

    def wait(r, c):
        pltpu.make_async_copy(h_ref.at[pl.ds(0, 1)], o_ref.at[pl.ds(base, 1)], sem).wait()
        return c

    lax.fori_loop(0, n, wait, 0)


def _moe_gather(h, row_tok, nact, n_items):
    d = h.shape[1]
    return pl.pallas_call(
        _gather_kernel,
        grid_spec=pltpu.PrefetchScalarGridSpec(
            num_scalar_prefetch=2,
            grid=(n_items,),
            in_specs=[pl.BlockSpec(memory_space=pl.ANY)],
            out_specs=pl.BlockSpec(memory_space=pl.ANY),
            scratch_shapes=[pltpu.SemaphoreType.DMA(())]),
        out_shape=jax.ShapeDtypeStruct((n_items * MOE_BLOCK, d), F32),
        compiler_params=_cparams(1),
        name="moe_gather",
    )(row_tok, nact, h)


def _moe_glu_kernel(e_ref, blk_ref, oblk_ref, nact_ref, h_ref, wg_ref, wu_ref, o_ref, hb_ref):
    p = pl.program_id(0)
    nact = nact_ref[p]
    first = pl.program_id(1) == 0
    wg = wg_ref[0].astype(BF16)
    wu = wu_ref[0].astype(BF16)
    for s in range(MOE_BLOCK // MOE_SUB):
        rows = slice(s * MOE_SUB, (s + 1) * MOE_SUB)

        @pl.when(jnp.logical_and(first, s < nact))
        def _():
            hb_ref[rows] = h_ref[rows].astype(BF16)

        @pl.when(s < nact)
        def _():
            h = hb_ref[rows]
            o_ref[rows] = (_silu(_dot(h, wg)) * _dot(h, wu)).astype(BF16)

        @pl.when(s >= nact)
        def _():
            o_ref[rows] = jnp.zeros((MOE_SUB, o_ref.shape[1]), BF16)


def _moe_glu(hs, wg, wu, plan, n_items, tf):
    item_e, in_blk, out_blk, nact = plan
    d, f = wg.shape[1], wg.shape[2]
    nj = f // tf

    def w_map(p, j, e, b, ob, na):
        return (e[p], 0, jnp.where(na[p] > 0, j, nj - 1))

    def o_map(p, j, e, b, ob, na):
        return (ob[p], jnp.where(na[p] > 0, j, 0))

    return pl.pallas_call(
        _moe_glu_kernel,
        grid_spec=pltpu.PrefetchScalarGridSpec(
            num_scalar_prefetch=4,
            grid=(n_items, nj),
            in_specs=[pl.BlockSpec((MOE_BLOCK, d), lambda p, j, e, b, ob, na: (b[p], 0)),
                      pl.BlockSpec((1, d, tf), w_map),
                      pl.BlockSpec((1, d, tf), w_map)],
            out_specs=pl.BlockSpec((MOE_BLOCK, tf), o_map),
            scratch_shapes=[pltpu.VMEM((MOE_BLOCK, d), BF16)]),
        out_shape=jax.ShapeDtypeStruct(((n_items + 1) * MOE_BLOCK, f), BF16),
        compiler_params=_cparams(2),
        name="moe_glu",
    )(item_e, in_blk, out_blk, nact, hs, wg, wu)


def _moe_down_kernel(e_ref, blk_ref, oblk_ref, nact_ref, a_ref, wd_ref, o_ref):
    nact = nact_ref[pl.program_id(0)]
    wd = wd_ref[0].astype(BF16)
    for s in range(MOE_BLOCK // MOE_SUB):
        rows = slice(s * MOE_SUB, (s + 1) * MOE_SUB)

        @pl.when(s < nact)
        def _():
            o_ref[rows] = _dot(a_ref[rows], wd)

        @pl.when(s >= nact)
        def _():
            o_ref[rows] = jnp.zeros((MOE_SUB, o_ref.shape[1]), F32)


def _moe_down(a, wd, plan, n_items, tn):
    item_e, in_blk, out_blk, nact = plan
    f, d = wd.shape[1], wd.shape[2]
    nj = d // tn

    def w_map(p, j, e, b, ob, na):
        return (e[p], 0, jnp.where(na[p] > 0, j, nj - 1))

    def o_map(p, j, e, b, ob, na):
        return (ob[p], jnp.where(na[p] > 0, j, 0))

    return pl.pallas_call(
        _moe_down_kernel,
        grid_spec=pltpu.PrefetchScalarGridSpec(
            num_scalar_prefetch=4,
            grid=(n_items, nj),
            in_specs=[pl.BlockSpec((MOE_BLOCK, f), lambda p, j, e, b, ob, na: (b[p], 0)),
                      pl.BlockSpec((1, f, tn), w_map)],
            out_specs=pl.BlockSpec((MOE_BLOCK, tn), o_map)),
        out_shape=jax.ShapeDtypeStruct(((n_items + 1) * MOE_BLOCK, d), F32),
        compiler_params=_cparams(2),
        name="moe_down",
    )(item_e, in_blk, out_blk, nact, a, wd)


def _combine_kernel(dest_ref, x_ref, mod_ref, w_ref, fg_ref, y_ref, o_ref, buf, sem, *, tm):
    i = pl.program_id(0)

    def copy(r, k, row):
        return pltpu.make_async_copy(y_ref.at[pl.ds(row, 1)], buf.at[k, pl.ds(r, 1)], sem)

    def start(r, c):
        for k in range(TOP_K):
            copy(r, k, dest_ref[(i * tm + r) * TOP_K + k]).start()
        return c

    lax.fori_loop(0, tm, start, 0)

    def wait(r, c):
        for k in range(TOP_K):
            copy(r, k, 0).wait()
        return c

    lax.fori_loop(0, tm, wait, 0)
    w = w_ref[...]
    moe = w[:, 0:1] * buf[0] + w[:, 1:2] * buf[1]
    m = mod_ref[0]
    o_ref[...] = _rms(x_ref[...] + m[5:6] * moe, fg_ref[...])


def _moe_combine(dest, x, mod, top_w, final_g, ys, dims, tm):
    lat, d = x.shape
    seq_tiles = dims["seq"] // tm
    return pl.pallas_call(
        functools.partial(_combine_kernel, tm=tm),
        grid_spec=pltpu.PrefetchScalarGridSpec(
            num_scalar_prefetch=1,
            grid=(lat // tm,),
            in_specs=[pl.BlockSpec((tm, d), lambda i, dr: (i, 0)),
                      pl.BlockSpec((1, 6, d), lambda i, dr: (i // seq_tiles, 0, 0)),
                      pl.BlockSpec((tm, LANES), lambda i, dr: (i, 0)),
                      pl.BlockSpec((1, d), lambda i, dr: (0, 0)),
                      pl.BlockSpec(memory_space=pl.ANY)],
            out_specs=pl.BlockSpec((tm, d), lambda i, dr: (i, 0)),
            scratch_shapes=[pltpu.VMEM((TOP_K, tm, d), F32), pltpu.SemaphoreType.DMA(())]),
        out_shape=jax.ShapeDtypeStruct((lat, d), F32),
        compiler_params=_cparams(1),
        name="moe_combine",
    )(dest, x, mod, top_w, final_g.reshape(1, d), ys)


def kernel(x, c, ctx, c_ctx, ada_w, ada_b, norm1_g, norm2_g, ev_w_in, ev_w_out, ev_lambda, ev_subln_g, od_w_dq, od_q_norm_g, od_w_uq, od_w_dkv, od_kv_norm_g, od_w_ukv, od_w_o, ffn_w_gate, ffn_w_up, ffn_w_down, moe_router, moe_w_gate, moe_w_up, moe_w_down, final_norm_g):
    batch, seq, d = x.shape
    n_ctx = ctx.shape[1]
    depth = ada_w.shape[0]
    assert depth == 2 and batch < MOD_ROWS and seq % GRID_W == 0
    lat = batch * seq
    dims = dict(batch=batch, seq=seq, ctx=n_ctx, lat=lat)
    tm = 1024
    assert seq % tm == 0 and (batch * n_ctx) % tm == 0

    cond = jnp.concatenate([c, c_ctx[None, :], jnp.zeros((MOD_ROWS - batch - 1, d), F32)], axis=0)
    mod = _ada(cond, ada_w, ada_b).reshape(depth, MOD_ROWS, 6, d)
    xs = jnp.concatenate([x.reshape(lat, d), ctx.reshape(batch * n_ctx, d)], axis=0)

    lam_init = 0.8 - 0.6 * math.exp(-0.3 * 0)
    fw = ev_w_in.shape[2] // 4
    tabs_ev = jnp.asarray(np.stack([
        _rope_tables(seq, tm, DIFF_HEAD_DIM // 4, DIFF_HEAD_DIM ** -0.5),
        _rope_tables(seq, tm, DIFF_HEAD_DIM // 4, 1.0)]))
    qkvf = _inproj(xs, norm1_g[0], mod[0], ev_w_in[0], tabs_ev, dims, tm, 512)
    o_attn = _diff_attn(qkvf, ev_lambda[0], ev_subln_g[0], lam_init, dims, n_ctx)

    gw = fw // FOURIER_GROUPS
    cc_np, sc_np = _dft_cos_sin(gw)
    cc = jnp.asarray(cc_np.astype(np.float32)).astype(BF16)
    sc = jnp.asarray(sc_np.astype(np.float32)).astype(BF16)
    cn_np, sn_np = _dft_cos_sin(seq)
    cs_lat = jnp.asarray(np.concatenate([cn_np, -sn_np], axis=1).astype(np.float32)).astype(BF16)
    cx_np, sx_np = _dft_cos_sin(n_ctx)
    cs_ctx = jnp.asarray(np.concatenate([cx_np, -sx_np], axis=1).astype(np.float32)).astype(BF16)
    fm = _fourier(qkvf, None, seq, 0, batch, fw, cc, sc, cs_lat, "ev_fourier_lat")
    fm = _fourier(qkvf, fm, n_ctx, lat // n_ctx, batch, fw, cc, sc, cs_ctx, "ev_fourier_ctx")

    m_rows = xs.shape[0]
    xs = _mm_res([fm, o_attn], ev_w_out[0], xs, mod[0], 2, dims, m_rows, tm, 512, "ev_outproj")
    act = _glu(xs, norm2_g[0], mod[0], ffn_w_gate[0], ffn_w_up[0], dims, tm, 512)
    xs = _mm_res([act], ffn_w_down[0], xs, mod[0], 5, dims, m_rows, tm, 256, "ffn_down")

    heads = od_w_ukv.shape[2] // (MLA_NOPE + MLA_V)
    q_lora = od_w_dq.shape[2]
    wd_cat = jnp.concatenate(
        [od_w_dq[0], od_w_dkv[0], jnp.zeros((d, LANES - MLA_ROPE), F32)], axis=1).astype(BF16)
    wuq = jnp.pad(od_w_uq[0].reshape(q_lora, heads, MLA_NOPE + MLA_ROPE),
                  ((0, 0), (0, 0), (0, 2 * LANES - MLA_NOPE - MLA_ROPE))).reshape(q_lora, heads * 2 * LANES)
    tm_mla = 256
    tab_q = jnp.asarray(_rope_tables(seq, tm_mla, MLA_ROPE // 4, (MLA_NOPE + MLA_ROPE) ** -0.5))
    tab_k = jnp.asarray(_rope_tables(seq, tm_mla, MLA_ROPE // 4, 1.0))
    q, kv, kr = _mla_proj(xs, norm1_g[1], mod[1], wd_cat, od_q_norm_g[0], od_kv_norm_g[0],
                          wuq.astype(BF16), od_w_ukv[0].astype(BF16), tab_q, tab_k, dims, tm_mla)
    o_mla = _mla_attn(q, kv, kr, dims, 512)
    xl = _mm_res([o_mla], od_w_o[0], xs, mod[1], 2, dims, lat, tm, 512, "od_outproj")

    router_pad = jnp.pad(moe_router[0], ((0, 0), (0, LANES - N_EXPERTS)))
    h2, top_idx, top_w = _router(xl, norm2_g[1], mod[1], router_pad, dims, 512)
    n_items = lat * TOP_K // MOE_BLOCK + N_EXPERTS
    dest, row_tok, item_e, in_blk, out_blk, nact = _moe_plan(top_idx[:, :TOP_K], n_items)
    plan = (item_e, in_blk, out_blk, nact)
    hs = _moe_gather(h2, row_tok, nact, n_items)
    act = _moe_glu(hs, moe_w_gate[0], moe_w_up[0], plan, n_items, 512)
    ys = _moe_down(act, moe_w_down[0], plan, n_items, 256)
    out = _moe_combine(dest, xl, mod[1], top_w, final_norm_g, ys, dims, 256)
    return out.reshape(batch, seq, d)
```

```python
import functools
import math

import numpy as np
import jax
import jax.numpy as jnp
from jax import lax
from jax.experimental import pallas as pl
from jax.experimental.pallas import tpu as pltpu

F32 = jnp.float32
BF16 = jnp.bfloat16

GRID_W = 64
NORM_EPS = 1e-6
ROPE_BASE = 10000.0
FOURIER_GROUPS = 4
DIFF_HEAD_DIM = 128
MLA_NOPE = 128
MLA_ROPE = 64
MLA_V = 128
N_EXPERTS = 8
TOP_K = 2

LANES = 128
MOD_ROWS = 8
VMEM_LIMIT = 56 * 1024 * 1024
MOE_BLOCK = 1024
MOE_SUB = 256
GATHER_UNROLL = 8


def _cparams(n_axes):
    return pltpu.CompilerParams(dimension_semantics=("arbitrary",) * n_axes,
                                vmem_limit_bytes=VMEM_LIMIT)


def _rms(x, g):
    return x * lax.rsqrt(jnp.mean(x * x, axis=-1, keepdims=True) + NORM_EPS) * g


def _norm_mod(x, g, shift, scale):
    return _rms(x, g) * (1.0 + scale) + shift


def _silu(x):
    return x * (1.0 / (1.0 + jnp.exp(-x)))


def _dot(a, b):
    return jnp.dot(a, b, preferred_element_type=F32)


def _dot_nt(a, b):
    return lax.dot_general(a, b, (((1,), (1,)), ((), ())), preferred_element_type=F32)


def _rope_tables(seq, extra_rows, chunk, scale):
    n = np.arange(seq)
    row, col = n // GRID_W, n % GRID_W
    lane = np.arange(LANES)
    a = 2 * chunk
    inv = ROPE_BASE ** (-np.arange(0, a, 2, dtype=np.float64) / a)
    used = lane < 4 * chunk
    freq = inv[lane % chunk]
    pos = np.where(lane[None, :] < 2 * chunk, row[:, None], col[:, None]).astype(np.float64)
    ang = pos * freq[None, :]
    first = (lane // chunk) % 2 == 0
    cos = np.where(used[None, :], np.cos(ang), 0.0)
    sin = np.where(used[None, :], np.sin(ang), 0.0)
    s1 = np.where(first[None, :], -sin, 0.0)
    s2 = np.where(first[None, :], 0.0, sin)
    ident = np.zeros((3, extra_rows, LANES))
    ident[0] = used[None, :].astype(np.float64)
    tab = np.concatenate([np.stack([cos, s1, s2]), ident], axis=1) * scale
    return tab.astype(np.float32)


def _apply_rope(x, tab_ref, chunk):
    return (x * tab_ref[0] + pltpu.roll(x, LANES - chunk, 1) * tab_ref[1]
            + pltpu.roll(x, chunk, 1) * tab_ref[2])


def _dft_cos_sin(n):
    k = np.arange(n)
    ang = 2.0 * np.pi * ((k[:, None] * k[None, :]) % n) / n
    return np.cos(ang), np.sin(ang)


def _ada_kernel(s_ref, w_ref, b_ref, o_ref):
    s = _silu(s_ref[...]).astype(BF16)
    o_ref[0] = _dot(s, w_ref[0].astype(BF16)) + b_ref[0]


def _ada(cond, ada_w, ada_b):
    depth, d, n = ada_w.shape
    tn = 1024
    return pl.pallas_call(
        _ada_kernel,
        grid=(depth, n // tn),
        in_specs=[pl.BlockSpec((MOD_ROWS, d), lambda i, j: (0, 0)),
                  pl.BlockSpec((1, d, tn), lambda i, j: (i, 0, j)),
                  pl.BlockSpec((1, 1, tn), lambda i, j: (i, 0, j))],
        out_specs=pl.BlockSpec((1, MOD_ROWS, tn), lambda i, j: (i, 0, j)),
        out_shape=jax.ShapeDtypeStruct((depth, MOD_ROWS, n), F32),
        compiler_params=_cparams(2),
        name="ada",
    )(cond, ada_w, ada_b.reshape(depth, 1, n))


def _inproj_kernel(x_ref, g_ref, mod_ref, w_ref, tab_ref, o_ref, h_ref):
    j = pl.program_id(1)
    quarter = pl.num_programs(1) // 4

    @pl.when(j == 0)
    def _():
        m = mod_ref[0]
        h_ref[...] = _norm_mod(x_ref[...], g_ref[...], m[0:1], m[1:2]).astype(BF16)

    res = _dot(h_ref[...], w_ref[...].astype(BF16))
    is_rope = jnp.logical_and(j >= quarter, j < 3 * quarter)

    @pl.when(is_rope)
    def _():
        for c in range(res.shape[1] // LANES):
            sl = slice(c * LANES, (c + 1) * LANES)
            o_ref[:, sl] = _apply_rope(res[:, sl], tab_ref.at[0], DIFF_HEAD_DIM // 4).astype(BF16)

    @pl.when(jnp.logical_not(is_rope))
    def _():
        o_ref[...] = res.astype(BF16)


def _inproj(x, g, mod, w, tabs, dims, tm, tn):
    m_rows, d = x.shape
    seq, lat = dims["seq"], dims["lat"]
    n = w.shape[1]
    nj = n // tn
    assert nj % 4 == 0
    lat_tiles, seq_tiles = lat // tm, seq // tm

    def sel(i):
        return jnp.minimum(i // seq_tiles, dims["batch"])

    def tab_map(i, j):
        return (jnp.where(j >= nj // 2, 1, 0), 0, jnp.where(i < lat_tiles, i % seq_tiles, seq_tiles), 0)

    return pl.pallas_call(
        _inproj_kernel,
        grid=(m_rows // tm, nj),
        in_specs=[pl.BlockSpec((tm, d), lambda i, j: (i, 0)),
                  pl.BlockSpec((1, d), lambda i, j: (0, 0)),
                  pl.BlockSpec((1, 6, d), lambda i, j: (sel(i), 0, 0)),
                  pl.BlockSpec((d, tn), lambda i, j: (0, j)),
                  pl.BlockSpec((1, 3, tm, LANES), tab_map)],
        out_specs=pl.BlockSpec((tm, tn), lambda i, j: (i, j)),
        out_shape=jax.ShapeDtypeStruct((m_rows, n), BF16),
        scratch_shapes=[pltpu.VMEM((tm, d), BF16)],
        compiler_params=_cparams(2),
        name="ev_inproj",
    )(x, g.reshape(1, d), mod, w, tabs)


def _diff_attn_kernel(lam_ref, q_ref, kc_ref, kl_ref, vc_ref, vl_ref, g_ref, o_ref, k_scr, v_scr,
                      *, lam_init, n_lat_tiles, n_ctx):
    i = pl.program_id(2)
    hd = DIFF_HEAD_DIM

    @pl.when(i == 0)
    def _():
        k_scr[0:n_ctx] = kc_ref[...]
        k_scr[n_ctx:] = kl_ref[...]
        v_scr[0:n_ctx] = vc_ref[...]
        v_scr[n_ctx:] = vl_ref[...]

    lv = lam_ref[...]
    lam = (jnp.exp(jnp.sum(lv[0:1] * lv[1:2], axis=-1, keepdims=True))
           - jnp.exp(jnp.sum(lv[2:3] * lv[3:4], axis=-1, keepdims=True)) + lam_init)
    q = q_ref[...]

    def attend(n_keys):
        e, inv = [], []
        for c in range(2):
            s = _dot_nt(q[:, c * hd:(c + 1) * hd], k_scr[0:n_keys, c * hd:(c + 1) * hd])
            ex = jnp.exp(s - jnp.max(s, axis=-1, keepdims=True))
            e.append(ex)
            inv.append(1.0 / jnp.sum(ex, axis=-1, keepdims=True))
        a = (e[0] * inv[0] - e[1] * (lam * inv[1])).astype(BF16)
        o = _dot(a, v_scr[0:n_keys])
        o_ref[...] = (_rms(o, g_ref[...]) * (1.0 - lam_init)).astype(BF16)

    @pl.when(i < n_lat_tiles)
    def _():
        attend(k_scr.shape[0])

    @pl.when(i >= n_lat_tiles)
    def _():
        attend(n_ctx)


def _diff_attn(qkvf, lam_vec, subln_g, lam_init, dims, tq):
    m_rows = qkvf.shape[0]
    batch, seq, ctx = dims["batch"], dims["seq"], dims["ctx"]
    hw = 2 * DIFF_HEAD_DIM
    width = qkvf.shape[1] // 4
    heads = width // hw
    nq = seq // tq
    assert ctx == tq
    ctx_blk0 = batch * seq // ctx
    qcol, kcol, vcol = width // hw, 2 * width // hw, 3 * width // hw

    def q_map(b, h, i):
        return (jnp.where(i < nq, b * nq + i, ctx_blk0 + b), qcol + h)

    def o_map(b, h, i):
        return (jnp.where(i < nq, b * nq + i, ctx_blk0 + b), h)

    kern = functools.partial(_diff_attn_kernel, lam_init=lam_init, n_lat_tiles=nq, n_ctx=ctx)
    return pl.pallas_call(
        kern,
        grid=(batch, heads, nq + 1),
        in_specs=[pl.BlockSpec((4, DIFF_HEAD_DIM), lambda b, h, i: (0, 0)),
                  pl.BlockSpec((tq, hw), q_map),
                  pl.BlockSpec((ctx, hw), lambda b, h, i: (ctx_blk0 + b, kcol + h)),
                  pl.BlockSpec((seq, hw), lambda b, h, i: (b, kcol + h)),
                  pl.BlockSpec((ctx, hw), lambda b, h, i: (ctx_blk0 + b, vcol + h)),
                  pl.BlockSpec((seq, hw), lambda b, h, i: (b, vcol + h)),
                  pl.BlockSpec((1, hw), lambda b, h, i: (0, 0))],
        out_specs=pl.BlockSpec((tq, hw), o_map),
        out_shape=jax.ShapeDtypeStruct((m_rows, width), BF16),
        scratch_shapes=[pltpu.VMEM((ctx + seq, hw), BF16), pltpu.VMEM((ctx + seq, hw), BF16)],
        compiler_params=_cparams(3),
        name="ev_diff_attn",
    )(lam_vec, qkvf, qkvf, qkvf, qkvf, qkvf, subln_g.reshape(1, hw))


def _fourier_kernel(u_ref, cc_ref, sc_ref, cs_ref, *rest, n, norm):
    o_ref, ab_ref = rest[-2], rest[-1]
    u = u_ref[...]
    ab_ref[0:n] = _dot(u, cc_ref[...]).astype(BF16)
    ab_ref[n:] = _dot(u, sc_ref[...]).astype(BF16)
    o_ref[...] = (_dot(cs_ref[...], ab_ref[...]) * norm).astype(BF16)


def _fourier(qkvf, prev, n, row_blk0, batch, width, cc, sc, cs, name):
    m_rows = qkvf.shape[0]
    gw = width // FOURIER_GROUPS
    kern = functools.partial(_fourier_kernel, n=n, norm=1.0 / math.sqrt(n * gw))
    in_specs = [pl.BlockSpec((n, gw), lambda b, g: (row_blk0 + b, g)),
                pl.BlockSpec((gw, gw), lambda b, g: (0, 0)),
                pl.BlockSpec((gw, gw), lambda b, g: (0, 0)),
                pl.BlockSpec((n, 2 * n), lambda b, g: (0, 0), pipeline_mode=pl.Buffered(1))]
    args = [qkvf, cc, sc, cs]
    aliases = {}
    if prev is not None:
        in_specs.append(pl.BlockSpec(memory_space=pl.ANY))
        args.append(prev)
        aliases = {4: 0}
    return pl.pallas_call(
        kern,
        grid=(batch, FOURIER_GROUPS),
        in_specs=in_specs,
        out_specs=pl.BlockSpec((n, gw), lambda b, g: (row_blk0 + b, g)),
        out_shape=jax.ShapeDtypeStruct((m_rows, width), BF16),
        scratch_shapes=[pltpu.VMEM((2 * n, gw), BF16)],
        input_output_aliases=aliases,
        compiler_params=_cparams(2),
        name=name,
    )(*args)


def _mm_res_kernel(*refs, n_a, gate_idx):
    a_refs, w_refs = refs[:n_a], refs[n_a:2 * n_a]
    x_ref, mod_ref, o_ref = refs[2 * n_a:]
    acc = _dot(a_refs[0][...], w_refs[0][...].astype(BF16))
    for a_ref, w_ref in zip(a_refs[1:], w_refs[1:]):
        acc = acc + _dot(a_ref[...], w_ref[...].astype(BF16))
    m = mod_ref[0]
    o_ref[...] = x_ref[...] + m[gate_idx:gate_idx + 1] * acc


def _mm_res(a_list, w, x, mod, gate_idx, dims, rows, tm, tn, name):
    d = w.shape[1]
    seq_tiles = dims["seq"] // tm
    n_a = len(a_list)

    def sel(i):
        return jnp.minimum(i // seq_tiles, dims["batch"])

    in_specs, w_args, k0 = [], [], 0
    for a in a_list:
        in_specs.append(pl.BlockSpec((tm, a.shape[1]), lambda i, j: (i, 0)))
    for a in a_list:
        ka = a.shape[1]
        assert k0 % ka == 0
        in_specs.append(pl.BlockSpec((ka, tn), lambda i, j, kb=k0 // ka: (kb, j)))
        w_args.append(w)
        k0 += ka
    assert k0 == w.shape[0]
    in_specs += [pl.BlockSpec((tm, tn), lambda i, j: (i, j)),
                 pl.BlockSpec((1, 6, tn), lambda i, j: (sel(i), 0, j))]
    return pl.pallas_call(
        functools.partial(_mm_res_kernel, n_a=n_a, gate_idx=gate_idx),
        grid=(rows // tm, d // tn),
        in_specs=in_specs,
        out_specs=pl.BlockSpec((tm, tn), lambda i, j: (i, j)),
        out_shape=jax.ShapeDtypeStruct((rows, d), F32),
        compiler_params=_cparams(2),
        name=name,
    )(*a_list, *w_args, x, mod)


def _glu_kernel(x_ref, g_ref, mod_ref, wg_ref, wu_ref, o_ref, h_ref):
    @pl.when(pl.program_id(1) == 0)
    def _():
        m = mod_ref[0]
        h_ref[...] = _norm_mod(x_ref[...], g_ref[...], m[3:4], m[4:5]).astype(BF16)

    h = h_ref[...]
    gate = _dot(h, wg_ref[...].astype(BF16))
    up = _dot(h, wu_ref[...].astype(BF16))
    o_ref[...] = (_silu(gate) * up).astype(BF16)


def _glu(x, g, mod, wg, wu, dims, tm, tf):
    m_rows, d = x.shape
    f = wg.shape[1]
    seq_tiles = dims["seq"] // tm

    def sel(i):
        return jnp.minimum(i // seq_tiles, dims["batch"])

    return pl.pallas_call(
        _glu_kernel,
        grid=(m_rows // tm, f // tf),
        in_specs=[pl.BlockSpec((tm, d), lambda i, j: (i, 0)),
                  pl.BlockSpec((1, d), lambda i, j: (0, 0)),
                  pl.BlockSpec((1, 6, d), lambda i, j: (sel(i), 0, 0)),
                  pl.BlockSpec((d, tf), lambda i, j: (0, j)),
                  pl.BlockSpec((d, tf), lambda i, j: (0, j))],
        out_specs=pl.BlockSpec((tm, tf), lambda i, j: (i, j)),
        out_shape=jax.ShapeDtypeStruct((m_rows, f), BF16),
        scratch_shapes=[pltpu.VMEM((tm, d), BF16)],
        compiler_params=_cparams(2),
        name="ffn_glu",
    )(x, g.reshape(1, d), mod, wg, wu)


def _mla_proj_kernel(x_ref, g_ref, mod_ref, wd_ref, qg_ref, kvg_ref, wuq_ref, wukv_ref,
                     tq_ref, tk_ref, q_ref, kv_ref, kr_ref, *, n_lat_tiles, q_lora, kv_lora, heads):
    i = pl.program_id(0)
    m = mod_ref[0]
    h = _norm_mod(x_ref[...], g_ref[...], m[0:1], m[1:2]).astype(BF16)
    t = _dot(h, wd_ref[...])
    ckv = _rms(t[:, q_lora:q_lora + kv_lora], kvg_ref[...]).astype(BF16)
    kv_ref[...] = _dot(ckv, wukv_ref[...]).astype(BF16)
    kr = t[:, q_lora + kv_lora:]
    kr_ref[...] = _apply_rope(kr, tk_ref, MLA_ROPE // 4).astype(BF16)

    @pl.when(i < n_lat_tiles)
    def _():
        cq = _rms(t[:, :q_lora], qg_ref[...]).astype(BF16)
        q = _dot(cq, wuq_ref[...])
        scale = (MLA_NOPE + MLA_ROPE) ** -0.5
        for hh in range(heads):
            c0 = hh * 2 * LANES
            q_ref[:, c0:c0 + LANES] = (q[:, c0:c0 + LANES] * scale).astype(BF16)
            q_ref[:, c0 + LANES:c0 + 2 * LANES] = _apply_rope(
                q[:, c0 + LANES:c0 + 2 * LANES], tq_ref, MLA_ROPE // 4).astype(BF16)


def _mla_proj(x, g, mod, wd, qg, kvg, wuq, wukv, tab_q, tab_k, dims, tm):
    m_rows, d = x.shape
    seq, lat = dims["seq"], dims["lat"]
    q_lora, kv_lora = qg.shape[0], kvg.shape[0]
    heads = wukv.shape[1] // (MLA_NOPE + MLA_V)
    lat_tiles, seq_tiles = lat // tm, seq // tm

    def sel(i):
        return jnp.minimum(i // seq_tiles, dims["batch"])

    def tab_map(i):
        return (0, jnp.where(i < lat_tiles, i % seq_tiles, seq_tiles), 0)

    kern = functools.partial(_mla_proj_kernel, n_lat_tiles=lat_tiles, q_lora=q_lora, kv_lora=kv_lora,
                             heads=heads)
    const = lambda i: (0, 0)
    return pl.pallas_call(
        kern,
        grid=(m_rows // tm,),
        in_specs=[pl.BlockSpec((tm, d), lambda i: (i, 0)),
                  pl.BlockSpec((1, d), const),
                  pl.BlockSpec((1, 6, d), lambda i: (sel(i), 0, 0)),
                  pl.BlockSpec(wd.shape, const),
                  pl.BlockSpec((1, q_lora), const),
                  pl.BlockSpec((1, kv_lora), const),
                  pl.BlockSpec(wuq.shape, const),
                  pl.BlockSpec(wukv.shape, const),
                  pl.BlockSpec((3, tm, LANES), tab_map),
                  pl.BlockSpec((3, tm, LANES), tab_map)],
        out_specs=[pl.BlockSpec((tm, wuq.shape[1]), lambda i: (jnp.minimum(i, lat_tiles - 1), 0)),
                   pl.BlockSpec((tm, wukv.shape[1]), lambda i: (i, 0)),
                   pl.BlockSpec((tm, LANES), lambda i: (i, 0))],
        out_shape=[jax.ShapeDtypeStruct((lat, wuq.shape[1]), BF16),
                   jax.ShapeDtypeStruct((m_rows, wukv.shape[1]), BF16),
                   jax.ShapeDtypeStruct((m_rows, LANES), BF16)],
        compiler_params=_cparams(1),
        name="od_mla_proj",
    )(x, g.reshape(1, d), mod, wd, qg.reshape(1, q_lora), kvg.reshape(1, kv_lora), wuq, wukv, tab_q, tab_k)


def _mla_attn_kernel(q_ref, knc_ref, knl_ref, krc_ref, krl_ref, vc_ref, vl_ref, o_ref, k_scr, v_scr,
                     *, n_ctx):
    @pl.when(pl.program_id(2) == 0)
    def _():
        k_scr[0:n_ctx, 0:LANES] = knc_ref[...]
        k_scr[n_ctx:, 0:LANES] = knl_ref[...]
        k_scr[0:n_ctx, LANES:] = krc_ref[...]
        k_scr[n_ctx:, LANES:] = krl_ref[...]
        v_scr[0:n_ctx] = vc_ref[...]
        v_scr[n_ctx:] = vl_ref[...]

    s = _dot_nt(q_ref[...], k_scr[...])
    e = jnp.exp(s - jnp.max(s, axis=-1, keepdims=True))
    inv = 1.0 / jnp.sum(e, axis=-1, keepdims=True)
    o_ref[...] = (_dot(e.astype(BF16), v_scr[...]) * inv).astype(BF16)


def _mla_attn(q, kv, kr, dims, tq):
    batch, seq, ctx, lat = dims["batch"], dims["seq"], dims["ctx"], dims["lat"]
    heads = q.shape[1] // (2 * LANES)
    nq = seq // tq
    ctx_blk0 = lat // ctx
    return pl.pallas_call(
        functools.partial(_mla_attn_kernel, n_ctx=ctx),
        grid=(batch, heads, nq),
        in_specs=[pl.BlockSpec((tq, 2 * LANES), lambda b, h, i: (b * nq + i, h)),
                  pl.BlockSpec((ctx, LANES), lambda b, h, i: (ctx_blk0 + b, 2 * h)),
                  pl.BlockSpec((seq, LANES), lambda b, h, i: (b, 2 * h)),
                  pl.BlockSpec((ctx, LANES), lambda b, h, i: (ctx_blk0 + b, 0)),
                  pl.BlockSpec((seq, LANES), lambda b, h, i: (b, 0)),
                  pl.BlockSpec((ctx, LANES), lambda b, h, i: (ctx_blk0 + b, 2 * h + 1)),
                  pl.BlockSpec((seq, LANES), lambda b, h, i: (b, 2 * h + 1))],
        out_specs=pl.BlockSpec((tq, LANES), lambda b, h, i: (b * nq + i, h)),
        out_shape=jax.ShapeDtypeStruct((lat, heads * LANES), BF16),
        scratch_shapes=[pltpu.VMEM((ctx + seq, 2 * LANES), BF16), pltpu.VMEM((ctx + seq, LANES), BF16)],
        compiler_params=_cparams(3),
        name="od_mla_attn",
    )(q, kv, kv, kr, kr, kv, kv)


def _router_kernel(x_ref, g_ref, mod_ref, r_ref, h_ref, idx_ref, w_ref):
    m = mod_ref[0]
    h = _norm_mod(x_ref[...], g_ref[...], m[3:4], m[4:5])
    h_ref[...] = h
    r = r_ref[...]
    h_hi = h.astype(BF16)
    h_lo = (h - h_hi.astype(F32)).astype(BF16)
    r_hi = r.astype(BF16)
    r_lo = (r - r_hi.astype(F32)).astype(BF16)
    logits = _dot(h_hi, r_hi) + (_dot(h_lo, r_hi) + _dot(h_hi, r_lo))
    lane = lax.broadcasted_iota(jnp.int32, logits.shape, 1)
    lane_f = lane.astype(F32)
    neg = jnp.float32(-jnp.inf)
    logits = jnp.where(lane < N_EXPERTS, logits, neg)
    m1 = jnp.max(logits, axis=-1, keepdims=True)
    i1 = jnp.min(jnp.where(logits == m1, lane_f, float(LANES)), axis=-1, keepdims=True)
    rest = jnp.where(lane_f == i1, neg, logits)
    m2 = jnp.max(rest, axis=-1, keepdims=True)
    i2 = jnp.min(jnp.where(rest == m2, lane_f, float(LANES)), axis=-1, keepdims=True)
    e2 = jnp.exp(m2 - m1)
    w1 = 1.0 / (1.0 + e2)
    w2 = e2 / (1.0 + e2)
    idx_ref[...] = jnp.where(lane == 0, i1, jnp.where(lane == 1, i2, 0.0)).astype(jnp.int32)
    w_ref[...] = jnp.where(lane == 0, w1, jnp.where(lane == 1, w2, 0.0))


def _router(x, g, mod, router_pad, dims, tm):
    lat, d = x.shape
    seq_tiles = dims["seq"] // tm
    return pl.pallas_call(
        _router_kernel,
        grid=(lat // tm,),
        in_specs=[pl.BlockSpec((tm, d), lambda i: (i, 0)),
                  pl.BlockSpec((1, d), lambda i: (0, 0)),
                  pl.BlockSpec((1, 6, d), lambda i: (i // seq_tiles, 0, 0)),
                  pl.BlockSpec((d, LANES), lambda i: (0, 0))],
        out_specs=[pl.BlockSpec((tm, d), lambda i: (i, 0)),
                   pl.BlockSpec((tm, LANES), lambda i: (i, 0)),
                   pl.BlockSpec((tm, LANES), lambda i: (i, 0))],
        out_shape=[jax.ShapeDtypeStruct((lat, d), F32),
                   jax.ShapeDtypeStruct((lat, LANES), jnp.int32),
                   jax.ShapeDtypeStruct((lat, LANES), F32)],
        compiler_params=_cparams(1),
        name="moe_router",
    )(x, g.reshape(1, d), mod, router_pad)


def _moe_plan(top_idx, n_items):
    e_flat = top_idx.reshape(-1)
    n_assign = e_flat.shape[0]
    onehot = (e_flat[:, None] == jnp.arange(N_EXPERTS, dtype=jnp.int32)[None, :]).astype(jnp.int32)
    csum = jnp.cumsum(onehot, axis=0)
    counts = csum[-1]
    rank = jnp.sum(csum * onehot, axis=1) - 1
    blocks = (counts + MOE_BLOCK - 1) // MOE_BLOCK
    blk_end = jnp.cumsum(blocks)
    blk_start = blk_end - blocks
    total = blk_end[-1]
    dest = blk_start[e_flat] * MOE_BLOCK + rank
    row_tok = jnp.zeros((n_items * MOE_BLOCK,), jnp.int32).at[dest].set(
        jnp.arange(n_assign, dtype=jnp.int32) // TOP_K)
    p = jnp.arange(n_items, dtype=jnp.int32)
    pc = jnp.minimum(p, total - 1)
    item_e = jnp.sum((pc[:, None] >= blk_end[None, :]).astype(jnp.int32), axis=1)
    rows_left = counts[item_e] - (pc - blk_start[item_e]) * MOE_BLOCK
    nact = jnp.clip((rows_left + MOE_SUB - 1) // MOE_SUB, 0, MOE_BLOCK // MOE_SUB)
    nact = jnp.where(p < total, nact, 0).astype(jnp.int32)
    out_blk = jnp.where(p < total, p, n_items).astype(jnp.int32)
    return dest.astype(jnp.int32), row_tok, item_e.astype(jnp.int32), pc.astype(jnp.int32), out_blk, nact


def _gather_kernel(tok_ref, nact_ref, h_ref, o_ref, buf, sem):
    p = pl.program_id(0)
    nact = nact_ref[p]
    base = p * MOE_BLOCK

    def copy(r, t):
        return pltpu.make_async_copy(h_ref.at[pl.ds(t, 1)], buf.at[pl.ds(r, 1)], sem)

    def start(grp, c):
        for u in range(GATHER_UNROLL):
            r = grp * GATHER_UNROLL + u
            copy(r, tok_ref[base + r]).start()
        return c

    def wait(grp, c):
        for u in range(GATHER_UNROLL):
            copy(grp * GATHER_UNROLL + u, 0).wait()
        return c

    n_grp = nact * (MOE_SUB // GATHER_UNROLL)
    lax.fori_loop(0, n_grp, start, 0)
    lax.fori_loop(0, n_grp, wait, 0)
    for s in range(MOE_BLOCK // MOE_SUB):
        rows = pl.ds(s * MOE_SUB, MOE_SUB)

        @pl.when(s < nact)
        def _():
            o_ref[rows] = buf[rows].astype(BF16)

        @pl.when(s >= nact)
        def _():
            o_ref[rows] = jnp.zeros((MOE_SUB, o_ref.shape[1]), BF16)


def _moe_gather(h, row_tok, nact, n_items):
    d = h.shape[1]
    return pl.pallas_call(
        _gather_kernel,
        grid_spec=pltpu.PrefetchScalarGridSpec(
            num_scalar_prefetch=2,
            grid=(n_items,),
            in_specs=[pl.BlockSpec(memory_space=pl.ANY)],
            out_specs=pl.BlockSpec((MOE_BLOCK, d), lambda p, tok, na: (p, 0)),
            scratch_shapes=[pltpu.VMEM((MOE_BLOCK, d), F32), pltpu.SemaphoreType.DMA(())]),
        out_shape=jax.ShapeDtypeStruct((n_items * MOE_BLOCK, d), BF16),
        compiler_params=_cparams(1),
        name="moe_gather",
    )(row_tok, nact, h)


def _moe_glu_kernel(e_ref, blk_ref, oblk_ref, nact_ref, h_ref, wg_ref, wu_ref, o_ref):
    p = pl.program_id(0)
    nact = nact_ref[p]
    wg = wg_ref[0].astype(BF16)
    wu = wu_ref[0].astype(BF16)
    for s in range(MOE_BLOCK // MOE_SUB):
        rows = slice(s * MOE_SUB, (s + 1) * MOE_SUB)

        @pl.when(s < nact)
        def _():
            h = h_ref[rows]
            o_ref[rows] = (_silu(_dot(h, wg)) * _dot(h, wu)).astype(BF16)

        @pl.when(s >= nact)
        def _():
            o_ref[rows] = jnp.zeros((MOE_SUB, o_ref.shape[1]), BF16)


def _moe_glu(hs, wg, wu, plan, n_items, tf):
    item_e, in_blk, out_blk, nact = plan
    d, f = wg.shape[1], wg.shape[2]
    nj = f // tf

    def w_map(p, j, e, b, ob, na):
        return (e[p], 0, jnp.where(na[p] > 0, j, nj - 1))

    def o_map(p, j, e, b, ob, na):
        return (ob[p], jnp.where(na[p] > 0, j, 0))

    return pl.pallas_call(
        _moe_glu_kernel,
        grid_spec=pltpu.PrefetchScalarGridSpec(
            num_scalar_prefetch=4,
            grid=(n_items, nj),
            in_specs=[pl.BlockSpec((MOE_BLOCK, d), lambda p, j, e, b, ob, na: (b[p], 0)),
                      pl.BlockSpec((1, d, tf), w_map),
                      pl.BlockSpec((1, d, tf), w_map)],
            out_specs=pl.BlockSpec((MOE_BLOCK, tf), o_map)),
        out_shape=jax.ShapeDtypeStruct(((n_items + 1) * MOE_BLOCK, f), BF16),
        compiler_params=_cparams(2),
        name="moe_glu",
    )(item_e, in_blk, out_blk, nact, hs, wg, wu)


def _moe_down_kernel(e_ref, blk_ref, oblk_ref, nact_ref, a_ref, wd_ref, o_ref):
    nact = nact_ref[pl.program_id(0)]
    wd = wd_ref[0].astype(BF16)
    for s in range(MOE_BLOCK // MOE_SUB):
        rows = slice(s * MOE_SUB, (s + 1) * MOE_SUB)

        @pl.when(s < nact)
        def _():
            o_ref[rows] = _dot(a_ref[rows], wd)

        @pl.when(s >= nact)
        def _():
            o_ref[rows] = jnp.zeros((MOE_SUB, o_ref.shape[1]), F32)


def _moe_down(a, wd, plan, n_items, tn):
    item_e, in_blk, out_blk, nact = plan
    f, d = wd.shape[1], wd.shape[2]
    nj = d // tn

    def w_map(p, j, e, b, ob, na):
        return (e[p], 0, jnp.where(na[p] > 0, j, nj - 1))

    def o_map(p, j, e, b, ob, na):
        return (ob[p], jnp.where(na[p] > 0, j, 0))

    return pl.pallas_call(
        _moe_down_kernel,
        grid_spec=pltpu.PrefetchScalarGridSpec(
            num_scalar_prefetch=4,
            grid=(n_items, nj),
            in_specs=[pl.BlockSpec((MOE_BLOCK, f), lambda p, j, e, b, ob, na: (b[p], 0)),
                      pl.BlockSpec((1, f, tn), w_map)],
            out_specs=pl.BlockSpec((MOE_BLOCK, tn), o_map)),
        out_shape=jax.ShapeDtypeStruct(((n_items + 1) * MOE_BLOCK, d), F32),
        compiler_params=_cparams(2),
        name="moe_down",
    )(item_e, in_blk, out_blk, nact, a, wd)


def _combine_kernel(dest_ref, x_ref, mod_ref, w_ref, fg_ref, y_ref, o_ref, buf, sem, *, tm):
    i = pl.program_id(0)

    def copy(r, k, row):
        return pltpu.make_async_copy(y_ref.at[pl.ds(row, 1)], buf.at[k, pl.ds(r, 1)], sem)

    def start(grp, c):
        for u in range(GATHER_UNROLL):
            r = grp * GATHER_UNROLL + u
            for k in range(TOP_K):
                copy(r, k, dest_ref[(i * tm + r) * TOP_K + k]).start()
        return c

    def wait(grp, c):
        for u in range(GATHER_UNROLL):
            for k in range(TOP_K):
                copy(grp * GATHER_UNROLL + u, k, 0).wait()
        return c

    lax.fori_loop(0, tm // GATHER_UNROLL, start, 0)
    lax.fori_loop(0, tm // GATHER_UNROLL, wait, 0)
    w = w_ref[...]
    moe = w[:, 0:1] * buf[0] + w[:, 1:2] * buf[1]
    m = mod_ref[0]
    o_ref[...] = _rms(x_ref[...] + m[5:6] * moe, fg_ref[...])


def _moe_combine(dest, x, mod, top_w, final_g, ys, dims, tm):
    lat, d = x.shape
    seq_tiles = dims["seq"] // tm
    return pl.pallas_call(
        functools.partial(_combine_kernel, tm=tm),
        grid_spec=pltpu.PrefetchScalarGridSpec(
            num_scalar_prefetch=1,
            grid=(lat // tm,),
            in_specs=[pl.BlockSpec((tm, d), lambda i, dr: (i, 0)),
                      pl.BlockSpec((1, 6, d), lambda i, dr: (i // seq_tiles, 0, 0)),
                      pl.BlockSpec((tm, LANES), lambda i, dr: (i, 0)),
                      pl.BlockSpec((1, d), lambda i, dr: (0, 0)),
                      pl.BlockSpec(memory_space=pl.ANY)],
            out_specs=pl.BlockSpec((tm, d), lambda i, dr: (i, 0)),
            scratch_shapes=[pltpu.VMEM((TOP_K, tm, d), F32), pltpu.SemaphoreType.DMA(())]),
        out_shape=jax.ShapeDtypeStruct((lat, d), F32),
        compiler_params=_cparams(1),
        name="moe_combine",
    )(dest, x, mod, top_w, final_g.reshape(1, d), ys)


def kernel(x, c, ctx, c_ctx, ada_w, ada_b, norm1_g, norm2_g, ev_w_in, ev_w_out, ev_lambda, ev_subln_g, od_w_dq, od_q_norm_g, od_w_uq, od_w_dkv, od_kv_norm_g, od_w_ukv, od_w_o, ffn_w_gate, ffn_w_up, ffn_w_down, moe_router, moe_w_gate, moe_w_up, moe_w_down, final_norm_g):
    batch, seq, d = x.shape
    n_ctx = ctx.shape[1]
    depth = ada_w.shape[0]
    assert depth == 2 and batch < MOD_ROWS and seq % GRID_W == 0
    lat = batch * seq
    dims = dict(batch=batch, seq=seq, ctx=n_ctx, lat=lat)
    tm = 1024
    assert seq % tm == 0 and (batch * n_ctx) % tm == 0

    cond = jnp.concatenate([c, c_ctx[None, :], jnp.zeros((MOD_ROWS - batch - 1, d), F32)], axis=0)
    mod = _ada(cond, ada_w, ada_b).reshape(depth, MOD_ROWS, 6, d)
    xs = jnp.concatenate([x.reshape(lat, d), ctx.reshape(batch * n_ctx, d)], axis=0)

    lam_init = 0.8 - 0.6 * math.exp(-0.3 * 0)
    fw = ev_w_in.shape[2] // 4
    tabs_ev = jnp.asarray(np.stack([
        _rope_tables(seq, tm, DIFF_HEAD_DIM // 4, DIFF_HEAD_DIM ** -0.5),
        _rope_tables(seq, tm, DIFF_HEAD_DIM // 4, 1.0)]))
    qkvf = _inproj(xs, norm1_g[0], mod[0], ev_w_in[0], tabs_ev, dims, tm, 512)
    o_attn = _diff_attn(qkvf, ev_lambda[0], ev_subln_g[0], lam_init, dims, n_ctx)

    gw = fw // FOURIER_GROUPS
    cc_np, sc_np = _dft_cos_sin(gw)
    cc = jnp.asarray(cc_np.astype(np.float32)).astype(BF16)
    sc = jnp.asarray(sc_np.astype(np.float32)).astype(BF16)
    cn_np, sn_np = _dft_cos_sin(seq)
    cs_lat = jnp.asarray(np.concatenate([cn_np, -sn_np], axis=1).astype(np.float32)).astype(BF16)
    cx_np, sx_np = _dft_cos_sin(n_ctx)
    cs_ctx = jnp.asarray(np.concatenate([cx_np, -sx_np], axis=1).astype(np.float32)).astype(BF16)
    fm = _fourier(qkvf, None, seq, 0, batch, fw, cc, sc, cs_lat, "ev_fourier_lat")
    fm = _fourier(qkvf, fm, n_ctx, lat // n_ctx, batch, fw, cc, sc, cs_ctx, "ev_fourier_ctx")

    m_rows = xs.shape[0]
    xs = _mm_res([fm, o_attn], ev_w_out[0], xs, mod[0], 2, dims, m_rows, tm, 512, "ev_outproj")
    act = _glu(xs, norm2_g[0], mod[0], ffn_w_gate[0], ffn_w_up[0], dims, tm, 512)
    xs = _mm_res([act], ffn_w_down[0], xs, mod[0], 5, dims, m_rows, tm, 256, "ffn_down")

    heads = od_w_ukv.shape[2] // (MLA_NOPE + MLA_V)
    q_lora = od_w_dq.shape[2]
    wd_cat = jnp.concatenate(
        [od_w_dq[0], od_w_dkv[0], jnp.zeros((d, LANES - MLA_ROPE), F32)], axis=1).astype(BF16)
    wuq = jnp.pad(od_w_uq[0].reshape(q_lora, heads, MLA_NOPE + MLA_ROPE),
                  ((0, 0), (0, 0), (0, 2 * LANES - MLA_NOPE - MLA_ROPE))).reshape(q_lora, heads * 2 * LANES)
    tm_mla = 256
    tab_q = jnp.asarray(_rope_tables(seq, tm_mla, MLA_ROPE // 4, (MLA_NOPE + MLA_ROPE) ** -0.5))
    tab_k = jnp.asarray(_rope_tables(seq, tm_mla, MLA_ROPE // 4, 1.0))
    q, kv, kr = _mla_proj(xs, norm1_g[1], mod[1], wd_cat, od_q_norm_g[0], od_kv_norm_g[0],
                          wuq.astype(BF16), od_w_ukv[0].astype(BF16), tab_q, tab_k, dims, tm_mla)
    o_mla = _mla_attn(q, kv, kr, dims, 512)
    xl = _mm_res([o_mla], od_w_o[0], xs, mod[1], 2, dims, lat, tm, 512, "od_outproj")

    router_pad = jnp.pad(moe_router[0], ((0, 0), (0, LANES - N_EXPERTS)))
    h2, top_idx, top_w = _router(xl, norm2_g[1], mod[1], router_pad, dims, 512)
    n_items = lat * TOP_K // MOE_BLOCK + N_EXPERTS
    dest, row_tok, item_e, in_blk, out_blk, nact = _moe_plan(top_idx[:, :TOP_K], n_items)
    plan = (item_e, in_blk, out_blk, nact)
    hs = _moe_gather(h2, row_tok, nact, n_items)
    act = _moe_glu(hs, moe_w_gate[0], moe_w_up[0], plan, n_items, 512)
    ys = _moe_down(act, moe_w_down[0], plan, n_items, 256)
    out = _moe_combine(dest, xl, mod[1], top_w, final_norm_g, ys, dims, 256)
    return out.reshape(batch, seq, d)
```

```python
import functools
import math

import numpy as np
import jax
import jax.numpy as jnp
from jax import lax
from jax.experimental import pallas as pl
from jax.experimental.pallas import tpu as pltpu

F32 = jnp.float32
BF16 = jnp.bfloat16

GRID_W = 64
NORM_EPS = 1e-6
ROPE_BASE = 10000.0
FOURIER_GROUPS = 4
DIFF_HEAD_DIM = 128
MLA_NOPE = 128
MLA_ROPE = 64
MLA_V = 128
N_EXPERTS = 8
TOP_K = 2

LANES = 128
MOD_ROWS = 8
VMEM_LIMIT = 56 * 1024 * 1024
MOE_BLOCK = 1024
MOE_SUB = 256
GATHER_UNROLL = 8
KEY_CHUNK = 256
LOG2E = math.log2(math.e)


def _cparams(n_axes):
    return pltpu.CompilerParams(dimension_semantics=("arbitrary",) * n_axes,
                                vmem_limit_bytes=VMEM_LIMIT)


def _rms(x, g):
    return x * lax.rsqrt(jnp.mean(x * x, axis=-1, keepdims=True) + NORM_EPS) * g


def _norm_mod(x, g, shift, scale):
    return _rms(x, g) * (1.0 + scale) + shift


def _silu(x):
    return x * (1.0 / (1.0 + jnp.exp(-x)))


def _dot(a, b):
    return jnp.dot(a, b, preferred_element_type=F32)


def _dot_nt(a, b):
    return lax.dot_general(a, b, (((1,), (1,)), ((), ())), preferred_element_type=F32)


def _rope_tables(seq, extra_rows, chunk, scale):
    n = np.arange(seq)
    row, col = n // GRID_W, n % GRID_W
    lane = np.arange(LANES)
    a = 2 * chunk
    inv = ROPE_BASE ** (-np.arange(0, a, 2, dtype=np.float64) / a)
    used = lane < 4 * chunk
    freq = inv[lane % chunk]
    pos = np.where(lane[None, :] < 2 * chunk, row[:, None], col[:, None]).astype(np.float64)
    ang = pos * freq[None, :]
    first = (lane // chunk) % 2 == 0
    cos = np.where(used[None, :], np.cos(ang), 0.0)
    sin = np.where(used[None, :], np.sin(ang), 0.0)
    s1 = np.where(first[None, :], -sin, 0.0)
    s2 = np.where(first[None, :], 0.0, sin)
    ident = np.zeros((3, extra_rows, LANES))
    ident[0] = used[None, :].astype(np.float64)
    tab = np.concatenate([np.stack([cos, s1, s2]), ident], axis=1) * scale
    return tab.astype(np.float32)


def _apply_rope(x, tab_ref, chunk):
    return (x * tab_ref[0] + pltpu.roll(x, LANES - chunk, 1) * tab_ref[1]
            + pltpu.roll(x, chunk, 1) * tab_ref[2])


def _dft_cos_sin(n):
    k = np.arange(n)
    ang = 2.0 * np.pi * ((k[:, None] * k[None, :]) % n) / n
    return np.cos(ang), np.sin(ang)


def _ada_kernel(s_ref, w_ref, b_ref, o_ref):
    s = _silu(s_ref[...]).astype(BF16)
    o_ref[0] = _dot(s, w_ref[0].astype(BF16)) + b_ref[0]


def _ada(cond, ada_w, ada_b):
    depth, d, n = ada_w.shape
    tn = 1024
    return pl.pallas_call(
        _ada_kernel,
        grid=(depth, n // tn),
        in_specs=[pl.BlockSpec((MOD_ROWS, d), lambda i, j: (0, 0)),
                  pl.BlockSpec((1, d, tn), lambda i, j: (i, 0, j)),
                  pl.BlockSpec((1, 1, tn), lambda i, j: (i, 0, j))],
        out_specs=pl.BlockSpec((1, MOD_ROWS, tn), lambda i, j: (i, 0, j)),
        out_shape=jax.ShapeDtypeStruct((depth, MOD_ROWS, n), F32),
        compiler_params=_cparams(2),
        name="ada",
    )(cond, ada_w, ada_b.reshape(depth, 1, n))


def _inproj_kernel(x_ref, g_ref, mod_ref, w_ref, tab_ref, o_ref, h_ref):
    j = pl.program_id(1)
    quarter = pl.num_programs(1) // 4

    @pl.when(j == 0)
    def _():
        m = mod_ref[0]
        h_ref[...] = _norm_mod(x_ref[...], g_ref[...], m[0:1], m[1:2]).astype(BF16)

    res = _dot(h_ref[...], w_ref[...].astype(BF16))
    is_rope = jnp.logical_and(j >= quarter, j < 3 * quarter)

    @pl.when(is_rope)
    def _():
        for c in range(res.shape[1] // LANES):
            sl = slice(c * LANES, (c + 1) * LANES)
            o_ref[:, sl] = _apply_rope(res[:, sl], tab_ref.at[0], DIFF_HEAD_DIM // 4).astype(BF16)

    @pl.when(jnp.logical_not(is_rope))
    def _():
        o_ref[...] = res.astype(BF16)


def _inproj(x, g, mod, w, tabs, dims, tm, tn):
    m_rows, d = x.shape
    seq, lat = dims["seq"], dims["lat"]
    n = w.shape[1]
    nj = n // tn
    assert nj % 4 == 0
    lat_tiles, seq_tiles = lat // tm, seq // tm

    def sel(i):
        return jnp.minimum(i // seq_tiles, dims["batch"])

    def tab_map(i, j):
        return (jnp.where(j >= nj // 2, 1, 0), 0, jnp.where(i < lat_tiles, i % seq_tiles, seq_tiles), 0)

    return pl.pallas_call(
        _inproj_kernel,
        grid=(m_rows // tm, nj),
        in_specs=[pl.BlockSpec((tm, d), lambda i, j: (i, 0)),
                  pl.BlockSpec((1, d), lambda i, j: (0, 0)),
                  pl.BlockSpec((1, 6, d), lambda i, j: (sel(i), 0, 0)),
                  pl.BlockSpec((d, tn), lambda i, j: (0, j)),
                  pl.BlockSpec((1, 3, tm, LANES), tab_map)],
        out_specs=pl.BlockSpec((tm, tn), lambda i, j: (i, j)),
        out_shape=jax.ShapeDtypeStruct((m_rows, n), BF16),
        scratch_shapes=[pltpu.VMEM((tm, d), BF16)],
        compiler_params=_cparams(2),
        name="ev_inproj",
    )(x, g.reshape(1, d), mod, w, tabs)


def _softmax_numerators(qs, k_ref, kcols, s_scr, p_scr, want_sum):
    n_keys = k_ref.shape[0]
    chunks = [slice(c, c + KEY_CHUNK) for c in range(0, n_keys, KEY_CHUNK)]
    maxes = []
    for i, (q, cols) in enumerate(zip(qs, kcols)):
        m = None
        for ks in chunks:
            s = _dot_nt(q, k_ref[ks, cols])
            s_scr[i, :, ks] = s
            mc = jnp.max(s, axis=-1, keepdims=True)
            m = mc if m is None else jnp.maximum(m, mc)
        maxes.append(m)
    totals = []
    for i, m in enumerate(maxes):
        total = None
        for ks in chunks:
            e = jnp.exp2(s_scr[i, :, ks] - m)
            if want_sum:
                part = jnp.sum(e, axis=-1, keepdims=True)
                total = part if total is None else total + part
            p_scr[i, :, ks] = e.astype(BF16)
        totals.append(total)
    return totals


def _diff_attn_kernel(lam_ref, q_ref, *rest, lam_init, seg_rows):
    n_seg = len(seg_rows)
    k_refs, v_refs, g_ref = rest[:n_seg], rest[n_seg:2 * n_seg], rest[2 * n_seg]
    o_ref, k_scr, v_scr, s_scr, p_scr = rest[-5:]
    hd = DIFF_HEAD_DIM

    @pl.when(pl.program_id(2) == 0)
    def _():
        r0 = 0
        for k_ref, v_ref, n in zip(k_refs, v_refs, seg_rows):
            k_scr[r0:r0 + n] = k_ref[...]
            v_scr[r0:r0 + n] = v_ref[...]
            r0 += n

    lv = lam_ref[...]
    lam = (jnp.exp(jnp.sum(lv[0:1] * lv[1:2], axis=-1, keepdims=True))
           - jnp.exp(jnp.sum(lv[2:3] * lv[3:4], axis=-1, keepdims=True)) + lam_init)
    q = q_ref[...]
    cols = [slice(c * hd, (c + 1) * hd) for c in range(2)]
    totals = _softmax_numerators([q[:, c] for c in cols], k_scr, cols, s_scr, p_scr, True)
    outs = [_dot(p_scr[c], v_scr[...]) * (1.0 / totals[c]) for c in range(2)]
    o = outs[0] - lam * outs[1]
    o_ref[...] = (_rms(o, g_ref[...]) * (1.0 - lam_init)).astype(BF16)


def _diff_attn(qkvf, prev, lam_vec, subln_g, lam_init, dims, tq, latent):
    m_rows = qkvf.shape[0]
    batch, seq, ctx = dims["batch"], dims["seq"], dims["ctx"]
    hw = 2 * DIFF_HEAD_DIM
    width = qkvf.shape[1] // 4
    heads = width // hw
    ctx_blk0 = batch * seq // ctx
    qcol, kcol, vcol = width // hw, 2 * width // hw, 3 * width // hw
    if latent:
        nq, q_blk0, seg_rows = seq // tq, 0, (ctx, seq)
    else:
        assert tq == ctx
        nq, q_blk0, seg_rows = 1, ctx_blk0, (ctx,)
    n_keys = sum(seg_rows)

    def kv_specs(col):
        specs = [pl.BlockSpec((ctx, hw), lambda b, h, i: (ctx_blk0 + b, col + h))]
        if latent:
            specs.append(pl.BlockSpec((seq, hw), lambda b, h, i: (b, col + h)))
        return specs

    in_specs = ([pl.BlockSpec((4, DIFF_HEAD_DIM), lambda b, h, i: (0, 0)),
                 pl.BlockSpec((tq, hw), lambda b, h, i: (q_blk0 + b * nq + i, qcol + h))]
                + kv_specs(kcol) + kv_specs(vcol)
                + [pl.BlockSpec((1, hw), lambda b, h, i: (0, 0))])
    args = [lam_vec, qkvf] + [qkvf] * (2 * len(seg_rows)) + [subln_g.reshape(1, hw)]
    aliases = {}
    if prev is not None:
        in_specs.append(pl.BlockSpec(memory_space=pl.ANY))
        args.append(prev)
        aliases = {len(args) - 1: 0}
    kern = functools.partial(_diff_attn_kernel, lam_init=lam_init, seg_rows=seg_rows)
    return pl.pallas_call(
        kern,
        grid=(batch, heads, nq),
        in_specs=in_specs,
        out_specs=pl.BlockSpec((tq, hw), lambda b, h, i: (q_blk0 + b * nq + i, h)),
        out_shape=jax.ShapeDtypeStruct((m_rows, width), BF16),
        scratch_shapes=[pltpu.VMEM((n_keys, hw), BF16), pltpu.VMEM((n_keys, hw), BF16),
                        pltpu.VMEM((2, tq, n_keys), F32), pltpu.VMEM((2, tq, n_keys), BF16)],
        input_output_aliases=aliases,
        compiler_params=_cparams(3),
        name="ev_diff_attn_lat" if latent else "ev_diff_attn_ctx",
    )(*args)


def _fourier_kernel(u_ref, cc_ref, sc_ref, cs_ref, *rest, n, norm):
    o_ref, ab_ref = rest[-2], rest[-1]
    u = u_ref[...]
    ab_ref[0:n] = _dot(u, cc_ref[...]).astype(BF16)
    ab_ref[n:] = _dot(u, sc_ref[...]).astype(BF16)
    o_ref[...] = (_dot(cs_ref[...], ab_ref[...]) * norm).astype(BF16)


def _fourier(qkvf, prev, n, row_blk0, batch, width, cc, sc, cs, name):
    m_rows = qkvf.shape[0]
    gw = width // FOURIER_GROUPS
    kern = functools.partial(_fourier_kernel, n=n, norm=1.0 / math.sqrt(n * gw))
    in_specs = [pl.BlockSpec((n, gw), lambda b, g: (row_blk0 + b, g)),
                pl.BlockSpec((gw, gw), lambda b, g: (0, 0)),
                pl.BlockSpec((gw, gw), lambda b, g: (0, 0)),
                pl.BlockSpec((n, 2 * n), lambda b, g: (0, 0), pipeline_mode=pl.Buffered(1))]
    args = [qkvf, cc, sc, cs]
    aliases = {}
    if prev is not None:
        in_specs.append(pl.BlockSpec(memory_space=pl.ANY))
        args.append(prev)
        aliases = {4: 0}
    return pl.pallas_call(
        kern,
        grid=(batch, FOURIER_GROUPS),
        in_specs=in_specs,
        out_specs=pl.BlockSpec((n, gw), lambda b, g: (row_blk0 + b, g)),
        out_shape=jax.ShapeDtypeStruct((m_rows, width), BF16),
        scratch_shapes=[pltpu.VMEM((2 * n, gw), BF16)],
        input_output_aliases=aliases,
        compiler_params=_cparams(2),
        name=name,
    )(*args)


def _mm_res_kernel(*refs, n_a, gate_idx):
    a_refs, w_refs = refs[:n_a], refs[n_a:2 * n_a]
    x_ref, mod_ref, o_ref = refs[2 * n_a:]
    acc = _dot(a_refs[0][...], w_refs[0][...].astype(BF16))
    for a_ref, w_ref in zip(a_refs[1:], w_refs[1:]):
        acc = acc + _dot(a_ref[...], w_ref[...].astype(BF16))
    m = mod_ref[0]
    o_ref[...] = x_ref[...] + m[gate_idx:gate_idx + 1] * acc


def _mm_res(a_list, w, x, mod, gate_idx, dims, rows, tm, tn, name):
    d = w.shape[1]
    seq_tiles = dims["seq"] // tm
    n_a = len(a_list)

    def sel(i):
        return jnp.minimum(i // seq_tiles, dims["batch"])

    in_specs, w_args, k0 = [], [], 0
    for a in a_list:
        in_specs.append(pl.BlockSpec((tm, a.shape[1]), lambda i, j: (i, 0)))
    for a in a_list:
        ka = a.shape[1]
        assert k0 % ka == 0
        in_specs.append(pl.BlockSpec((ka, tn), lambda i, j, kb=k0 // ka: (kb, j)))
        w_args.append(w)
        k0 += ka
    assert k0 == w.shape[0]
    in_specs += [pl.BlockSpec((tm, tn), lambda i, j: (i, j)),
                 pl.BlockSpec((1, 6, tn), lambda i, j: (sel(i), 0, j))]
    return pl.pallas_call(
        functools.partial(_mm_res_kernel, n_a=n_a, gate_idx=gate_idx),
        grid=(rows // tm, d // tn),
        in_specs=in_specs,
        out_specs=pl.BlockSpec((tm, tn), lambda i, j: (i, j)),
        out_shape=jax.ShapeDtypeStruct((rows, d), F32),
        compiler_params=_cparams(2),
        name=name,
    )(*a_list, *w_args, x, mod)


def _glu_kernel(x_ref, g_ref, mod_ref, wg_ref, wu_ref, o_ref, h_ref):
    @pl.when(pl.program_id(1) == 0)
    def _():
        m = mod_ref[0]
        h_ref[...] = _norm_mod(x_ref[...], g_ref[...], m[3:4], m[4:5]).astype(BF16)

    h = h_ref[...]
    gate = _dot(h, wg_ref[...].astype(BF16))
    up = _dot(h, wu_ref[...].astype(BF16))
    o_ref[...] = (_silu(gate) * up).astype(BF16)


def _glu(x, g, mod, wg, wu, dims, tm, tf):
    m_rows, d = x.shape
    f = wg.shape[1]
    seq_tiles = dims["seq"] // tm

    def sel(i):
        return jnp.minimum(i // seq_tiles, dims["batch"])

    return pl.pallas_call(
        _glu_kernel,
        grid=(m_rows // tm, f // tf),
        in_specs=[pl.BlockSpec((tm, d), lambda i, j: (i, 0)),
                  pl.BlockSpec((1, d), lambda i, j: (0, 0)),
                  pl.BlockSpec((1, 6, d), lambda i, j: (sel(i), 0, 0)),
                  pl.BlockSpec((d, tf), lambda i, j: (0, j)),
                  pl.BlockSpec((d, tf), lambda i, j: (0, j))],
        out_specs=pl.BlockSpec((tm, tf), lambda i, j: (i, j)),
        out_shape=jax.ShapeDtypeStruct((m_rows, f), BF16),
        scratch_shapes=[pltpu.VMEM((tm, d), BF16)],
        compiler_params=_cparams(2),
        name="ffn_glu",
    )(x, g.reshape(1, d), mod, wg, wu)


def _mla_proj_kernel(x_ref, g_ref, mod_ref, wd_ref, qg_ref, kvg_ref, wuq_ref, wukv_ref,
                     tq_ref, tk_ref, q_ref, kv_ref, kr_ref, *, n_lat_tiles, q_lora, kv_lora, heads):
    i = pl.program_id(0)
    m = mod_ref[0]
    h = _norm_mod(x_ref[...], g_ref[...], m[0:1], m[1:2]).astype(BF16)
    t = _dot(h, wd_ref[...])
    ckv = _rms(t[:, q_lora:q_lora + kv_lora], kvg_ref[...]).astype(BF16)
    kv_ref[...] = _dot(ckv, wukv_ref[...]).astype(BF16)
    kr = t[:, q_lora + kv_lora:]
    kr_ref[...] = _apply_rope(kr, tk_ref, MLA_ROPE // 4).astype(BF16)

    @pl.when(i < n_lat_tiles)
    def _():
        cq = _rms(t[:, :q_lora], qg_ref[...]).astype(BF16)
        q = _dot(cq, wuq_ref[...])
        scale = (MLA_NOPE + MLA_ROPE) ** -0.5 * LOG2E
        for hh in range(heads):
            c0 = hh * 2 * LANES
            q_ref[:, c0:c0 + LANES] = (q[:, c0:c0 + LANES] * scale).astype(BF16)
            q_ref[:, c0 + LANES:c0 + 2 * LANES] = _apply_rope(
                q[:, c0 + LANES:c0 + 2 * LANES], tq_ref, MLA_ROPE // 4).astype(BF16)


def _mla_proj(x, g, mod, wd, qg, kvg, wuq, wukv, tab_q, tab_k, dims, tm):
    m_rows, d = x.shape
    seq, lat = dims["seq"], dims["lat"]
    q_lora, kv_lora = qg.shape[0], kvg.shape[0]
    heads = wukv.shape[1] // (MLA_NOPE + MLA_V)
    lat_tiles, seq_tiles = lat // tm, seq // tm

    def sel(i):
        return jnp.minimum(i // seq_tiles, dims["batch"])

    def tab_map(i):
        return (0, jnp.where(i < lat_tiles, i % seq_tiles, seq_tiles), 0)

    kern = functools.partial(_mla_proj_kernel, n_lat_tiles=lat_tiles, q_lora=q_lora, kv_lora=kv_lora,
                             heads=heads)
    const = lambda i: (0, 0)
    return pl.pallas_call(
        kern,
        grid=(m_rows // tm,),
        in_specs=[pl.BlockSpec((tm, d), lambda i: (i, 0)),
                  pl.BlockSpec((1, d), const),
                  pl.BlockSpec((1, 6, d), lambda i: (sel(i), 0, 0)),
                  pl.BlockSpec(wd.shape, const),
                  pl.BlockSpec((1, q_lora), const),
                  pl.BlockSpec((1, kv_lora), const),
                  pl.BlockSpec(wuq.shape, const),
                  pl.BlockSpec(wukv.shape, const),
                  pl.BlockSpec((3, tm, LANES), tab_map),
                  pl.BlockSpec((3, tm, LANES), tab_map)],
        out_specs=[pl.BlockSpec((tm, wuq.shape[1]), lambda i: (jnp.minimum(i, lat_tiles - 1), 0)),
                   pl.BlockSpec((tm, wukv.shape[1]), lambda i: (i, 0)),
                   pl.BlockSpec((tm, LANES), lambda i: (i, 0))],
        out_shape=[jax.ShapeDtypeStruct((lat, wuq.shape[1]), BF16),
                   jax.ShapeDtypeStruct((m_rows, wukv.shape[1]), BF16),
                   jax.ShapeDtypeStruct((m_rows, LANES), BF16)],
        compiler_params=_cparams(1),
        name="od_mla_proj",
    )(x, g.reshape(1, d), mod, wd, qg.reshape(1, q_lora), kvg.reshape(1, kv_lora), wuq, wukv, tab_q, tab_k)


def _mla_attn_kernel(q_ref, knc_ref, knl_ref, krc_ref, krl_ref, vc_ref, vl_ref, o_ref,
                     k_scr, v_scr, s_scr, p_scr, *, n_ctx):
    @pl.when(pl.program_id(2) == 0)
    def _():
        k_scr[0:n_ctx, 0:LANES] = knc_ref[...]
        k_scr[n_ctx:, 0:LANES] = knl_ref[...]
        k_scr[0:n_ctx, LANES:] = krc_ref[...]
        k_scr[n_ctx:, LANES:] = krl_ref[...]
        v_scr[0:n_ctx, 0:LANES] = vc_ref[...]
        v_scr[n_ctx:, 0:LANES] = vl_ref[...]
        v_scr[:, LANES:] = jnp.ones((v_scr.shape[0], LANES), BF16)

    half = q_ref.shape[0] // 2
    halves = [slice(0, half), slice(half, 2 * half)]
    _softmax_numerators([q_ref[r] for r in halves], k_scr, [slice(None)] * 2, s_scr, p_scr, False)
    for i, r in enumerate(halves):
        acc = _dot(p_scr[i], v_scr[...])
        o_ref[r] = (acc[:, :LANES] / acc[:, LANES:LANES + 1]).astype(BF16)


def _mla_attn(q, kv, kr, dims, tq):
    batch, seq, ctx, lat = dims["batch"], dims["seq"], dims["ctx"], dims["lat"]
    heads = q.shape[1] // (2 * LANES)
    nq = seq // tq
    ctx_blk0 = lat // ctx
    return pl.pallas_call(
        functools.partial(_mla_attn_kernel, n_ctx=ctx),
        grid=(batch, heads, nq),
        in_specs=[pl.BlockSpec((tq, 2 * LANES), lambda b, h, i: (b * nq + i, h)),
                  pl.BlockSpec((ctx, LANES), lambda b, h, i: (ctx_blk0 + b, 2 * h)),
                  pl.BlockSpec((seq, LANES), lambda b, h, i: (b, 2 * h)),
                  pl.BlockSpec((ctx, LANES), lambda b, h, i: (ctx_blk0 + b, 0)),
                  pl.BlockSpec((seq, LANES), lambda b, h, i: (b, 0)),
                  pl.BlockSpec((ctx, LANES), lambda b, h, i: (ctx_blk0 + b, 2 * h + 1)),
                  pl.BlockSpec((seq, LANES), lambda b, h, i: (b, 2 * h + 1))],
        out_specs=pl.BlockSpec((tq, LANES), lambda b, h, i: (b * nq + i, h)),
        out_shape=jax.ShapeDtypeStruct((lat, heads * LANES), BF16),
        scratch_shapes=[pltpu.VMEM((ctx + seq, 2 * LANES), BF16), pltpu.VMEM((ctx + seq, 2 * LANES), BF16),
                        pltpu.VMEM((2, tq // 2, ctx + seq), F32), pltpu.VMEM((2, tq // 2, ctx + seq), BF16)],
        compiler_params=_cparams(3),
        name="od_mla_attn",
    )(q, kv, kv, kr, kr, kv, kv)


def _router_kernel(x_ref, g_ref, mod_ref, r_ref, h_ref, idx_ref, w_ref):
    m = mod_ref[0]
    h = _norm_mod(x_ref[...], g_ref[...], m[3:4], m[4:5])
    h_ref[...] = h
    r = r_ref[...]
    h_hi = h.astype(BF16)
    h_lo = (h - h_hi.astype(F32)).astype(BF16)
    r_hi = r.astype(BF16)
    r_lo = (r - r_hi.astype(F32)).astype(BF16)
    logits = _dot(h_hi, r_hi) + (_dot(h_lo, r_hi) + _dot(h_hi, r_lo))
    lane = lax.broadcasted_iota(jnp.int32, logits.shape, 1)
    lane_f = lane.astype(F32)
    neg = jnp.float32(-jnp.inf)
    logits = jnp.where(lane < N_EXPERTS, logits, neg)
    m1 = jnp.max(logits, axis=-1, keepdims=True)
    i1 = jnp.min(jnp.where(logits == m1, lane_f, float(LANES)), axis=-1, keepdims=True)
    rest = jnp.where(lane_f == i1, neg, logits)
    m2 = jnp.max(rest, axis=-1, keepdims=True)
    i2 = jnp.min(jnp.where(rest == m2, lane_f, float(LANES)), axis=-1, keepdims=True)
    e2 = jnp.exp(m2 - m1)
    w1 = 1.0 / (1.0 + e2)
    w2 = e2 / (1.0 + e2)
    idx_ref[...] = jnp.where(lane == 0, i1, jnp.where(lane == 1, i2, 0.0)).astype(jnp.int32)
    w_ref[...] = jnp.where(lane == 0, w1, jnp.where(lane == 1, w2, 0.0))


def _router(x, g, mod, router_pad, dims, tm):
    lat, d = x.shape
    seq_tiles = dims["seq"] // tm
    return pl.pallas_call(
        _router_kernel,
        grid=(lat // tm,),
        in_specs=[pl.BlockSpec((tm, d), lambda i: (i, 0)),
                  pl.BlockSpec((1, d), lambda i: (0, 0)),
                  pl.BlockSpec((1, 6, d), lambda i: (i // seq_tiles, 0, 0)),
                  pl.BlockSpec((d, LANES), lambda i: (0, 0))],
        out_specs=[pl.BlockSpec((tm, d), lambda i: (i, 0)),
                   pl.BlockSpec((tm, LANES), lambda i: (i, 0)),
                   pl.BlockSpec((tm, LANES), lambda i: (i, 0))],
        out_shape=[jax.ShapeDtypeStruct((lat, d), F32),
                   jax.ShapeDtypeStruct((lat, LANES), jnp.int32),
                   jax.ShapeDtypeStruct((lat, LANES), F32)],
        compiler_params=_cparams(1),
        name="moe_router",
    )(x, g.reshape(1, d), mod, router_pad)


def _moe_plan(top_idx, n_items):
    e_flat = top_idx.reshape(-1)
    n_assign = e_flat.shape[0]
    onehot = (e_flat[:, None] == jnp.arange(N_EXPERTS, dtype=jnp.int32)[None, :]).astype(jnp.int32)
    csum = jnp.cumsum(onehot, axis=0)
    counts = csum[-1]
    rank = jnp.sum(csum * onehot, axis=1) - 1
    blocks = (counts + MOE_BLOCK - 1) // MOE_BLOCK
    blk_end = jnp.cumsum(blocks)
    blk_start = blk_end - blocks
    total = blk_end[-1]
    dest = blk_start[e_flat] * MOE_BLOCK + rank
    row_tok = jnp.zeros((n_items * MOE_BLOCK,), jnp.int32).at[dest].set(
        jnp.arange(n_assign, dtype=jnp.int32) // TOP_K)
    p = jnp.arange(n_items, dtype=jnp.int32)
    pc = jnp.minimum(p, total - 1)
    item_e = jnp.sum((pc[:, None] >= blk_end[None, :]).astype(jnp.int32), axis=1)
    rows_left = counts[item_e] - (pc - blk_start[item_e]) * MOE_BLOCK
    nact = jnp.clip((rows_left + MOE_SUB - 1) // MOE_SUB, 0, MOE_BLOCK // MOE_SUB)
    nact = jnp.where(p < total, nact, 0).astype(jnp.int32)
    out_blk = jnp.where(p < total, p, n_items).astype(jnp.int32)
    return dest.astype(jnp.int32), row_tok, item_e.astype(jnp.int32), pc.astype(jnp.int32), out_blk, nact


def _gather_kernel(tok_ref, nact_ref, h_ref, o_ref, buf, sem):
    p = pl.program_id(0)
    nact = nact_ref[p]
    base = p * MOE_BLOCK

    def copy(r, t):
        return pltpu.make_async_copy(h_ref.at[pl.ds(t, 1)], buf.at[pl.ds(r, 1)], sem)

    def start(grp, c):
        for u in range(GATHER_UNROLL):
            r = grp * GATHER_UNROLL + u
            copy(r, tok_ref[base + r]).start()
        return c

    def wait(grp, c):
        for u in range(GATHER_UNROLL):
            copy(grp * GATHER_UNROLL + u, 0).wait()
        return c

    n_grp = nact * (MOE_SUB // GATHER_UNROLL)
    lax.fori_loop(0, n_grp, start, 0)
    lax.fori_loop(0, n_grp, wait, 0)
    for s in range(MOE_BLOCK // MOE_SUB):
        rows = pl.ds(s * MOE_SUB, MOE_SUB)

        @pl.when(s < nact)
        def _():
            o_ref[rows] = buf[rows].astype(BF16)

        @pl.when(s >= nact)
        def _():
            o_ref[rows] = jnp.zeros((MOE_SUB, o_ref.shape[1]), BF16)


def _moe_gather(h, row_tok, nact, n_items):
    d = h.shape[1]
    return pl.pallas_call(
        _gather_kernel,
        grid_spec=pltpu.PrefetchScalarGridSpec(
            num_scalar_prefetch=2,
            grid=(n_items,),
            in_specs=[pl.BlockSpec(memory_space=pl.ANY)],
            out_specs=pl.BlockSpec((MOE_BLOCK, d), lambda p, tok, na: (p, 0)),
            scratch_shapes=[pltpu.VMEM((MOE_BLOCK, d), F32), pltpu.SemaphoreType.DMA(())]),
        out_shape=jax.ShapeDtypeStruct((n_items * MOE_BLOCK, d), BF16),
        compiler_params=_cparams(1),
        name="moe_gather",
    )(row_tok, nact, h)


def _for_active_rows(nact, in_ref, o_ref, fn):
    n_sub = MOE_BLOCK // MOE_SUB
    for k in range(n_sub + 1):
        @pl.when(nact == k)
        def _():
            if k > 0:
                o_ref[0:k * MOE_SUB] = fn(in_ref[0:k * MOE_SUB])
            if k < n_sub:
                o_ref[k * MOE_SUB:] = jnp.zeros((MOE_BLOCK - k * MOE_SUB, o_ref.shape[1]), o_ref.dtype)


def _moe_glu_kernel(e_ref, blk_ref, oblk_ref, nact_ref, h_ref, wg_ref, wu_ref, o_ref):
    nact = nact_ref[pl.program_id(0)]
    wg = wg_ref[0].astype(BF16)
    wu = wu_ref[0].astype(BF16)

    def run(h):
        return (_silu(_dot(h, wg)) * _dot(h, wu)).astype(BF16)

    _for_active_rows(nact, h_ref, o_ref, run)


def _moe_glu(hs, wg, wu, plan, n_items, tf):
    item_e, in_blk, out_blk, nact = plan
    d, f = wg.shape[1], wg.shape[2]
    nj = f // tf

    def w_map(p, j, e, b, ob, na):
        return (e[p], 0, jnp.where(na[p] > 0, j, nj - 1))

    def o_map(p, j, e, b, ob, na):
        return (ob[p], jnp.where(na[p] > 0, j, 0))

    return pl.pallas_call(
        _moe_glu_kernel,
        grid_spec=pltpu.PrefetchScalarGridSpec(
            num_scalar_prefetch=4,
            grid=(n_items, nj),
            in_specs=[pl.BlockSpec((MOE_BLOCK, d), lambda p, j, e, b, ob, na: (b[p], 0)),
                      pl.BlockSpec((1, d, tf), w_map),
                      pl.BlockSpec((1, d, tf), w_map)],
            out_specs=pl.BlockSpec((MOE_BLOCK, tf), o_map)),
        out_shape=jax.ShapeDtypeStruct(((n_items + 1) * MOE_BLOCK, f), BF16),
        compiler_params=_cparams(2),
        name="moe_glu",
    )(item_e, in_blk, out_blk, nact, hs, wg, wu)


def _moe_down_kernel(e_ref, blk_ref, oblk_ref, nact_ref, a_ref, wd_ref, o_ref):
    nact = nact_ref[pl.program_id(0)]
    wd = wd_ref[0].astype(BF16)
    _for_active_rows(nact, a_ref, o_ref, lambda a: _dot(a, wd))


def _moe_down(a, wd, plan, n_items, tn):
    item_e, in_blk, out_blk, nact = plan
    f, d = wd.shape[1], wd.shape[2]
    nj = d // tn

    def w_map(p, j, e, b, ob, na):
        return (e[p], 0, jnp.where(na[p] > 0, j, nj - 1))

    def o_map(p, j, e, b, ob, na):
        return (ob[p], jnp.where(na[p] > 0, j, 0))

    return pl.pallas_call(
        _moe_down_kernel,
        grid_spec=pltpu.PrefetchScalarGridSpec(
            num_scalar_prefetch=4,
            grid=(n_items, nj),
            in_specs=[pl.BlockSpec((MOE_BLOCK, f), lambda p, j, e, b, ob, na: (b[p], 0)),
                      pl.BlockSpec((1, f, tn), w_map)],
            out_specs=pl.BlockSpec((MOE_BLOCK, tn), o_map)),
        out_shape=jax.ShapeDtypeStruct(((n_items + 1) * MOE_BLOCK, d), F32),
        compiler_params=_cparams(2),
        name="moe_down",
    )(item_e, in_blk, out_blk, nact, a, wd)


def _combine_kernel(dest_ref, x_ref, mod_ref, w_ref, fg_ref, y_ref, o_ref, buf, sem, *, tm):
    i = pl.program_id(0)

    def copy(r, k, row):
        return pltpu.make_async_copy(y_ref.at[pl.ds(row, 1)], buf.at[k, pl.ds(r, 1)], sem)

    def start(grp, c):
        for u in range(GATHER_UNROLL):
            r = grp * GATHER_UNROLL + u
            for k in range(TOP_K):
                copy(r, k, dest_ref[(i * tm + r) * TOP_K + k]).start()
        return c

    def wait(grp, c):
        for u in range(GATHER_UNROLL):
            for k in range(TOP_K):
                copy(grp * GATHER_UNROLL + u, k, 0).wait()
        return c

    lax.fori_loop(0, tm // GATHER_UNROLL, start, 0)
    lax.fori_loop(0, tm // GATHER_UNROLL, wait, 0)
    w = w_ref[...]
    moe = w[:, 0:1] * buf[0] + w[:, 1:2] * buf[1]
    m = mod_ref[0]
    o_ref[...] = _rms(x_ref[...] + m[5:6] * moe, fg_ref[...])


def _moe_combine(dest, x, mod, top_w, final_g, ys, dims, tm):
    lat, d = x.shape
    seq_tiles = dims["seq"] // tm
    return pl.pallas_call(
        functools.partial(_combine_kernel, tm=tm),
        grid_spec=pltpu.PrefetchScalarGridSpec(
            num_scalar_prefetch=1,
            grid=(lat // tm,),
            in_specs=[pl.BlockSpec((tm, d), lambda i, dr: (i, 0)),
                      pl.BlockSpec((1, 6, d), lambda i, dr: (i // seq_tiles, 0, 0)),
                      pl.BlockSpec((tm, LANES), lambda i, dr: (i, 0)),
                      pl.BlockSpec((1, d), lambda i, dr: (0, 0)),
                      pl.BlockSpec(memory_space=pl.ANY)],
            out_specs=pl.BlockSpec((tm, d), lambda i, dr: (i, 0)),
            scratch_shapes=[pltpu.VMEM((TOP_K, tm, d), F32), pltpu.SemaphoreType.DMA(())]),
        out_shape=jax.ShapeDtypeStruct((lat, d), F32),
        compiler_params=_cparams(1),
        name="moe_combine",
    )(dest, x, mod, top_w, final_g.reshape(1, d), ys)


def kernel(x, c, ctx, c_ctx, ada_w, ada_b, norm1_g, norm2_g, ev_w_in, ev_w_out, ev_lambda, ev_subln_g, od_w_dq, od_q_norm_g, od_w_uq, od_w_dkv, od_kv_norm_g, od_w_ukv, od_w_o, ffn_w_gate, ffn_w_up, ffn_w_down, moe_router, moe_w_gate, moe_w_up, moe_w_down, final_norm_g):
    batch, seq, d = x.shape
    n_ctx = ctx.shape[1]
    depth = ada_w.shape[0]
    assert depth == 2 and batch < MOD_ROWS and seq % GRID_W == 0
    lat = batch * seq
    dims = dict(batch=batch, seq=seq, ctx=n_ctx, lat=lat)
    tm = 1024
    assert seq % tm == 0 and (batch * n_ctx) % tm == 0

    cond = jnp.concatenate([c, c_ctx[None, :], jnp.zeros((MOD_ROWS - batch - 1, d), F32)], axis=0)
    mod = _ada(cond, ada_w, ada_b).reshape(depth, MOD_ROWS, 6, d)
    xs = jnp.concatenate([x.reshape(lat, d), ctx.reshape(batch * n_ctx, d)], axis=0)

    lam_init = 0.8 - 0.6 * math.exp(-0.3 * 0)
    fw = ev_w_in.shape[2] // 4
    tabs_ev = jnp.asarray(np.stack([
        _rope_tables(seq, tm, DIFF_HEAD_DIM // 4, DIFF_HEAD_DIM ** -0.5 * LOG2E),
        _rope_tables(seq, tm, DIFF_HEAD_DIM // 4, 1.0)]))
    qkvf = _inproj(xs, norm1_g[0], mod[0], ev_w_in[0], tabs_ev, dims, tm, 512)
    o_attn = _diff_attn(qkvf, None, ev_lambda[0], ev_subln_g[0], lam_init, dims, 512, True)
    o_attn = _diff_attn(qkvf, o_attn, ev_lambda[0], ev_subln_g[0], lam_init, dims, n_ctx, False)

    gw = fw // FOURIER_GROUPS
    cc_np, sc_np = _dft_cos_sin(gw)
    cc = jnp.asarray(cc_np.astype(np.float32)).astype(BF16)
    sc = jnp.asarray(sc_np.astype(np.float32)).astype(BF16)
    cn_np, sn_np = _dft_cos_sin(seq)
    cs_lat = jnp.asarray(np.concatenate([cn_np, -sn_np], axis=1).astype(np.float32)).astype(BF16)
    cx_np, sx_np = _dft_cos_sin(n_ctx)
    cs_ctx = jnp.asarray(np.concatenate([cx_np, -sx_np], axis=1).astype(np.float32)).astype(BF16)
    fm = _fourier(qkvf, None, seq, 0, batch, fw, cc, sc, cs_lat, "ev_fourier_lat")
    fm = _fourier(qkvf, fm, n_ctx, lat // n_ctx, batch, fw, cc, sc, cs_ctx, "ev_fourier_ctx")

    m_rows = xs.shape[0]
    xs = _mm_res([fm, o_attn], ev_w_out[0], xs, mod[0], 2, dims, m_rows, tm, 512, "ev_outproj")
    act = _glu(xs, norm2_g[0], mod[0], ffn_w_gate[0], ffn_w_up[0], dims, tm, 512)
    xs = _mm_res([act], ffn_w_down[0], xs, mod[0], 5, dims, m_rows, tm, 256, "ffn_down")

    heads = od_w_ukv.shape[2] // (MLA_NOPE + MLA_V)
    q_lora = od_w_dq.shape[2]
    wd_cat = jnp.concatenate(
        [od_w_dq[0], od_w_dkv[0], jnp.zeros((d, LANES - MLA_ROPE), F32)], axis=1).astype(BF16)
    wuq = jnp.pad(od_w_uq[0].reshape(q_lora, heads, MLA_NOPE + MLA_ROPE),
                  ((0, 0), (0, 0), (0, 2 * LANES - MLA_NOPE - MLA_ROPE))).reshape(q_lora, heads * 2 * LANES)
    tm_mla = 256
    tab_q = jnp.asarray(_rope_tables(seq, tm_mla, MLA_ROPE // 4, (MLA_NOPE + MLA_ROPE) ** -0.5 * LOG2E))
    tab_k = jnp.asarray(_rope_tables(seq, tm_mla, MLA_ROPE // 4, 1.0))
    q, kv, kr = _mla_proj(xs, norm1_g[1], mod[1], wd_cat, od_q_norm_g[0], od_kv_norm_g[0],
                          wuq.astype(BF16), od_w_ukv[0].astype(BF16), tab_q, tab_k, dims, tm_mla)
    o_mla = _mla_attn(q, kv, kr, dims, 1024)
    xl = _mm_res([o_mla], od_w_o[0], xs, mod[1], 2, dims, lat, tm, 512, "od_outproj")

    router_pad = jnp.pad(moe_router[0], ((0, 0), (0, LANES - N_EXPERTS)))
    h2, top_idx, top_w = _router(xl, norm2_g[1], mod[1], router_pad, dims, 512)
    n_items = lat * TOP_K // MOE_BLOCK + N_EXPERTS
    dest, row_tok, item_e, in_blk, out_blk, nact = _moe_plan(top_idx[:, :TOP_K], n_items)
    plan = (item_e, in_blk, out_blk, nact)
    hs = _moe_gather(h2, row_tok, nact, n_items)
    act = _moe_glu(hs, moe_w_gate[0], moe_w_up[0], plan, n_items, 512)
    ys = _moe_down(act, moe_w_down[0], plan, n_items, 256)
    out = _moe_combine(dest, xl, mod[1], top_w, final_norm_g, ys, dims, 256)
    return out.reshape(batch, seq, d)
```

```python
import functools
import math

import numpy as np
import jax
import jax.numpy as jnp
from jax import lax
from jax.experimental import pallas as pl
from jax.experimental.pallas import tpu as pltpu

F32 = jnp.float32
BF16 = jnp.bfloat16

GRID_W = 64
NORM_EPS = 1e-6
ROPE_BASE = 10000.0
FOURIER_GROUPS = 4
DIFF_HEAD_DIM = 128
MLA_NOPE = 128
MLA_ROPE = 64
MLA_V = 128
N_EXPERTS = 8
TOP_K = 2

LANES = 128
MOD_ROWS = 8
VMEM_LIMIT = 56 * 1024 * 1024
MOE_BLOCK = 1024
MOE_SUB = 256
GATHER_UNROLL = 8
KEY_CHUNK = 256
LOG2E = math.log2(math.e)


def _cparams(n_axes):
    return pltpu.CompilerParams(dimension_semantics=("arbitrary",) * n_axes,
                                vmem_limit_bytes=VMEM_LIMIT)


def _rms(x, g):
    return x * lax.rsqrt(jnp.mean(x * x, axis=-1, keepdims=True) + NORM_EPS) * g


def _norm_mod(x, g, shift, scale):
    return _rms(x, g) * (1.0 + scale) + shift


def _silu(x):
    return x * (1.0 / (1.0 + jnp.exp(-x)))


def _dot(a, b):
    return jnp.dot(a, b, preferred_element_type=F32)


def _dot_nt(a, b):
    return lax.dot_general(a, b, (((1,), (1,)), ((), ())), preferred_element_type=F32)


def _rope_tables(seq, extra_rows, chunk, scale):
    n = np.arange(seq)
    row, col = n // GRID_W, n % GRID_W
    lane = np.arange(LANES)
    a = 2 * chunk
    inv = ROPE_BASE ** (-np.arange(0, a, 2, dtype=np.float64) / a)
    used = lane < 4 * chunk
    freq = inv[lane % chunk]
    pos = np.where(lane[None, :] < 2 * chunk, row[:, None], col[:, None]).astype(np.float64)
    ang = pos * freq[None, :]
    first = (lane // chunk) % 2 == 0
    cos = np.where(used[None, :], np.cos(ang), 0.0)
    sin = np.where(used[None, :], np.sin(ang), 0.0)
    s1 = np.where(first[None, :], -sin, 0.0)
    s2 = np.where(first[None, :], 0.0, sin)
    ident = np.zeros((3, extra_rows, LANES))
    ident[0] = used[None, :].astype(np.float64)
    tab = np.concatenate([np.stack([cos, s1, s2]), ident], axis=1) * scale
    return tab.astype(np.float32)


def _apply_rope(x, tab_ref, chunk):
    return (x * tab_ref[0] + pltpu.roll(x, LANES - chunk, 1) * tab_ref[1]
            + pltpu.roll(x, chunk, 1) * tab_ref[2])


def _dft_cos_sin(n):
    k = np.arange(n)
    ang = 2.0 * np.pi * ((k[:, None] * k[None, :]) % n) / n
    return np.cos(ang), np.sin(ang)


def _ada_kernel(s_ref, w_ref, b_ref, o_ref):
    s = _silu(s_ref[...]).astype(BF16)
    o_ref[0] = _dot(s, w_ref[0].astype(BF16)) + b_ref[0]


def _ada(cond, ada_w, ada_b):
    depth, d, n = ada_w.shape
    tn = 1024
    return pl.pallas_call(
        _ada_kernel,
        grid=(depth, n // tn),
        in_specs=[pl.BlockSpec((MOD_ROWS, d), lambda i, j: (0, 0)),
                  pl.BlockSpec((1, d, tn), lambda i, j: (i, 0, j)),
                  pl.BlockSpec((1, 1, tn), lambda i, j: (i, 0, j))],
        out_specs=pl.BlockSpec((1, MOD_ROWS, tn), lambda i, j: (i, 0, j)),
        out_shape=jax.ShapeDtypeStruct((depth, MOD_ROWS, n), F32),
        compiler_params=_cparams(2),
        name="ada",
    )(cond, ada_w, ada_b.reshape(depth, 1, n))


def _inproj_kernel(x_ref, g_ref, mod_ref, w_ref, tab_ref, o_ref, h_ref):
    j = pl.program_id(1)
    quarter = pl.num_programs(1) // 4

    @pl.when(j == 0)
    def _():
        m = mod_ref[0]
        h_ref[...] = _norm_mod(x_ref[...], g_ref[...], m[0:1], m[1:2]).astype(BF16)

    is_rope = jnp.logical_and(j >= quarter, j < 3 * quarter)

    @pl.when(is_rope)
    def _():
        res = _dot(h_ref[...], w_ref[...].astype(BF16))
        for c in range(res.shape[1] // LANES):
            sl = slice(c * LANES, (c + 1) * LANES)
            o_ref[:, sl] = _apply_rope(res[:, sl], tab_ref.at[0], DIFF_HEAD_DIM // 4).astype(BF16)

    @pl.when(jnp.logical_not(is_rope))
    def _():
        o_ref[...] = _dot(h_ref[...], w_ref[...].astype(BF16)).astype(BF16)


def _inproj(x, g, mod, w, tabs, dims, tm, tn):
    m_rows, d = x.shape
    seq, lat = dims["seq"], dims["lat"]
    n = w.shape[1]
    nj = n // tn
    assert nj % 4 == 0
    lat_tiles, seq_tiles = lat // tm, seq // tm

    def sel(i):
        return jnp.minimum(i // seq_tiles, dims["batch"])

    def tab_map(i, j):
        return (jnp.where(j >= nj // 2, 1, 0), 0, jnp.where(i < lat_tiles, i % seq_tiles, seq_tiles), 0)

    return pl.pallas_call(
        _inproj_kernel,
        grid=(m_rows // tm, nj),
        in_specs=[pl.BlockSpec((tm, d), lambda i, j: (i, 0)),
                  pl.BlockSpec((1, d), lambda i, j: (0, 0)),
                  pl.BlockSpec((1, 6, d), lambda i, j: (sel(i), 0, 0)),
                  pl.BlockSpec((d, tn), lambda i, j: (0, j)),
                  pl.BlockSpec((1, 3, tm, LANES), tab_map)],
        out_specs=pl.BlockSpec((tm, tn), lambda i, j: (i, j)),
        out_shape=jax.ShapeDtypeStruct((m_rows, n), BF16),
        scratch_shapes=[pltpu.VMEM((tm, d), BF16)],
        compiler_params=_cparams(2),
        name="ev_inproj",
    )(x, g.reshape(1, d), mod, w, tabs)


def _softmax_numerators(qs, k_ref, kcols, s_scr, p_scr, want_sum):
    n_keys = k_ref.shape[0]
    chunks = [slice(c, c + KEY_CHUNK) for c in range(0, n_keys, KEY_CHUNK)]
    maxes = []
    for i, (q, cols) in enumerate(zip(qs, kcols)):
        m = None
        for ks in chunks:
            s = _dot_nt(q, k_ref[ks, cols])
            s_scr[i, :, ks] = s
            mc = jnp.max(s, axis=-1, keepdims=True)
            m = mc if m is None else jnp.maximum(m, mc)
        maxes.append(m)
    totals = []
    for i, m in enumerate(maxes):
        total = None
        for ks in chunks:
            e = jnp.exp2(s_scr[i, :, ks] - m)
            if want_sum:
                part = jnp.sum(e, axis=-1, keepdims=True)
                total = part if total is None else total + part
            p_scr[i, :, ks] = e.astype(BF16)
        totals.append(total)
    return totals


def _diff_attn_kernel(lam_ref, q_ref, *rest, lam_init, seg_rows):
    n_seg = len(seg_rows)
    k_refs, v_refs, g_ref = rest[:n_seg], rest[n_seg:2 * n_seg], rest[2 * n_seg]
    o_ref, k_scr, v_scr, s_scr, p_scr = rest[-5:]
    hd = DIFF_HEAD_DIM

    @pl.when(pl.program_id(2) == 0)
    def _():
        r0 = 0
        for k_ref, v_ref, n in zip(k_refs, v_refs, seg_rows):
            k_scr[r0:r0 + n] = k_ref[...]
            v_scr[r0:r0 + n] = v_ref[...]
            r0 += n

    lv = lam_ref[...]
    lam = (jnp.exp(jnp.sum(lv[0:1] * lv[1:2], axis=-1, keepdims=True))
           - jnp.exp(jnp.sum(lv[2:3] * lv[3:4], axis=-1, keepdims=True)) + lam_init)
    q = q_ref[...]
    cols = [slice(c * hd, (c + 1) * hd) for c in range(2)]
    totals = _softmax_numerators([q[:, c] for c in cols], k_scr, cols, s_scr, p_scr, True)
    outs = [_dot(p_scr[c], v_scr[...]) * (1.0 / totals[c]) for c in range(2)]
    o = outs[0] - lam * outs[1]
    o_ref[...] = (_rms(o, g_ref[...]) * (1.0 - lam_init)).astype(BF16)


def _diff_attn(qkvf, prev, lam_vec, subln_g, lam_init, dims, tq, latent):
    m_rows = qkvf.shape[0]
    batch, seq, ctx = dims["batch"], dims["seq"], dims["ctx"]
    hw = 2 * DIFF_HEAD_DIM
    width = qkvf.shape[1] // 4
    heads = width // hw
    ctx_blk0 = batch * seq // ctx
    qcol, kcol, vcol = width // hw, 2 * width // hw, 3 * width // hw
    if latent:
        nq, q_blk0, seg_rows = seq // tq, 0, (ctx, seq)
    else:
        assert tq == ctx
        nq, q_blk0, seg_rows = 1, ctx_blk0, (ctx,)
    n_keys = sum(seg_rows)

    def kv_specs(col):
        specs = [pl.BlockSpec((ctx, hw), lambda b, h, i: (ctx_blk0 + b, col + h))]
        if latent:
            specs.append(pl.BlockSpec((seq, hw), lambda b, h, i: (b, col + h)))
        return specs

    in_specs = ([pl.BlockSpec((4, DIFF_HEAD_DIM), lambda b, h, i: (0, 0)),
                 pl.BlockSpec((tq, hw), lambda b, h, i: (q_blk0 + b * nq + i, qcol + h))]
                + kv_specs(kcol) + kv_specs(vcol)
                + [pl.BlockSpec((1, hw), lambda b, h, i: (0, 0))])
    args = [lam_vec, qkvf] + [qkvf] * (2 * len(seg_rows)) + [subln_g.reshape(1, hw)]
    aliases = {}
    if prev is not None:
        in_specs.append(pl.BlockSpec(memory_space=pl.ANY))
        args.append(prev)
        aliases = {len(args) - 1: 0}
    kern = functools.partial(_diff_attn_kernel, lam_init=lam_init, seg_rows=seg_rows)
    return pl.pallas_call(
        kern,
        grid=(batch, heads, nq),
        in_specs=in_specs,
        out_specs=pl.BlockSpec((tq, hw), lambda b, h, i: (q_blk0 + b * nq + i, h)),
        out_shape=jax.ShapeDtypeStruct((m_rows, width), BF16),
        scratch_shapes=[pltpu.VMEM((n_keys, hw), BF16), pltpu.VMEM((n_keys, hw), BF16),
                        pltpu.VMEM((2, tq, n_keys), F32), pltpu.VMEM((2, tq, n_keys), BF16)],
        input_output_aliases=aliases,
        compiler_params=_cparams(3),
        name="ev_diff_attn_lat" if latent else "ev_diff_attn_ctx",
    )(*args)


def _fourier_kernel(u_ref, cc_ref, sc_ref, cs_ref, *rest, n, norm):
    o_ref, ab_ref = rest[-2], rest[-1]
    u = u_ref[...]
    ab_ref[0:n] = _dot(u, cc_ref[...]).astype(BF16)
    ab_ref[n:] = _dot(u, sc_ref[...]).astype(BF16)
    o_ref[...] = (_dot(cs_ref[...], ab_ref[...]) * norm).astype(BF16)


def _fourier(qkvf, prev, n, row_blk0, batch, width, cc, sc, cs, name):
    m_rows = qkvf.shape[0]
    gw = width // FOURIER_GROUPS
    kern = functools.partial(_fourier_kernel, n=n, norm=1.0 / math.sqrt(n * gw))
    in_specs = [pl.BlockSpec((n, gw), lambda b, g: (row_blk0 + b, g)),
                pl.BlockSpec((gw, gw), lambda b, g: (0, 0)),
                pl.BlockSpec((gw, gw), lambda b, g: (0, 0)),
                pl.BlockSpec((n, 2 * n), lambda b, g: (0, 0), pipeline_mode=pl.Buffered(1))]
    args = [qkvf, cc, sc, cs]
    aliases = {}
    if prev is not None:
        in_specs.append(pl.BlockSpec(memory_space=pl.ANY))
        args.append(prev)
        aliases = {4: 0}
    return pl.pallas_call(
        kern,
        grid=(batch, FOURIER_GROUPS),
        in_specs=in_specs,
        out_specs=pl.BlockSpec((n, gw), lambda b, g: (row_blk0 + b, g)),
        out_shape=jax.ShapeDtypeStruct((m_rows, width), BF16),
        scratch_shapes=[pltpu.VMEM((2 * n, gw), BF16)],
        input_output_aliases=aliases,
        compiler_params=_cparams(2),
        name=name,
    )(*args)


def _mm_res_kernel(*refs, n_a, gate_idx):
    a_refs, w_refs = refs[:n_a], refs[n_a:2 * n_a]
    x_ref, mod_ref, o_ref = refs[2 * n_a:]
    acc = _dot(a_refs[0][...], w_refs[0][...].astype(BF16))
    for a_ref, w_ref in zip(a_refs[1:], w_refs[1:]):
        acc = acc + _dot(a_ref[...], w_ref[...].astype(BF16))
    m = mod_ref[0]
    o_ref[...] = x_ref[...] + m[gate_idx:gate_idx + 1] * acc


def _mm_res(a_list, w, x, mod, gate_idx, dims, rows, tm, tn, name):
    d = w.shape[1]
    seq_tiles = dims["seq"] // tm
    n_a = len(a_list)

    def sel(i):
        return jnp.minimum(i // seq_tiles, dims["batch"])

    in_specs, w_args, k0 = [], [], 0
    for a in a_list:
        in_specs.append(pl.BlockSpec((tm, a.shape[1]), lambda i, j: (i, 0)))
    for a in a_list:
        ka = a.shape[1]
        assert k0 % ka == 0
        in_specs.append(pl.BlockSpec((ka, tn), lambda i, j, kb=k0 // ka: (kb, j)))
        w_args.append(w)
        k0 += ka
    assert k0 == w.shape[0]
    in_specs += [pl.BlockSpec((tm, tn), lambda i, j: (i, j)),
                 pl.BlockSpec((1, 6, tn), lambda i, j: (sel(i), 0, j))]
    return pl.pallas_call(
        functools.partial(_mm_res_kernel, n_a=n_a, gate_idx=gate_idx),
        grid=(rows // tm, d // tn),
        in_specs=in_specs,
        out_specs=pl.BlockSpec((tm, tn), lambda i, j: (i, j)),
        out_shape=jax.ShapeDtypeStruct((rows, d), F32),
        compiler_params=_cparams(2),
        name=name,
    )(*a_list, *w_args, x, mod)


def _glu_kernel(x_ref, g_ref, mod_ref, wg_ref, wu_ref, o_ref, h_ref):
    @pl.when(pl.program_id(1) == 0)
    def _():
        m = mod_ref[0]
        h_ref[...] = _norm_mod(x_ref[...], g_ref[...], m[3:4], m[4:5]).astype(BF16)

    h = h_ref[...]
    gate = _dot(h, wg_ref[...].astype(BF16))
    up = _dot(h, wu_ref[...].astype(BF16))
    o_ref[...] = (_silu(gate) * up).astype(BF16)


def _glu(x, g, mod, wg, wu, dims, tm, tf):
    m_rows, d = x.shape
    f = wg.shape[1]
    seq_tiles = dims["seq"] // tm

    def sel(i):
        return jnp.minimum(i // seq_tiles, dims["batch"])

    return pl.pallas_call(
        _glu_kernel,
        grid=(m_rows // tm, f // tf),
        in_specs=[pl.BlockSpec((tm, d), lambda i, j: (i, 0)),
                  pl.BlockSpec((1, d), lambda i, j: (0, 0)),
                  pl.BlockSpec((1, 6, d), lambda i, j: (sel(i), 0, 0)),
                  pl.BlockSpec((d, tf), lambda i, j: (0, j)),
                  pl.BlockSpec((d, tf), lambda i, j: (0, j))],
        out_specs=pl.BlockSpec((tm, tf), lambda i, j: (i, j)),
        out_shape=jax.ShapeDtypeStruct((m_rows, f), BF16),
        scratch_shapes=[pltpu.VMEM((tm, d), BF16)],
        compiler_params=_cparams(2),
        name="ffn_glu",
    )(x, g.reshape(1, d), mod, wg, wu)


def _mla_proj_kernel(x_ref, g_ref, mod_ref, wd_ref, qg_ref, kvg_ref, wuq_ref, wukv_ref,
                     tq_ref, tk_ref, q_ref, kv_ref, kr_ref, *, n_lat_tiles, q_lora, kv_lora, heads):
    i = pl.program_id(0)
    m = mod_ref[0]
    h = _norm_mod(x_ref[...], g_ref[...], m[0:1], m[1:2]).astype(BF16)
    t = _dot(h, wd_ref[...])
    ckv = _rms(t[:, q_lora:q_lora + kv_lora], kvg_ref[...]).astype(BF16)
    kv_ref[...] = _dot(ckv, wukv_ref[...]).astype(BF16)
    kr = t[:, q_lora + kv_lora:]
    kr_ref[...] = _apply_rope(kr, tk_ref, MLA_ROPE // 4).astype(BF16)

    @pl.when(i < n_lat_tiles)
    def _():
        cq = _rms(t[:, :q_lora], qg_ref[...]).astype(BF16)
        q = _dot(cq, wuq_ref[...])
        scale = (MLA_NOPE + MLA_ROPE) ** -0.5 * LOG2E
        for hh in range(heads):
            c0 = hh * 2 * LANES
            q_ref[:, c0:c0 + LANES] = (q[:, c0:c0 + LANES] * scale).astype(BF16)
            q_ref[:, c0 + LANES:c0 + 2 * LANES] = _apply_rope(
                q[:, c0 + LANES:c0 + 2 * LANES], tq_ref, MLA_ROPE // 4).astype(BF16)


def _mla_proj(x, g, mod, wd, qg, kvg, wuq, wukv, tab_q, tab_k, dims, tm):
    m_rows, d = x.shape
    seq, lat = dims["seq"], dims["lat"]
    q_lora, kv_lora = qg.shape[0], kvg.shape[0]
    heads = wukv.shape[1] // (MLA_NOPE + MLA_V)
    lat_tiles, seq_tiles = lat // tm, seq // tm

    def sel(i):
        return jnp.minimum(i // seq_tiles, dims["batch"])

    def tab_map(i):
        return (0, jnp.where(i < lat_tiles, i % seq_tiles, seq_tiles), 0)

    kern = functools.partial(_mla_proj_kernel, n_lat_tiles=lat_tiles, q_lora=q_lora, kv_lora=kv_lora,
                             heads=heads)
    const = lambda i: (0, 0)
    return pl.pallas_call(
        kern,
        grid=(m_rows // tm,),
        in_specs=[pl.BlockSpec((tm, d), lambda i: (i, 0)),
                  pl.BlockSpec((1, d), const),
                  pl.BlockSpec((1, 6, d), lambda i: (sel(i), 0, 0)),
                  pl.BlockSpec(wd.shape, const),
                  pl.BlockSpec((1, q_lora), const),
                  pl.BlockSpec((1, kv_lora), const),
                  pl.BlockSpec(wuq.shape, const),
                  pl.BlockSpec(wukv.shape, const),
                  pl.BlockSpec((3, tm, LANES), tab_map),
                  pl.BlockSpec((3, tm, LANES), tab_map)],
        out_specs=[pl.BlockSpec((tm, wuq.shape[1]), lambda i: (jnp.minimum(i, lat_tiles - 1), 0)),
                   pl.BlockSpec((tm, wukv.shape[1]), lambda i: (i, 0)),
                   pl.BlockSpec((tm, LANES), lambda i: (i, 0))],
        out_shape=[jax.ShapeDtypeStruct((lat, wuq.shape[1]), BF16),
                   jax.ShapeDtypeStruct((m_rows, wukv.shape[1]), BF16),
                   jax.ShapeDtypeStruct((m_rows, LANES), BF16)],
        compiler_params=_cparams(1),
        name="od_mla_proj",
    )(x, g.reshape(1, d), mod, wd, qg.reshape(1, q_lora), kvg.reshape(1, kv_lora), wuq, wukv, tab_q, tab_k)


def _mla_attn_kernel(q_ref, knc_ref, knl_ref, krc_ref, krl_ref, vc_ref, vl_ref, o_ref,
                     k_scr, v_scr, s_scr, p_scr, *, n_ctx):
    @pl.when(pl.program_id(2) == 0)
    def _():
        k_scr[0:n_ctx, 0:LANES] = knc_ref[...]
        k_scr[n_ctx:, 0:LANES] = knl_ref[...]
        k_scr[0:n_ctx, LANES:] = krc_ref[...]
        k_scr[n_ctx:, LANES:] = krl_ref[...]
        v_scr[0:n_ctx, 0:LANES] = vc_ref[...]
        v_scr[n_ctx:, 0:LANES] = vl_ref[...]
        v_scr[:, LANES:] = jnp.ones((v_scr.shape[0], LANES), BF16)

    half = q_ref.shape[0] // 2
    halves = [slice(0, half), slice(half, 2 * half)]
    _softmax_numerators([q_ref[r] for r in halves], k_scr, [slice(None)] * 2, s_scr, p_scr, False)
    for i, r in enumerate(halves):
        acc = _dot(p_scr[i], v_scr[...])
        o_ref[r] = (acc[:, :LANES] / acc[:, LANES:LANES + 1]).astype(BF16)


def _mla_attn(q, kv, kr, dims, tq):
    batch, seq, ctx, lat = dims["batch"], dims["seq"], dims["ctx"], dims["lat"]
    heads = q.shape[1] // (2 * LANES)
    nq = seq // tq
    ctx_blk0 = lat // ctx
    return pl.pallas_call(
        functools.partial(_mla_attn_kernel, n_ctx=ctx),
        grid=(batch, heads, nq),
        in_specs=[pl.BlockSpec((tq, 2 * LANES), lambda b, h, i: (b * nq + i, h)),
                  pl.BlockSpec((ctx, LANES), lambda b, h, i: (ctx_blk0 + b, 2 * h)),
                  pl.BlockSpec((seq, LANES), lambda b, h, i: (b, 2 * h)),
                  pl.BlockSpec((ctx, LANES), lambda b, h, i: (ctx_blk0 + b, 0)),
                  pl.BlockSpec((seq, LANES), lambda b, h, i: (b, 0)),
                  pl.BlockSpec((ctx, LANES), lambda b, h, i: (ctx_blk0 + b, 2 * h + 1)),
                  pl.BlockSpec((seq, LANES), lambda b, h, i: (b, 2 * h + 1))],
        out_specs=pl.BlockSpec((tq, LANES), lambda b, h, i: (b * nq + i, h)),
        out_shape=jax.ShapeDtypeStruct((lat, heads * LANES), BF16),
        scratch_shapes=[pltpu.VMEM((ctx + seq, 2 * LANES), BF16), pltpu.VMEM((ctx + seq, 2 * LANES), BF16),
                        pltpu.VMEM((2, tq // 2, ctx + seq), F32), pltpu.VMEM((2, tq // 2, ctx + seq), BF16)],
        compiler_params=_cparams(3),
        name="od_mla_attn",
    )(q, kv, kv, kr, kr, kv, kv)


def _router_kernel(x_ref, g_ref, mod_ref, r_ref, h_ref, idx_ref, w_ref):
    m = mod_ref[0]
    h = _norm_mod(x_ref[...], g_ref[...], m[3:4], m[4:5])
    h_ref[...] = h
    r = r_ref[...]
    h_hi = h.astype(BF16)
    h_lo = (h - h_hi.astype(F32)).astype(BF16)
    r_hi = r.astype(BF16)
    r_lo = (r - r_hi.astype(F32)).astype(BF16)
    logits = _dot(h_hi, r_hi) + (_dot(h_lo, r_hi) + _dot(h_hi, r_lo))
    lane = lax.broadcasted_iota(jnp.int32, logits.shape, 1)
    lane_f = lane.astype(F32)
    neg = jnp.float32(-jnp.inf)
    logits = jnp.where(lane < N_EXPERTS, logits, neg)
    m1 = jnp.max(logits, axis=-1, keepdims=True)
    i1 = jnp.min(jnp.where(logits == m1, lane_f, float(LANES)), axis=-1, keepdims=True)
    rest = jnp.where(lane_f == i1, neg, logits)
    m2 = jnp.max(rest, axis=-1, keepdims=True)
    i2 = jnp.min(jnp.where(rest == m2, lane_f, float(LANES)), axis=-1, keepdims=True)
    e2 = jnp.exp(m2 - m1)
    w1 = 1.0 / (1.0 + e2)
    w2 = e2 / (1.0 + e2)
    idx_ref[...] = jnp.where(lane == 0, i1, jnp.where(lane == 1, i2, 0.0)).astype(jnp.int32)
    w_ref[...] = jnp.where(lane == 0, w1, jnp.where(lane == 1, w2, 0.0))


def _router(x, g, mod, router_pad, dims, tm):
    lat, d = x.shape
    seq_tiles = dims["seq"] // tm
    return pl.pallas_call(
        _router_kernel,
        grid=(lat // tm,),
        in_specs=[pl.BlockSpec((tm, d), lambda i: (i, 0)),
                  pl.BlockSpec((1, d), lambda i: (0, 0)),
                  pl.BlockSpec((1, 6, d), lambda i: (i // seq_tiles, 0, 0)),
                  pl.BlockSpec((d, LANES), lambda i: (0, 0))],
        out_specs=[pl.BlockSpec((tm, d), lambda i: (i, 0)),
                   pl.BlockSpec((tm, LANES), lambda i: (i, 0)),
                   pl.BlockSpec((tm, LANES), lambda i: (i, 0))],
        out_shape=[jax.ShapeDtypeStruct((lat, d), F32),
                   jax.ShapeDtypeStruct((lat, LANES), jnp.int32),
                   jax.ShapeDtypeStruct((lat, LANES), F32)],
        compiler_params=_cparams(1),
        name="moe_router",
    )(x, g.reshape(1, d), mod, router_pad)


def _moe_plan(top_idx, n_items):
    e_flat = top_idx.reshape(-1)
    n_assign = e_flat.shape[0]
    onehot = (e_flat[:, None] == jnp.arange(N_EXPERTS, dtype=jnp.int32)[None, :]).astype(jnp.int32)
    csum = jnp.cumsum(onehot, axis=0)
    counts = csum[-1]
    rank = jnp.sum(csum * onehot, axis=1) - 1
    blocks = (counts + MOE_BLOCK - 1) // MOE_BLOCK
    blk_end = jnp.cumsum(blocks)
    blk_start = blk_end - blocks
    total = blk_end[-1]
    dest = blk_start[e_flat] * MOE_BLOCK + rank
    row_tok = jnp.zeros((n_items * MOE_BLOCK,), jnp.int32).at[dest].set(
        jnp.arange(n_assign, dtype=jnp.int32) // TOP_K)
    p = jnp.arange(n_items, dtype=jnp.int32)
    pc = jnp.minimum(p, total - 1)
    item_e = jnp.sum((pc[:, None] >= blk_end[None, :]).astype(jnp.int32), axis=1)
    rows_left = counts[item_e] - (pc - blk_start[item_e]) * MOE_BLOCK
    nact = jnp.clip((rows_left + MOE_SUB - 1) // MOE_SUB, 0, MOE_BLOCK // MOE_SUB)
    nact = jnp.where(p < total, nact, 0).astype(jnp.int32)
    out_blk = jnp.where(p < total, p, n_items).astype(jnp.int32)
    items = (item_e.astype(jnp.int32), pc.astype(jnp.int32), out_blk, nact)
    return dest.astype(jnp.int32), row_tok, items, (blocks, blk_start, blk_end, total)


def _moe_steps(items, runs, n_items, nj):
    _, _, _, nact = items
    blocks, blk_start, blk_end, total = runs
    s = jnp.arange(n_items * nj, dtype=jnp.int32)
    live = s < total * nj
    sc = jnp.minimum(s, total * nj - 1)
    e = jnp.sum((sc[:, None] >= (blk_end * nj)[None, :]).astype(jnp.int32), axis=1)
    t = sc - blk_start[e] * nj
    j = t // blocks[e]
    blk = blk_start[e] + t % blocks[e]
    out_blk = jnp.where(live, blk, n_items)
    out_j = jnp.where(live, j, 0)
    step_nact = jnp.where(live, nact[blk], 0)
    return tuple(a.astype(jnp.int32) for a in (e, blk, j, out_blk, out_j, step_nact))


def _gather_kernel(tok_ref, nact_ref, h_ref, o_ref, buf, sem):
    p = pl.program_id(0)
    nact = nact_ref[p]
    base = p * MOE_BLOCK

    def copy(r, t):
        return pltpu.make_async_copy(h_ref.at[pl.ds(t, 1)], buf.at[pl.ds(r, 1)], sem)

    def start(grp, c):
        for u in range(GATHER_UNROLL):
            r = grp * GATHER_UNROLL + u
            copy(r, tok_ref[base + r]).start()
        return c

    def wait(grp, c):
        for u in range(GATHER_UNROLL):
            copy(grp * GATHER_UNROLL + u, 0).wait()
        return c

    n_grp = nact * (MOE_SUB // GATHER_UNROLL)
    lax.fori_loop(0, n_grp, start, 0)
    lax.fori_loop(0, n_grp, wait, 0)
    for s in range(MOE_BLOCK // MOE_SUB):
        rows = pl.ds(s * MOE_SUB, MOE_SUB)

        @pl.when(s < nact)
        def _():
            o_ref[rows] = buf[rows].astype(BF16)

        @pl.when(s >= nact)
        def _():
            o_ref[rows] = jnp.zeros((MOE_SUB, o_ref.shape[1]), BF16)


def _moe_gather(h, row_tok, nact, n_items):
    d = h.shape[1]
    return pl.pallas_call(
        _gather_kernel,
        grid_spec=pltpu.PrefetchScalarGridSpec(
            num_scalar_prefetch=2,
            grid=(n_items,),
            in_specs=[pl.BlockSpec(memory_space=pl.ANY)],
            out_specs=pl.BlockSpec((MOE_BLOCK, d), lambda p, tok, na: (p, 0)),
            scratch_shapes=[pltpu.VMEM((MOE_BLOCK, d), F32), pltpu.SemaphoreType.DMA(())]),
        out_shape=jax.ShapeDtypeStruct((n_items * MOE_BLOCK, d), BF16),
        compiler_params=_cparams(1),
        name="moe_gather",
    )(row_tok, nact, h)


def _for_active_rows(nact, in_ref, o_ref, fn):
    n_sub = MOE_BLOCK // MOE_SUB
    for k in range(n_sub + 1):
        @pl.when(nact == k)
        def _():
            if k > 0:
                o_ref[0:k * MOE_SUB] = fn(in_ref[0:k * MOE_SUB])
            if k < n_sub:
                o_ref[k * MOE_SUB:] = jnp.zeros((MOE_BLOCK - k * MOE_SUB, o_ref.shape[1]), o_ref.dtype)


def _moe_glu_kernel(e_ref, blk_ref, j_ref, oblk_ref, oj_ref, nact_ref, h_ref, wg_ref, wu_ref, o_ref):
    nact = nact_ref[pl.program_id(0)]

    def run(h):
        gate = _dot(h, wg_ref[0].astype(BF16))
        return (_silu(gate) * _dot(h, wu_ref[0].astype(BF16))).astype(BF16)

    _for_active_rows(nact, h_ref, o_ref, run)


def _moe_glu(hs, wg, wu, steps, n_items, tf):
    d, f = wg.shape[1], wg.shape[2]
    nj = f // tf

    def w_map(s, e, b, j, ob, oj, na):
        return (e[s], 0, j[s])

    return pl.pallas_call(
        _moe_glu_kernel,
        grid_spec=pltpu.PrefetchScalarGridSpec(
            num_scalar_prefetch=6,
            grid=(n_items * nj,),
            in_specs=[pl.BlockSpec((MOE_BLOCK, d), lambda s, e, b, j, ob, oj, na: (b[s], 0)),
                      pl.BlockSpec((1, d, tf), w_map),
                      pl.BlockSpec((1, d, tf), w_map)],
            out_specs=pl.BlockSpec((MOE_BLOCK, tf), lambda s, e, b, j, ob, oj, na: (ob[s], oj[s]))),
        out_shape=jax.ShapeDtypeStruct(((n_items + 1) * MOE_BLOCK, f), BF16),
        compiler_params=_cparams(1),
        name="moe_glu",
    )(*steps, hs, wg, wu)


def _moe_down_kernel(e_ref, blk_ref, oblk_ref, nact_ref, a_ref, wd_ref, o_ref):
    nact = nact_ref[pl.program_id(0)]
    _for_active_rows(nact, a_ref, o_ref, lambda a: _dot(a, wd_ref[0].astype(BF16)))


def _moe_down(a, wd, plan, n_items, tn):
    item_e, in_blk, out_blk, nact = plan
    f, d = wd.shape[1], wd.shape[2]
    nj = d // tn

    def w_map(p, j, e, b, ob, na):
        return (e[p], 0, jnp.where(na[p] > 0, j, nj - 1))

    def o_map(p, j, e, b, ob, na):
        return (ob[p], jnp.where(na[p] > 0, j, 0))

    return pl.pallas_call(
        _moe_down_kernel,
        grid_spec=pltpu.PrefetchScalarGridSpec(
            num_scalar_prefetch=4,
            grid=(n_items, nj),
            in_specs=[pl.BlockSpec((MOE_BLOCK, f), lambda p, j, e, b, ob, na: (b[p], 0)),
                      pl.BlockSpec((1, f, tn), w_map)],
            out_specs=pl.BlockSpec((MOE_BLOCK, tn), o_map)),
        out_shape=jax.ShapeDtypeStruct(((n_items + 1) * MOE_BLOCK, d), F32),
        compiler_params=_cparams(2),
        name="moe_down",
    )(item_e, in_blk, out_blk, nact, a, wd)


def _combine_kernel(dest_ref, x_ref, mod_ref, w_ref, fg_ref, y_ref, o_ref, buf, sem, *, tm):
    i = pl.program_id(0)

    def copy(r, k, row):
        return pltpu.make_async_copy(y_ref.at[pl.ds(row, 1)], buf.at[k, pl.ds(r, 1)], sem)

    def start(grp, c):
        for u in range(GATHER_UNROLL):
            r = grp * GATHER_UNROLL + u
            for k in range(TOP_K):
                copy(r, k, dest_ref[(i * tm + r) * TOP_K + k]).start()
        return c

    def wait(grp, c):
        for u in range(GATHER_UNROLL):
            for k in range(TOP_K):
                copy(grp * GATHER_UNROLL + u, k, 0).wait()
        return c

    lax.fori_loop(0, tm // GATHER_UNROLL, start, 0)
    lax.fori_loop(0, tm // GATHER_UNROLL, wait, 0)
    w = w_ref[...]
    moe = w[:, 0:1] * buf[0] + w[:, 1:2] * buf[1]
    m = mod_ref[0]
    o_ref[...] = _rms(x_ref[...] + m[5:6] * moe, fg_ref[...])


def _moe_combine(dest, x, mod, top_w, final_g, ys, dims, tm):
    lat, d = x.shape
    seq_tiles = dims["seq"] // tm
    return pl.pallas_call(
        functools.partial(_combine_kernel, tm=tm),
        grid_spec=pltpu.PrefetchScalarGridSpec(
            num_scalar_prefetch=1,
            grid=(lat // tm,),
            in_specs=[pl.BlockSpec((tm, d), lambda i, dr: (i, 0)),
                      pl.BlockSpec((1, 6, d), lambda i, dr: (i // seq_tiles, 0, 0)),
                      pl.BlockSpec((tm, LANES), lambda i, dr: (i, 0)),
                      pl.BlockSpec((1, d), lambda i, dr: (0, 0)),
                      pl.BlockSpec(memory_space=pl.ANY)],
            out_specs=pl.BlockSpec((tm, d), lambda i, dr: (i, 0)),
            scratch_shapes=[pltpu.VMEM((TOP_K, tm, d), F32), pltpu.SemaphoreType.DMA(())]),
        out_shape=jax.ShapeDtypeStruct((lat, d), F32),
        compiler_params=_cparams(1),
        name="moe_combine",
    )(dest, x, mod, top_w, final_g.reshape(1, d), ys)


def kernel(x, c, ctx, c_ctx, ada_w, ada_b, norm1_g, norm2_g, ev_w_in, ev_w_out, ev_lambda, ev_subln_g, od_w_dq, od_q_norm_g, od_w_uq, od_w_dkv, od_kv_norm_g, od_w_ukv, od_w_o, ffn_w_gate, ffn_w_up, ffn_w_down, moe_router, moe_w_gate, moe_w_up, moe_w_down, final_norm_g):
    batch, seq, d = x.shape
    n_ctx = ctx.shape[1]
    depth = ada_w.shape[0]
    assert depth == 2 and batch < MOD_ROWS and seq % GRID_W == 0
    lat = batch * seq
    dims = dict(batch=batch, seq=seq, ctx=n_ctx, lat=lat)
    tm = 1024
    assert seq % tm == 0 and (batch * n_ctx) % tm == 0

    cond = jnp.concatenate([c, c_ctx[None, :], jnp.zeros((MOD_ROWS - batch - 1, d), F32)], axis=0)
    mod = _ada(cond, ada_w, ada_b).reshape(depth, MOD_ROWS, 6, d)
    xs = jnp.concatenate([x.reshape(lat, d), ctx.reshape(batch * n_ctx, d)], axis=0)

    lam_init = 0.8 - 0.6 * math.exp(-0.3 * 0)
    fw = ev_w_in.shape[2] // 4
    tabs_ev = jnp.asarray(np.stack([
        _rope_tables(seq, tm, DIFF_HEAD_DIM // 4, DIFF_HEAD_DIM ** -0.5 * LOG2E),
        _rope_tables(seq, tm, DIFF_HEAD_DIM // 4, 1.0)]))
    qkvf = _inproj(xs, norm1_g[0], mod[0], ev_w_in[0], tabs_ev, dims, tm, 512)
    o_attn = _diff_attn(qkvf, None, ev_lambda[0], ev_subln_g[0], lam_init, dims, 512, True)
    o_attn = _diff_attn(qkvf, o_attn, ev_lambda[0], ev_subln_g[0], lam_init, dims, n_ctx, False)

    gw = fw // FOURIER_GROUPS
    cc_np, sc_np = _dft_cos_sin(gw)
    cc = jnp.asarray(cc_np.astype(np.float32)).astype(BF16)
    sc = jnp.asarray(sc_np.astype(np.float32)).astype(BF16)
    cn_np, sn_np = _dft_cos_sin(seq)
    cs_lat = jnp.asarray(np.concatenate([cn_np, -sn_np], axis=1).astype(np.float32)).astype(BF16)
    cx_np, sx_np = _dft_cos_sin(n_ctx)
    cs_ctx = jnp.asarray(np.concatenate([cx_np, -sx_np], axis=1).astype(np.float32)).astype(BF16)
    fm = _fourier(qkvf, None, seq, 0, batch, fw, cc, sc, cs_lat, "ev_fourier_lat")
    fm = _fourier(qkvf, fm, n_ctx, lat // n_ctx, batch, fw, cc, sc, cs_ctx, "ev_fourier_ctx")

    m_rows = xs.shape[0]
    xs = _mm_res([fm, o_attn], ev_w_out[0], xs, mod[0], 2, dims, m_rows, tm, 512, "ev_outproj")
    act = _glu(xs, norm2_g[0], mod[0], ffn_w_gate[0], ffn_w_up[0], dims, tm, 512)
    xs = _mm_res([act], ffn_w_down[0], xs, mod[0], 5, dims, m_rows, tm, 256, "ffn_down")

    heads = od_w_ukv.shape[2] // (MLA_NOPE + MLA_V)
    q_lora = od_w_dq.shape[2]
    wd_cat = jnp.concatenate(
        [od_w_dq[0], od_w_dkv[0], jnp.zeros((d, LANES - MLA_ROPE), F32)], axis=1).astype(BF16)
    wuq = jnp.pad(od_w_uq[0].reshape(q_lora, heads, MLA_NOPE + MLA_ROPE),
                  ((0, 0), (0, 0), (0, 2 * LANES - MLA_NOPE - MLA_ROPE))).reshape(q_lora, heads * 2 * LANES)
    tm_mla = 256
    tab_q = jnp.asarray(_rope_tables(seq, tm_mla, MLA_ROPE // 4, (MLA_NOPE + MLA_ROPE) ** -0.5 * LOG2E))
    tab_k = jnp.asarray(_rope_tables(seq, tm_mla, MLA_ROPE // 4, 1.0))
    q, kv, kr = _mla_proj(xs, norm1_g[1], mod[1], wd_cat, od_q_norm_g[0], od_kv_norm_g[0],
                          wuq.astype(BF16), od_w_ukv[0].astype(BF16), tab_q, tab_k, dims, tm_mla)
    o_mla = _mla_attn(q, kv, kr, dims, 1024)
    xl = _mm_res([o_mla], od_w_o[0], xs, mod[1], 2, dims, lat, tm, 512, "od_outproj")

    router_pad = jnp.pad(moe_router[0], ((0, 0), (0, LANES - N_EXPERTS)))
    h2, top_idx, top_w = _router(xl, norm2_g[1], mod[1], router_pad, dims, 512)
    n_items = lat * TOP_K // MOE_BLOCK + N_EXPERTS
    dest, row_tok, items, runs = _moe_plan(top_idx[:, :TOP_K], n_items)
    tf = 512
    steps = _moe_steps(items, runs, n_items, moe_w_gate.shape[3] // tf)
    hs = _moe_gather(h2, row_tok, items[3], n_items)
    act = _moe_glu(hs, moe_w_gate[0], moe_w_up[0], steps, n_items, tf)
    ys = _moe_down(act, moe_w_down[0], items, n_items, 256)
    out = _moe_combine(dest, xl, mod[1], top_w, final_norm_g, ys, dims, 256)
    return out.reshape(batch, seq, d)
```

```python
import functools
import math

import numpy as np
import jax
import jax.numpy as jnp
from jax import lax
from jax.experimental import pallas as pl
from jax.experimental.pallas import tpu as pltpu

F32 = jnp.float32
BF16 = jnp.bfloat16

GRID_W = 64
NORM_EPS = 1e-6
ROPE_BASE = 10000.0
FOURIER_GROUPS = 4
DIFF_HEAD_DIM = 128
MLA_NOPE = 128
MLA_ROPE = 64
MLA_V = 128
N_EXPERTS = 8
TOP_K = 2

LANES = 128
MOD_ROWS = 8
VMEM_LIMIT = 56 * 1024 * 1024
MOE_BLOCK = 1024
MOE_SUB = 256
GATHER_UNROLL = 8
KEY_CHUNK = 256
LOG2E = math.log2(math.e)


def _cparams(n_axes):
    return pltpu.CompilerParams(dimension_semantics=("arbitrary",) * n_axes,
                                vmem_limit_bytes=VMEM_LIMIT)


def _rms(x, g):
    return x * lax.rsqrt(jnp.mean(x * x, axis=-1, keepdims=True) + NORM_EPS) * g


def _norm_mod(x, g, shift, scale):
    return _rms(x, g) * (1.0 + scale) + shift


def _silu(x):
    return x * (1.0 / (1.0 + jnp.exp(-x)))


def _dot(a, b):
    return jnp.dot(a, b, preferred_element_type=F32)


def _dot_nt(a, b):
    return lax.dot_general(a, b, (((1,), (1,)), ((), ())), preferred_element_type=F32)


def _rope_tables(seq, extra_rows, chunk, scale):
    n = np.arange(seq)
    row, col = n // GRID_W, n % GRID_W
    lane = np.arange(LANES)
    a = 2 * chunk
    inv = ROPE_BASE ** (-np.arange(0, a, 2, dtype=np.float64) / a)
    used = lane < 4 * chunk
    freq = inv[lane % chunk]
    pos = np.where(lane[None, :] < 2 * chunk, row[:, None], col[:, None]).astype(np.float64)
    ang = pos * freq[None, :]
    first = (lane // chunk) % 2 == 0
    cos = np.where(used[None, :], np.cos(ang), 0.0)
    sin = np.where(used[None, :], np.sin(ang), 0.0)
    s1 = np.where(first[None, :], -sin, 0.0)
    s2 = np.where(first[None, :], 0.0, sin)
    ident = np.zeros((3, extra_rows, LANES))
    ident[0] = used[None, :].astype(np.float64)
    tab = np.concatenate([np.stack([cos, s1, s2]), ident], axis=1) * scale
    return tab.astype(np.float32)


def _apply_rope(x, tab_ref, chunk):
    return (x * tab_ref[0] + pltpu.roll(x, LANES - chunk, 1) * tab_ref[1]
            + pltpu.roll(x, chunk, 1) * tab_ref[2])


def _dft_cos_sin(n):
    k = np.arange(n)
    ang = 2.0 * np.pi * ((k[:, None] * k[None, :]) % n) / n
    return np.cos(ang), np.sin(ang)


def _ada_kernel(s_ref, w_ref, b_ref, o_ref):
    s = _silu(s_ref[...]).astype(BF16)
    o_ref[0] = _dot(s, w_ref[0].astype(BF16)) + b_ref[0]


def _ada(cond, ada_w, ada_b):
    depth, d, n = ada_w.shape
    tn = 1024
    return pl.pallas_call(
        _ada_kernel,
        grid=(depth, n // tn),
        in_specs=[pl.BlockSpec((MOD_ROWS, d), lambda i, j: (0, 0)),
                  pl.BlockSpec((1, d, tn), lambda i, j: (i, 0, j)),
                  pl.BlockSpec((1, 1, tn), lambda i, j: (i, 0, j))],
        out_specs=pl.BlockSpec((1, MOD_ROWS, tn), lambda i, j: (i, 0, j)),
        out_shape=jax.ShapeDtypeStruct((depth, MOD_ROWS, n), F32),
        compiler_params=_cparams(2),
        name="ada",
    )(cond, ada_w, ada_b.reshape(depth, 1, n))


def _inproj_kernel(x_ref, g_ref, mod_ref, w_ref, tab_ref, o_ref, h_ref):
    j = pl.program_id(1)
    quarter = pl.num_programs(1) // 4

    @pl.when(j == 0)
    def _():
        m = mod_ref[0]
        h_ref[...] = _norm_mod(x_ref[...], g_ref[...], m[0:1], m[1:2]).astype(BF16)

    is_rope = jnp.logical_and(j >= quarter, j < 3 * quarter)

    @pl.when(is_rope)
    def _():
        res = _dot(h_ref[...], w_ref[...].astype(BF16))
        for c in range(res.shape[1] // LANES):
            sl = slice(c * LANES, (c + 1) * LANES)
            o_ref[:, sl] = _apply_rope(res[:, sl], tab_ref.at[0], DIFF_HEAD_DIM // 4).astype(BF16)

    @pl.when(jnp.logical_not(is_rope))
    def _():
        o_ref[...] = _dot(h_ref[...], w_ref[...].astype(BF16)).astype(BF16)


def _inproj(x, g, mod, w, tabs, dims, tm, tn):
    m_rows, d = x.shape
    seq, lat = dims["seq"], dims["lat"]
    n = w.shape[1]
    nj = n // tn
    assert nj % 4 == 0
    lat_tiles, seq_tiles = lat // tm, seq // tm

    def sel(i):
        return jnp.minimum(i // seq_tiles, dims["batch"])

    def tab_map(i, j):
        return (jnp.where(j >= nj // 2, 1, 0), 0, jnp.where(i < lat_tiles, i % seq_tiles, seq_tiles), 0)

    return pl.pallas_call(
        _inproj_kernel,
        grid=(m_rows // tm, nj),
        in_specs=[pl.BlockSpec((tm, d), lambda i, j: (i, 0)),
                  pl.BlockSpec((1, d), lambda i, j: (0, 0)),
                  pl.BlockSpec((1, 6, d), lambda i, j: (sel(i), 0, 0)),
                  pl.BlockSpec((d, tn), lambda i, j: (0, j)),
                  pl.BlockSpec((1, 3, tm, LANES), tab_map)],
        out_specs=pl.BlockSpec((tm, tn), lambda i, j: (i, j)),
        out_shape=jax.ShapeDtypeStruct((m_rows, n), BF16),
        scratch_shapes=[pltpu.VMEM((tm, d), BF16)],
        compiler_params=_cparams(2),
        name="ev_inproj",
    )(x, g.reshape(1, d), mod, w, tabs)


def _softmax_numerators(qs, k_ref, kcols, s_scr, p_scr, want_sum):
    n_keys = k_ref.shape[0]
    chunks = [slice(c, c + KEY_CHUNK) for c in range(0, n_keys, KEY_CHUNK)]
    maxes = []
    for i, (q, cols) in enumerate(zip(qs, kcols)):
        m = None
        for ks in chunks:
            s = _dot_nt(q, k_ref[ks, cols])
            s_scr[i, :, ks] = s
            mc = jnp.max(s, axis=-1, keepdims=True)
            m = mc if m is None else jnp.maximum(m, mc)
        maxes.append(m)
    totals = []
    for i, m in enumerate(maxes):
        total = None
        for ks in chunks:
            e = jnp.exp2(s_scr[i, :, ks] - m)
            if want_sum:
                part = jnp.sum(e, axis=-1, keepdims=True)
                total = part if total is None else total + part
            p_scr[i, :, ks] = e.astype(BF16)
        totals.append(total)
    return totals


def _diff_attn_kernel(lam_ref, q_ref, *rest, lam_init, seg_rows):
    n_seg = len(seg_rows)
    k_refs, v_refs, g_ref = rest[:n_seg], rest[n_seg:2 * n_seg], rest[2 * n_seg]
    o_ref, k_scr, v_scr, s_scr, p_scr = rest[-5:]
    hd = DIFF_HEAD_DIM

    @pl.when(pl.program_id(2) == 0)
    def _():
        r0 = 0
        for k_ref, v_ref, n in zip(k_refs, v_refs, seg_rows):
            k_scr[r0:r0 + n] = k_ref[...]
            v_scr[r0:r0 + n] = v_ref[...]
            r0 += n

    lv = lam_ref[...]
    lam = (jnp.exp(jnp.sum(lv[0:1] * lv[1:2], axis=-1, keepdims=True))
           - jnp.exp(jnp.sum(lv[2:3] * lv[3:4], axis=-1, keepdims=True)) + lam_init)
    q = q_ref[...]
    cols = [slice(c * hd, (c + 1) * hd) for c in range(2)]
    totals = _softmax_numerators([q[:, c] for c in cols], k_scr, cols, s_scr, p_scr, True)
    outs = [_dot(p_scr[c], v_scr[...]) * (1.0 / totals[c]) for c in range(2)]
    o = outs[0] - lam * outs[1]
    o_ref[...] = (_rms(o, g_ref[...]) * (1.0 - lam_init)).astype(BF16)


def _diff_attn(qkvf, prev, lam_vec, subln_g, lam_init, dims, tq, latent):
    m_rows = qkvf.shape[0]
    batch, seq, ctx = dims["batch"], dims["seq"], dims["ctx"]
    hw = 2 * DIFF_HEAD_DIM
    width = qkvf.shape[1] // 4
    heads = width // hw
    ctx_blk0 = batch * seq // ctx
    qcol, kcol, vcol = width // hw, 2 * width // hw, 3 * width // hw
    if latent:
        nq, q_blk0, seg_rows = seq // tq, 0, (ctx, seq)
    else:
        assert tq == ctx
        nq, q_blk0, seg_rows = 1, ctx_blk0, (ctx,)
    n_keys = sum(seg_rows)

    def kv_specs(col):
        specs = [pl.BlockSpec((ctx, hw), lambda b, h, i: (ctx_blk0 + b, col + h))]
        if latent:
            specs.append(pl.BlockSpec((seq, hw), lambda b, h, i: (b, col + h)))
        return specs

    in_specs = ([pl.BlockSpec((4, DIFF_HEAD_DIM), lambda b, h, i: (0, 0)),
                 pl.BlockSpec((tq, hw), lambda b, h, i: (q_blk0 + b * nq + i, qcol + h))]
                + kv_specs(kcol) + kv_specs(vcol)
                + [pl.BlockSpec((1, hw), lambda b, h, i: (0, 0))])
    args = [lam_vec, qkvf] + [qkvf] * (2 * len(seg_rows)) + [subln_g.reshape(1, hw)]
    aliases = {}
    if prev is not None:
        in_specs.append(pl.BlockSpec(memory_space=pl.ANY))
        args.append(prev)
        aliases = {len(args) - 1: 0}
    kern = functools.partial(_diff_attn_kernel, lam_init=lam_init, seg_rows=seg_rows)
    return pl.pallas_call(
        kern,
        grid=(batch, heads, nq),
        in_specs=in_specs,
        out_specs=pl.BlockSpec((tq, hw), lambda b, h, i: (q_blk0 + b * nq + i, h)),
        out_shape=jax.ShapeDtypeStruct((m_rows, width), BF16),
        scratch_shapes=[pltpu.VMEM((n_keys, hw), BF16), pltpu.VMEM((n_keys, hw), BF16),
                        pltpu.VMEM((2, tq, n_keys), F32), pltpu.VMEM((2, tq, n_keys), BF16)],
        input_output_aliases=aliases,
        compiler_params=_cparams(3),
        name="ev_diff_attn_lat" if latent else "ev_diff_attn_ctx",
    )(*args)


def _fourier_kernel(u_ref, cc_ref, sc_ref, cs_ref, *rest, n, norm):
    o_ref, ab_ref = rest[-2], rest[-1]
    u = u_ref[...]
    ab_ref[0:n] = _dot(u, cc_ref[...]).astype(BF16)
    ab_ref[n:] = _dot(u, sc_ref[...]).astype(BF16)
    o_ref[...] = (_dot(cs_ref[...], ab_ref[...]) * norm).astype(BF16)


def _fourier(qkvf, prev, n, row_blk0, batch, width, cc, sc, cs, name):
    m_rows = qkvf.shape[0]
    gw = width // FOURIER_GROUPS
    kern = functools.partial(_fourier_kernel, n=n, norm=1.0 / math.sqrt(n * gw))
    in_specs = [pl.BlockSpec((n, gw), lambda b, g: (row_blk0 + b, g)),
                pl.BlockSpec((gw, gw), lambda b, g: (0, 0)),
                pl.BlockSpec((gw, gw), lambda b, g: (0, 0)),
                pl.BlockSpec((n, 2 * n), lambda b, g: (0, 0), pipeline_mode=pl.Buffered(1))]
    args = [qkvf, cc, sc, cs]
    aliases = {}
    if prev is not None:
        in_specs.append(pl.BlockSpec(memory_space=pl.ANY))
        args.append(prev)
        aliases = {4: 0}
    return pl.pallas_call(
        kern,
        grid=(batch, FOURIER_GROUPS),
        in_specs=in_specs,
        out_specs=pl.BlockSpec((n, gw), lambda b, g: (row_blk0 + b, g)),
        out_shape=jax.ShapeDtypeStruct((m_rows, width), BF16),
        scratch_shapes=[pltpu.VMEM((2 * n, gw), BF16)],
        input_output_aliases=aliases,
        compiler_params=_cparams(2),
        name=name,
    )(*args)


def _mm_res_kernel(*refs, n_a, gate_idx):
    a_refs, w_refs = refs[:n_a], refs[n_a:2 * n_a]
    x_ref, mod_ref, o_ref = refs[2 * n_a:]
    acc = _dot(a_refs[0][...], w_refs[0][...].astype(BF16))
    for a_ref, w_ref in zip(a_refs[1:], w_refs[1:]):
        acc = acc + _dot(a_ref[...], w_ref[...].astype(BF16))
    m = mod_ref[0]
    o_ref[...] = x_ref[...] + m[gate_idx:gate_idx + 1] * acc


def _mm_res(a_list, w, x, mod, gate_idx, dims, rows, tm, tn, name):
    d = w.shape[1]
    seq_tiles = dims["seq"] // tm
    n_a = len(a_list)

    def sel(i):
        return jnp.minimum(i // seq_tiles, dims["batch"])

    in_specs, w_args, k0 = [], [], 0
    for a in a_list:
        in_specs.append(pl.BlockSpec((tm, a.shape[1]), lambda i, j: (i, 0)))
    for a in a_list:
        ka = a.shape[1]
        assert k0 % ka == 0
        in_specs.append(pl.BlockSpec((ka, tn), lambda i, j, kb=k0 // ka: (kb, j)))
        w_args.append(w)
        k0 += ka
    assert k0 == w.shape[0]
    in_specs += [pl.BlockSpec((tm, tn), lambda i, j: (i, j)),
                 pl.BlockSpec((1, 6, tn), lambda i, j: (sel(i), 0, j))]
    return pl.pallas_call(
        functools.partial(_mm_res_kernel, n_a=n_a, gate_idx=gate_idx),
        grid=(rows // tm, d // tn),
        in_specs=in_specs,
        out_specs=pl.BlockSpec((tm, tn), lambda i, j: (i, j)),
        out_shape=jax.ShapeDtypeStruct((rows, d), F32),
        compiler_params=_cparams(2),
        name=name,
    )(*a_list, *w_args, x, mod)


def _glu_kernel(x_ref, g_ref, mod_ref, wg_ref, wu_ref, o_ref, h_ref):
    @pl.when(pl.program_id(1) == 0)
    def _():
        m = mod_ref[0]
        h_ref[...] = _norm_mod(x_ref[...], g_ref[...], m[3:4], m[4:5]).astype(BF16)

    h = h_ref[...]
    gate = _dot(h, wg_ref[...].astype(BF16))
    up = _dot(h, wu_ref[...].astype(BF16))
    o_ref[...] = (_silu(gate) * up).astype(BF16)


def _glu(x, g, mod, wg, wu, dims, tm, tf):
    m_rows, d = x.shape
    f = wg.shape[1]
    seq_tiles = dims["seq"] // tm

    def sel(i):
        return jnp.minimum(i // seq_tiles, dims["batch"])

    return pl.pallas_call(
        _glu_kernel,
        grid=(m_rows // tm, f // tf),
        in_specs=[pl.BlockSpec((tm, d), lambda i, j: (i, 0)),
                  pl.BlockSpec((1, d), lambda i, j: (0, 0)),
                  pl.BlockSpec((1, 6, d), lambda i, j: (sel(i), 0, 0)),
                  pl.BlockSpec((d, tf), lambda i, j: (0, j)),
                  pl.BlockSpec((d, tf), lambda i, j: (0, j))],
        out_specs=pl.BlockSpec((tm, tf), lambda i, j: (i, j)),
        out_shape=jax.ShapeDtypeStruct((m_rows, f), BF16),
        scratch_shapes=[pltpu.VMEM((tm, d), BF16)],
        compiler_params=_cparams(2),
        name="ffn_glu",
    )(x, g.reshape(1, d), mod, wg, wu)


def _mla_proj_kernel(x_ref, g_ref, mod_ref, wd_ref, qg_ref, kvg_ref, wuq_ref, wukv_ref,
                     tq_ref, tk_ref, q_ref, kv_ref, kr_ref, *, n_lat_tiles, q_lora, kv_lora, heads):
    i = pl.program_id(0)
    m = mod_ref[0]
    h = _norm_mod(x_ref[...], g_ref[...], m[0:1], m[1:2]).astype(BF16)
    t = _dot(h, wd_ref[...])
    ckv = _rms(t[:, q_lora:q_lora + kv_lora], kvg_ref[...]).astype(BF16)
    kv_ref[...] = _dot(ckv, wukv_ref[...]).astype(BF16)
    kr = t[:, q_lora + kv_lora:]
    kr_ref[...] = _apply_rope(kr, tk_ref, MLA_ROPE // 4).astype(BF16)

    @pl.when(i < n_lat_tiles)
    def _():
        cq = _rms(t[:, :q_lora], qg_ref[...]).astype(BF16)
        q = _dot(cq, wuq_ref[...])
        scale = (MLA_NOPE + MLA_ROPE) ** -0.5 * LOG2E
        for hh in range(heads):
            c0 = hh * 2 * LANES
            q_ref[:, c0:c0 + LANES] = (q[:, c0:c0 + LANES] * scale).astype(BF16)
            q_ref[:, c0 + LANES:c0 + 2 * LANES] = _apply_rope(
                q[:, c0 + LANES:c0 + 2 * LANES], tq_ref, MLA_ROPE // 4).astype(BF16)


def _mla_proj(x, g, mod, wd, qg, kvg, wuq, wukv, tab_q, tab_k, dims, tm):
    m_rows, d = x.shape
    seq, lat = dims["seq"], dims["lat"]
    q_lora, kv_lora = qg.shape[0], kvg.shape[0]
    heads = wukv.shape[1] // (MLA_NOPE + MLA_V)
    lat_tiles, seq_tiles = lat // tm, seq // tm

    def sel(i):
        return jnp.minimum(i // seq_tiles, dims["batch"])

    def tab_map(i):
        return (0, jnp.where(i < lat_tiles, i % seq_tiles, seq_tiles), 0)

    kern = functools.partial(_mla_proj_kernel, n_lat_tiles=lat_tiles, q_lora=q_lora, kv_lora=kv_lora,
                             heads=heads)
    const = lambda i: (0, 0)
    return pl.pallas_call(
        kern,
        grid=(m_rows // tm,),
        in_specs=[pl.BlockSpec((tm, d), lambda i: (i, 0)),
                  pl.BlockSpec((1, d), const),
                  pl.BlockSpec((1, 6, d), lambda i: (sel(i), 0, 0)),
                  pl.BlockSpec(wd.shape, const),
                  pl.BlockSpec((1, q_lora), const),
                  pl.BlockSpec((1, kv_lora), const),
                  pl.BlockSpec(wuq.shape, const),
                  pl.BlockSpec(wukv.shape, const),
                  pl.BlockSpec((3, tm, LANES), tab_map),
                  pl.BlockSpec((3, tm, LANES), tab_map)],
        out_specs=[pl.BlockSpec((tm, wuq.shape[1]), lambda i: (jnp.minimum(i, lat_tiles - 1), 0)),
                   pl.BlockSpec((tm, wukv.shape[1]), lambda i: (i, 0)),
                   pl.BlockSpec((tm, LANES), lambda i: (i, 0))],
        out_shape=[jax.ShapeDtypeStruct((lat, wuq.shape[1]), BF16),
                   jax.ShapeDtypeStruct((m_rows, wukv.shape[1]), BF16),
                   jax.ShapeDtypeStruct((m_rows, LANES), BF16)],
        compiler_params=_cparams(1),
        name="od_mla_proj",
    )(x, g.reshape(1, d), mod, wd, qg.reshape(1, q_lora), kvg.reshape(1, kv_lora), wuq, wukv, tab_q, tab_k)


def _mla_attn_kernel(q_ref, knc_ref, knl_ref, krc_ref, krl_ref, vc_ref, vl_ref, o_ref,
                     k_scr, v_scr, s_scr, p_scr, *, n_ctx):
    @pl.when(pl.program_id(2) == 0)
    def _():
        k_scr[0:n_ctx, 0:LANES] = knc_ref[...]
        k_scr[n_ctx:, 0:LANES] = knl_ref[...]
        k_scr[0:n_ctx, LANES:] = krc_ref[...]
        k_scr[n_ctx:, LANES:] = krl_ref[...]
        v_scr[0:n_ctx, 0:LANES] = vc_ref[...]
        v_scr[n_ctx:, 0:LANES] = vl_ref[...]
        v_scr[:, LANES:] = jnp.ones((v_scr.shape[0], LANES), BF16)

    half = q_ref.shape[0] // 2
    halves = [slice(0, half), slice(half, 2 * half)]
    _softmax_numerators([q_ref[r] for r in halves], k_scr, [slice(None)] * 2, s_scr, p_scr, False)
    for i, r in enumerate(halves):
        acc = _dot(p_scr[i], v_scr[...])
        o_ref[r] = (acc[:, :LANES] / acc[:, LANES:LANES + 1]).astype(BF16)


def _mla_attn(q, kv, kr, dims, tq):
    batch, seq, ctx, lat = dims["batch"], dims["seq"], dims["ctx"], dims["lat"]
    heads = q.shape[1] // (2 * LANES)
    nq = seq // tq
    ctx_blk0 = lat // ctx
    return pl.pallas_call(
        functools.partial(_mla_attn_kernel, n_ctx=ctx),
        grid=(batch, heads, nq),
        in_specs=[pl.BlockSpec((tq, 2 * LANES), lambda b, h, i: (b * nq + i, h)),
                  pl.BlockSpec((ctx, LANES), lambda b, h, i: (ctx_blk0 + b, 2 * h)),
                  pl.BlockSpec((seq, LANES), lambda b, h, i: (b, 2 * h)),
                  pl.BlockSpec((ctx, LANES), lambda b, h, i: (ctx_blk0 + b, 0)),
                  pl.BlockSpec((seq, LANES), lambda b, h, i: (b, 0)),
                  pl.BlockSpec((ctx, LANES), lambda b, h, i: (ctx_blk0 + b, 2 * h + 1)),
                  pl.BlockSpec((seq, LANES), lambda b, h, i: (b, 2 * h + 1))],
        out_specs=pl.BlockSpec((tq, LANES), lambda b, h, i: (b * nq + i, h)),
        out_shape=jax.ShapeDtypeStruct((lat, heads * LANES), BF16),
        scratch_shapes=[pltpu.VMEM((ctx + seq, 2 * LANES), BF16), pltpu.VMEM((ctx + seq, 2 * LANES), BF16),
                        pltpu.VMEM((2, tq // 2, ctx + seq), F32), pltpu.VMEM((2, tq // 2, ctx + seq), BF16)],
        compiler_params=_cparams(3),
        name="od_mla_attn",
    )(q, kv, kv, kr, kr, kv, kv)


def _router_kernel(x_ref, g_ref, mod_ref, r_ref, h_ref, idx_ref, w_ref):
    m = mod_ref[0]
    h = _norm_mod(x_ref[...], g_ref[...], m[3:4], m[4:5])
    h_ref[...] = h
    r = r_ref[...]
    h_hi = h.astype(BF16)
    h_lo = (h - h_hi.astype(F32)).astype(BF16)
    r_hi = r.astype(BF16)
    r_lo = (r - r_hi.astype(F32)).astype(BF16)
    logits = _dot(h_hi, r_hi) + (_dot(h_lo, r_hi) + _dot(h_hi, r_lo))
    lane = lax.broadcasted_iota(jnp.int32, logits.shape, 1)
    lane_f = lane.astype(F32)
    neg = jnp.float32(-jnp.inf)
    logits = jnp.where(lane < N_EXPERTS, logits, neg)
    m1 = jnp.max(logits, axis=-1, keepdims=True)
    i1 = jnp.min(jnp.where(logits == m1, lane_f, float(LANES)), axis=-1, keepdims=True)
    rest = jnp.where(lane_f == i1, neg, logits)
    m2 = jnp.max(rest, axis=-1, keepdims=True)
    i2 = jnp.min(jnp.where(rest == m2, lane_f, float(LANES)), axis=-1, keepdims=True)
    e2 = jnp.exp(m2 - m1)
    w1 = 1.0 / (1.0 + e2)
    w2 = e2 / (1.0 + e2)
    idx_ref[...] = jnp.where(lane == 0, i1, jnp.where(lane == 1, i2, 0.0)).astype(jnp.int32)
    w_ref[...] = jnp.where(lane == 0, w1, jnp.where(lane == 1, w2, 0.0))


def _router(x, g, mod, router_pad, dims, tm):
    lat, d = x.shape
    seq_tiles = dims["seq"] // tm
    return pl.pallas_call(
        _router_kernel,
        grid=(lat // tm,),
        in_specs=[pl.BlockSpec((tm, d), lambda i: (i, 0)),
                  pl.BlockSpec((1, d), lambda i: (0, 0)),
                  pl.BlockSpec((1, 6, d), lambda i: (i // seq_tiles, 0, 0)),
                  pl.BlockSpec((d, LANES), lambda i: (0, 0))],
        out_specs=[pl.BlockSpec((tm, d), lambda i: (i, 0)),
                   pl.BlockSpec((tm, LANES), lambda i: (i, 0)),
                   pl.BlockSpec((tm, LANES), lambda i: (i, 0))],
        out_shape=[jax.ShapeDtypeStruct((lat, d), F32),
                   jax.ShapeDtypeStruct((lat, LANES), jnp.int32),
                   jax.ShapeDtypeStruct((lat, LANES), F32)],
        compiler_params=_cparams(1),
        name="moe_router",
    )(x, g.reshape(1, d), mod, router_pad)


def _moe_plan(top_idx, n_items):
    e_flat = top_idx.reshape(-1)
    n_assign = e_flat.shape[0]
    onehot = (e_flat[:, None] == jnp.arange(N_EXPERTS, dtype=jnp.int32)[None, :]).astype(jnp.int32)
    csum = jnp.cumsum(onehot, axis=0)
    counts = csum[-1]
    rank = jnp.sum(csum * onehot, axis=1) - 1
    blocks = (counts + MOE_BLOCK - 1) // MOE_BLOCK
    blk_end = jnp.cumsum(blocks)
    blk_start = blk_end - blocks
    total = blk_end[-1]
    dest = blk_start[e_flat] * MOE_BLOCK + rank
    row_tok = jnp.zeros((n_items * MOE_BLOCK,), jnp.int32).at[dest].set(
        jnp.arange(n_assign, dtype=jnp.int32) // TOP_K)
    p = jnp.arange(n_items, dtype=jnp.int32)
    pc = jnp.minimum(p, total - 1)
    item_e = jnp.sum((pc[:, None] >= blk_end[None, :]).astype(jnp.int32), axis=1)
    rows_left = counts[item_e] - (pc - blk_start[item_e]) * MOE_BLOCK
    nact = jnp.clip((rows_left + MOE_SUB - 1) // MOE_SUB, 0, MOE_BLOCK // MOE_SUB)
    nact = jnp.where(p < total, nact, 0).astype(jnp.int32)
    out_blk = jnp.where(p < total, p, n_items).astype(jnp.int32)
    items = (item_e.astype(jnp.int32), pc.astype(jnp.int32), out_blk, nact)
    return dest.astype(jnp.int32), row_tok, items, (blocks, blk_start, blk_end, total)


def _moe_steps(items, runs, n_items, nj):
    _, _, _, nact = items
    blocks, blk_start, blk_end, total = runs
    s = jnp.arange(n_items * nj, dtype=jnp.int32)
    live = s < total * nj
    sc = jnp.minimum(s, total * nj - 1)
    e = jnp.sum((sc[:, None] >= (blk_end * nj)[None, :]).astype(jnp.int32), axis=1)
    t = sc - blk_start[e] * nj
    j = t // blocks[e]
    r = t % blocks[e]
    blk = blk_start[e] + jnp.where(j % 2 == 0, r, blocks[e] - 1 - r)
    out_blk = jnp.where(live, blk, n_items)
    out_j = jnp.where(live, j, 0)
    step_nact = jnp.where(live, nact[blk], 0)
    return tuple(a.astype(jnp.int32) for a in (e, blk, j, out_blk, out_j, step_nact))


def _gather_kernel(tok_ref, nact_ref, h_ref, o_ref, buf, sem):
    p = pl.program_id(0)
    nact = nact_ref[p]
    base = p * MOE_BLOCK

    def copy(r, t):
        return pltpu.make_async_copy(h_ref.at[pl.ds(t, 1)], buf.at[pl.ds(r, 1)], sem)

    def start(grp, c):
        for u in range(GATHER_UNROLL):
            r = grp * GATHER_UNROLL + u
            copy(r, tok_ref[base + r]).start()
        return c

    def wait(grp, c):
        for u in range(GATHER_UNROLL):
            copy(grp * GATHER_UNROLL + u, 0).wait()
        return c

    n_grp = nact * (MOE_SUB // GATHER_UNROLL)
    lax.fori_loop(0, n_grp, start, 0)
    lax.fori_loop(0, n_grp, wait, 0)
    for s in range(MOE_BLOCK // MOE_SUB):
        rows = pl.ds(s * MOE_SUB, MOE_SUB)

        @pl.when(s < nact)
        def _():
            o_ref[rows] = buf[rows].astype(BF16)

        @pl.when(s >= nact)
        def _():
            o_ref[rows] = jnp.zeros((MOE_SUB, o_ref.shape[1]), BF16)


def _moe_gather(h, row_tok, nact, n_items):
    d = h.shape[1]
    return pl.pallas_call(
        _gather_kernel,
        grid_spec=pltpu.PrefetchScalarGridSpec(
            num_scalar_prefetch=2,
            grid=(n_items,),
            in_specs=[pl.BlockSpec(memory_space=pl.ANY)],
            out_specs=pl.BlockSpec((MOE_BLOCK, d), lambda p, tok, na: (p, 0)),
            scratch_shapes=[pltpu.VMEM((MOE_BLOCK, d), F32), pltpu.SemaphoreType.DMA(())]),
        out_shape=jax.ShapeDtypeStruct((n_items * MOE_BLOCK, d), BF16),
        compiler_params=_cparams(1),
        name="moe_gather",
    )(row_tok, nact, h)


def _for_active_rows(nact, in_ref, o_ref, fn):
    n_sub = MOE_BLOCK // MOE_SUB
    for k in range(n_sub + 1):
        @pl.when(nact == k)
        def _():
            if k > 0:
                o_ref[0:k * MOE_SUB] = fn(in_ref[0:k * MOE_SUB])
            if k < n_sub:
                o_ref[k * MOE_SUB:] = jnp.zeros((MOE_BLOCK - k * MOE_SUB, o_ref.shape[1]), o_ref.dtype)


def _moe_glu_kernel(e_ref, blk_ref, j_ref, oblk_ref, oj_ref, nact_ref, h_ref, wg_ref, wu_ref, o_ref):
    nact = nact_ref[pl.program_id(0)]

    def run(h):
        gate = _dot(h, wg_ref[0].astype(BF16))
        return (_silu(gate) * _dot(h, wu_ref[0].astype(BF16))).astype(BF16)

    _for_active_rows(nact, h_ref, o_ref, run)


def _moe_glu(hs, wg, wu, steps, n_items, tf):
    d, f = wg.shape[1], wg.shape[2]
    nj = f // tf

    def w_map(s, e, b, j, ob, oj, na):
        return (e[s], 0, j[s])

    return pl.pallas_call(
        _moe_glu_kernel,
        grid_spec=pltpu.PrefetchScalarGridSpec(
            num_scalar_prefetch=6,
            grid=(n_items * nj,),
            in_specs=[pl.BlockSpec((MOE_BLOCK, d), lambda s, e, b, j, ob, oj, na: (b[s], 0)),
                      pl.BlockSpec((1, d, tf), w_map),
                      pl.BlockSpec((1, d, tf), w_map)],
            out_specs=pl.BlockSpec((MOE_BLOCK, tf), lambda s, e, b, j, ob, oj, na: (ob[s], oj[s]))),
        out_shape=jax.ShapeDtypeStruct(((n_items + 1) * MOE_BLOCK, f), BF16),
        compiler_params=_cparams(1),
        name="moe_glu",
    )(*steps, hs, wg, wu)


def _moe_down_kernel(e_ref, blk_ref, oblk_ref, nact_ref, a_ref, wd_ref, o_ref):
    nact = nact_ref[pl.program_id(0)]
    _for_active_rows(nact, a_ref, o_ref, lambda a: _dot(a, wd_ref[0].astype(BF16)))


def _moe_down(a, wd, plan, n_items, tn):
    item_e, in_blk, out_blk, nact = plan
    f, d = wd.shape[1], wd.shape[2]
    nj = d // tn

    def w_map(p, j, e, b, ob, na):
        return (e[p], 0, jnp.where(na[p] > 0, j, nj - 1))

    def o_map(p, j, e, b, ob, na):
        return (ob[p], jnp.where(na[p] > 0, j, 0))

    return pl.pallas_call(
        _moe_down_kernel,
        grid_spec=pltpu.PrefetchScalarGridSpec(
            num_scalar_prefetch=4,
            grid=(n_items, nj),
            in_specs=[pl.BlockSpec((MOE_BLOCK, f), lambda p, j, e, b, ob, na: (b[p], 0)),
                      pl.BlockSpec((1, f, tn), w_map)],
            out_specs=pl.BlockSpec((MOE_BLOCK, tn), o_map)),
        out_shape=jax.ShapeDtypeStruct(((n_items + 1) * MOE_BLOCK, d), F32),
        compiler_params=_cparams(2),
        name="moe_down",
    )(item_e, in_blk, out_blk, nact, a, wd)


def _combine_kernel(dest_ref, x_ref, mod_ref, w_ref, fg_ref, y_ref, o_ref, buf, sem, *, tm):
    i = pl.program_id(0)

    def copy(r, k, row):
        return pltpu.make_async_copy(y_ref.at[pl.ds(row, 1)], buf.at[k, pl.ds(r, 1)], sem)

    def start(grp, c):
        for u in range(GATHER_UNROLL):
            r = grp * GATHER_UNROLL + u
            for k in range(TOP_K):
                copy(r, k, dest_ref[(i * tm + r) * TOP_K + k]).start()
        return c

    def wait(grp, c):
        for u in range(GATHER_UNROLL):
            for k in range(TOP_K):
                copy(grp * GATHER_UNROLL + u, k, 0).wait()
        return c

    lax.fori_loop(0, tm // GATHER_UNROLL, start, 0)
    lax.fori_loop(0, tm // GATHER_UNROLL, wait, 0)
    w = w_ref[...]
    moe = w[:, 0:1] * buf[0] + w[:, 1:2] * buf[1]
    m = mod_ref[0]
    o_ref[...] = _rms(x_ref[...] + m[5:6] * moe, fg_ref[...])


def _moe_combine(dest, x, mod, top_w, final_g, ys, dims, tm):
    lat, d = x.shape
    seq_tiles = dims["seq"] // tm
    return pl.pallas_call(
        functools.partial(_combine_kernel, tm=tm),
        grid_spec=pltpu.PrefetchScalarGridSpec(
            num_scalar_prefetch=1,
            grid=(lat // tm,),
            in_specs=[pl.BlockSpec((tm, d), lambda i, dr: (i, 0)),
                      pl.BlockSpec((1, 6, d), lambda i, dr: (i // seq_tiles, 0, 0)),
                      pl.BlockSpec((tm, LANES), lambda i, dr: (i, 0)),
                      pl.BlockSpec((1, d), lambda i, dr: (0, 0)),
                      pl.BlockSpec(memory_space=pl.ANY)],
            out_specs=pl.BlockSpec((tm, d), lambda i, dr: (i, 0)),
            scratch_shapes=[pltpu.VMEM((TOP_K, tm, d), F32), pltpu.SemaphoreType.DMA(())]),
        out_shape=jax.ShapeDtypeStruct((lat, d), F32),
        compiler_params=_cparams(1),
        name="moe_combine",
    )(dest, x, mod, top_w, final_g.reshape(1, d), ys)


def kernel(x, c, ctx, c_ctx, ada_w, ada_b, norm1_g, norm2_g, ev_w_in, ev_w_out, ev_lambda, ev_subln_g, od_w_dq, od_q_norm_g, od_w_uq, od_w_dkv, od_kv_norm_g, od_w_ukv, od_w_o, ffn_w_gate, ffn_w_up, ffn_w_down, moe_router, moe_w_gate, moe_w_up, moe_w_down, final_norm_g):
    batch, seq, d = x.shape
    n_ctx = ctx.shape[1]
    depth = ada_w.shape[0]
    assert depth == 2 and batch < MOD_ROWS and seq % GRID_W == 0
    lat = batch * seq
    dims = dict(batch=batch, seq=seq, ctx=n_ctx, lat=lat)
    tm = 1024
    assert seq % tm == 0 and (batch * n_ctx) % tm == 0

    cond = jnp.concatenate([c, c_ctx[None, :], jnp.zeros((MOD_ROWS - batch - 1, d), F32)], axis=0)
    mod = _ada(cond, ada_w, ada_b).reshape(depth, MOD_ROWS, 6, d)
    xs = jnp.concatenate([x.reshape(lat, d), ctx.reshape(batch * n_ctx, d)], axis=0)

    lam_init = 0.8 - 0.6 * math.exp(-0.3 * 0)
    fw = ev_w_in.shape[2] // 4
    tabs_ev = jnp.asarray(np.stack([
        _rope_tables(seq, tm, DIFF_HEAD_DIM // 4, DIFF_HEAD_DIM ** -0.5 * LOG2E),
        _rope_tables(seq, tm, DIFF_HEAD_DIM // 4, 1.0)]))
    qkvf = _inproj(xs, norm1_g[0], mod[0], ev_w_in[0].astype(BF16), tabs_ev, dims, tm, 1024)
    o_attn = _diff_attn(qkvf, None, ev_lambda[0], ev_subln_g[0], lam_init, dims, 512, True)
    o_attn = _diff_attn(qkvf, o_attn, ev_lambda[0], ev_subln_g[0], lam_init, dims, n_ctx, False)

    gw = fw // FOURIER_GROUPS
    cc_np, sc_np = _dft_cos_sin(gw)
    cc = jnp.asarray(cc_np.astype(np.float32)).astype(BF16)
    sc = jnp.asarray(sc_np.astype(np.float32)).astype(BF16)
    cn_np, sn_np = _dft_cos_sin(seq)
    cs_lat = jnp.asarray(np.concatenate([cn_np, -sn_np], axis=1).astype(np.float32)).astype(BF16)
    cx_np, sx_np = _dft_cos_sin(n_ctx)
    cs_ctx = jnp.asarray(np.concatenate([cx_np, -sx_np], axis=1).astype(np.float32)).astype(BF16)
    fm = _fourier(qkvf, None, seq, 0, batch, fw, cc, sc, cs_lat, "ev_fourier_lat")
    fm = _fourier(qkvf, fm, n_ctx, lat // n_ctx, batch, fw, cc, sc, cs_ctx, "ev_fourier_ctx")

    m_rows = xs.shape[0]
    xs = _mm_res([fm, o_attn], ev_w_out[0].astype(BF16), xs, mod[0], 2, dims, m_rows, tm, 1024, "ev_outproj")
    act = _glu(xs, norm2_g[0], mod[0], ffn_w_gate[0].astype(BF16), ffn_w_up[0].astype(BF16), dims, tm, 512)
    xs = _mm_res([act], ffn_w_down[0].astype(BF16), xs, mod[0], 5, dims, m_rows, tm, 512, "ffn_down")

    heads = od_w_ukv.shape[2] // (MLA_NOPE + MLA_V)
    q_lora = od_w_dq.shape[2]
    wd_cat = jnp.concatenate(
        [od_w_dq[0], od_w_dkv[0], jnp.zeros((d, LANES - MLA_ROPE), F32)], axis=1).astype(BF16)
    wuq = jnp.pad(od_w_uq[0].reshape(q_lora, heads, MLA_NOPE + MLA_ROPE),
                  ((0, 0), (0, 0), (0, 2 * LANES - MLA_NOPE - MLA_ROPE))).reshape(q_lora, heads * 2 * LANES)
    tm_mla = 256
    tab_q = jnp.asarray(_rope_tables(seq, tm_mla, MLA_ROPE // 4, (MLA_NOPE + MLA_ROPE) ** -0.5 * LOG2E))
    tab_k = jnp.asarray(_rope_tables(seq, tm_mla, MLA_ROPE // 4, 1.0))
    q, kv, kr = _mla_proj(xs, norm1_g[1], mod[1], wd_cat, od_q_norm_g[0], od_kv_norm_g[0],
                          wuq.astype(BF16), od_w_ukv[0].astype(BF16), tab_q, tab_k, dims, tm_mla)
    o_mla = _mla_attn(q, kv, kr, dims, 1024)
    xl = _mm_res([o_mla], od_w_o[0].astype(BF16), xs, mod[1], 2, dims, lat, tm, 1024, "od_outproj")

    router_pad = jnp.pad(moe_router[0], ((0, 0), (0, LANES - N_EXPERTS)))
    h2, top_idx, top_w = _router(xl, norm2_g[1], mod[1], router_pad, dims, 512)
    n_items = lat * TOP_K // MOE_BLOCK + N_EXPERTS
    dest, row_tok, items, runs = _moe_plan(top_idx[:, :TOP_K], n_items)
    tf = 512
    steps = _moe_steps(items, runs, n_items, moe_w_gate.shape[3] // tf)
    hs = _moe_gather(h2, row_tok, items[3], n_items)
    act = _moe_glu(hs, moe_w_gate[0], moe_w_up[0], steps, n_items, tf)
    ys = _moe_down(act, moe_w_down[0], items, n_items, 256)
    out = _moe_combine(dest, xl, mod[1], top_w, final_norm_g, ys, dims, 256)
    return out.reshape(batch, seq, d)
```

```python
import functools
import math

import numpy as np
import jax
import jax.numpy as jnp
from jax import lax
from jax.experimental import pallas as pl
from jax.experimental.pallas import tpu as pltpu

F32 = jnp.float32
BF16 = jnp.bfloat16

GRID_W = 64
NORM_EPS = 1e-6
ROPE_BASE = 10000.0
FOURIER_GROUPS = 4
DIFF_HEAD_DIM = 128
MLA_NOPE = 128
MLA_ROPE = 64
MLA_V = 128
N_EXPERTS = 8
TOP_K = 2

LANES = 128
MOD_ROWS = 8
VMEM_LIMIT = 56 * 1024 * 1024
MOE_BLOCK = 1024
MOE_SUB = 256
GATHER_UNROLL = 8
KEY_CHUNK = 256
LOG2E = math.log2(math.e)


def _cparams(n_axes):
    return pltpu.CompilerParams(dimension_semantics=("arbitrary",) * n_axes,
                                vmem_limit_bytes=VMEM_LIMIT)


def _rms(x, g):
    return x * lax.rsqrt(jnp.mean(x * x, axis=-1, keepdims=True) + NORM_EPS) * g


def _norm_mod(x, g, shift, scale):
    return _rms(x, g) * (1.0 + scale) + shift


def _silu(x):
    return x * (1.0 / (1.0 + jnp.exp(-x)))


def _dot(a, b):
    return jnp.dot(a, b, preferred_element_type=F32)


def _dot_nt(a, b):
    return lax.dot_general(a, b, (((1,), (1,)), ((), ())), preferred_element_type=F32)


def _rope_tables(seq, extra_rows, chunk, scale):
    n = np.arange(seq)
    row, col = n // GRID_W, n % GRID_W
    lane = np.arange(LANES)
    a = 2 * chunk
    inv = ROPE_BASE ** (-np.arange(0, a, 2, dtype=np.float64) / a)
    used = lane < 4 * chunk
    freq = inv[lane % chunk]
    pos = np.where(lane[None, :] < 2 * chunk, row[:, None], col[:, None]).astype(np.float64)
    ang = pos * freq[None, :]
    first = (lane // chunk) % 2 == 0
    cos = np.where(used[None, :], np.cos(ang), 0.0)
    sin = np.where(used[None, :], np.sin(ang), 0.0)
    s1 = np.where(first[None, :], -sin, 0.0)
    s2 = np.where(first[None, :], 0.0, sin)
    ident = np.zeros((3, extra_rows, LANES))
    ident[0] = used[None, :].astype(np.float64)
    tab = np.concatenate([np.stack([cos, s1, s2]), ident], axis=1) * scale
    return tab.astype(np.float32)


def _apply_rope(x, tab_ref, chunk):
    return (x * tab_ref[0] + pltpu.roll(x, LANES - chunk, 1) * tab_ref[1]
            + pltpu.roll(x, chunk, 1) * tab_ref[2])


def _dft_cos_sin(n):
    k = np.arange(n)
    ang = 2.0 * np.pi * ((k[:, None] * k[None, :]) % n) / n
    return np.cos(ang), np.sin(ang)


def _ada_kernel(s_ref, w_ref, b_ref, o_ref):
    s = _silu(s_ref[...]).astype(BF16)
    o_ref[0] = _dot(s, w_ref[0].astype(BF16)) + b_ref[0]


def _ada(cond, ada_w, ada_b):
    depth, d, n = ada_w.shape
    tn = 1024
    return pl.pallas_call(
        _ada_kernel,
        grid=(depth, n // tn),
        in_specs=[pl.BlockSpec((MOD_ROWS, d), lambda i, j: (0, 0)),
                  pl.BlockSpec((1, d, tn), lambda i, j: (i, 0, j)),
                  pl.BlockSpec((1, 1, tn), lambda i, j: (i, 0, j))],
        out_specs=pl.BlockSpec((1, MOD_ROWS, tn), lambda i, j: (i, 0, j)),
        out_shape=jax.ShapeDtypeStruct((depth, MOD_ROWS, n), F32),
        compiler_params=_cparams(2),
        name="ada",
    )(cond, ada_w, ada_b.reshape(depth, 1, n))


def _inproj_kernel(x_ref, g_ref, mod_ref, w_ref, tab_ref, *rest):
    o_ref, h_ref = rest[-2:]
    j = pl.program_id(1)
    quarter = pl.num_programs(1) // 4

    @pl.when(j == 0)
    def _():
        m = mod_ref[0]
        h_ref[...] = _norm_mod(x_ref[...], g_ref[...], m[0:1], m[1:2]).astype(BF16)

    is_rope = jnp.logical_and(j >= quarter, j < 3 * quarter)

    @pl.when(is_rope)
    def _():
        res = _dot(h_ref[...], w_ref[...].astype(BF16))
        for c in range(res.shape[1] // LANES):
            sl = slice(c * LANES, (c + 1) * LANES)
            o_ref[:, sl] = _apply_rope(res[:, sl], tab_ref.at[0], DIFF_HEAD_DIM // 4).astype(BF16)

    @pl.when(jnp.logical_not(is_rope))
    def _():
        o_ref[...] = _dot(h_ref[...], w_ref[...].astype(BF16)).astype(BF16)


def _inproj(x, prev, g, mod, w, tabs, dims, tm, tn):
    rows, d = x.shape
    seq, lat, batch = dims["seq"], dims["lat"], dims["batch"]
    m_rows = lat + batch * dims["ctx"]
    n = w.shape[1]
    nj = n // tn
    assert nj % 4 == 0
    lat_tiles, seq_tiles = lat // tm, seq // tm
    is_ctx = prev is not None
    row_blk0 = lat_tiles if is_ctx else 0

    def tab_map(i, j):
        return (jnp.where(j >= nj // 2, 1, 0), 0, seq_tiles if is_ctx else i % seq_tiles, 0)

    in_specs = [pl.BlockSpec((tm, d), lambda i, j: (i, 0)),
                pl.BlockSpec((1, d), lambda i, j: (0, 0)),
                pl.BlockSpec((1, 6, d), lambda i, j: (batch if is_ctx else i // seq_tiles, 0, 0)),
                pl.BlockSpec((d, tn), lambda i, j: (0, j)),
                pl.BlockSpec((1, 3, tm, LANES), tab_map)]
    args = [x, g.reshape(1, d), mod, w, tabs]
    aliases = {}
    if is_ctx:
        in_specs.append(pl.BlockSpec(memory_space=pl.ANY))
        args.append(prev)
        aliases = {len(args) - 1: 0}
    return pl.pallas_call(
        _inproj_kernel,
        grid=(rows // tm, nj),
        in_specs=in_specs,
        out_specs=pl.BlockSpec((tm, tn), lambda i, j: (row_blk0 + i, j)),
        out_shape=jax.ShapeDtypeStruct((m_rows, n), BF16),
        scratch_shapes=[pltpu.VMEM((tm, d), BF16)],
        input_output_aliases=aliases,
        compiler_params=_cparams(2),
        name="ev_inproj_ctx" if is_ctx else "ev_inproj_lat",
    )(*args)


def _softmax_numerators(qs, k_ref, kcols, s_scr, p_scr, want_sum):
    n_keys = k_ref.shape[0]
    chunks = [slice(c, c + KEY_CHUNK) for c in range(0, n_keys, KEY_CHUNK)]
    maxes = []
    for i, (q, cols) in enumerate(zip(qs, kcols)):
        m = None
        for ks in chunks:
            s = _dot_nt(q, k_ref[ks, cols])
            s_scr[i, :, ks] = s
            mc = jnp.max(s, axis=-1, keepdims=True)
            m = mc if m is None else jnp.maximum(m, mc)
        maxes.append(m)
    totals = []
    for i, m in enumerate(maxes):
        total = None
        for ks in chunks:
            e = jnp.exp2(s_scr[i, :, ks] - m)
            if want_sum:
                part = jnp.sum(e, axis=-1, keepdims=True)
                total = part if total is None else total + part
            p_scr[i, :, ks] = e.astype(BF16)
        totals.append(total)
    return totals


def _diff_attn_kernel(lam_ref, q_ref, *rest, lam_init, seg_rows):
    n_seg = len(seg_rows)
    k_refs, v_refs, g_ref = rest[:n_seg], rest[n_seg:2 * n_seg], rest[2 * n_seg]
    o_ref, k_scr, v_scr, s_scr, p_scr = rest[-5:]
    hd = DIFF_HEAD_DIM

    @pl.when(pl.program_id(2) == 0)
    def _():
        r0 = 0
        for k_ref, v_ref, n in zip(k_refs, v_refs, seg_rows):
            k_scr[r0:r0 + n] = k_ref[...]
            v_scr[r0:r0 + n] = v_ref[...]
            r0 += n

    lv = lam_ref[...]
    lam = (jnp.exp(jnp.sum(lv[0:1] * lv[1:2], axis=-1, keepdims=True))
           - jnp.exp(jnp.sum(lv[2:3] * lv[3:4], axis=-1, keepdims=True)) + lam_init)
    q = q_ref[...]
    cols = [slice(c * hd, (c + 1) * hd) for c in range(2)]
    totals = _softmax_numerators([q[:, c] for c in cols], k_scr, cols, s_scr, p_scr, True)
    outs = [_dot(p_scr[c], v_scr[...]) * (1.0 / totals[c]) for c in range(2)]
    o = outs[0] - lam * outs[1]
    o_ref[...] = (_rms(o, g_ref[...]) * (1.0 - lam_init)).astype(BF16)


def _diff_attn(qkvf, prev, lam_vec, subln_g, lam_init, dims, tq, latent):
    m_rows = qkvf.shape[0]
    batch, seq, ctx = dims["batch"], dims["seq"], dims["ctx"]
    hw = 2 * DIFF_HEAD_DIM
    width = qkvf.shape[1] // 4
    heads = width // hw
    ctx_blk0 = batch * seq // ctx
    qcol, kcol, vcol = width // hw, 2 * width // hw, 3 * width // hw
    if latent:
        nq, q_blk0, seg_rows = seq // tq, 0, (ctx, seq)
    else:
        assert tq == ctx
        nq, q_blk0, seg_rows = 1, ctx_blk0, (ctx,)
    n_keys = sum(seg_rows)

    def kv_specs(col):
        specs = [pl.BlockSpec((ctx, hw), lambda b, h, i: (ctx_blk0 + b, col + h))]
        if latent:
            specs.append(pl.BlockSpec((seq, hw), lambda b, h, i: (b, col + h)))
        return specs

    in_specs = ([pl.BlockSpec((4, DIFF_HEAD_DIM), lambda b, h, i: (0, 0)),
                 pl.BlockSpec((tq, hw), lambda b, h, i: (q_blk0 + b * nq + i, qcol + h))]
                + kv_specs(kcol) + kv_specs(vcol)
                + [pl.BlockSpec((1, hw), lambda b, h, i: (0, 0))])
    args = [lam_vec, qkvf] + [qkvf] * (2 * len(seg_rows)) + [subln_g.reshape(1, hw)]
    aliases = {}
    if prev is not None:
        in_specs.append(pl.BlockSpec(memory_space=pl.ANY))
        args.append(prev)
        aliases = {len(args) - 1: 0}
    kern = functools.partial(_diff_attn_kernel, lam_init=lam_init, seg_rows=seg_rows)
    return pl.pallas_call(
        kern,
        grid=(batch, heads, nq),
        in_specs=in_specs,
        out_specs=pl.BlockSpec((tq, hw), lambda b, h, i: (q_blk0 + b * nq + i, h)),
        out_shape=jax.ShapeDtypeStruct((m_rows, width), BF16),
        scratch_shapes=[pltpu.VMEM((n_keys, hw), BF16), pltpu.VMEM((n_keys, hw), BF16),
                        pltpu.VMEM((2, tq, n_keys), F32), pltpu.VMEM((2, tq, n_keys), BF16)],
        input_output_aliases=aliases,
        compiler_params=_cparams(3),
        name="ev_diff_attn_lat" if latent else "ev_diff_attn_ctx",
    )(*args)


def _fourier_kernel(u_ref, cc_ref, sc_ref, cs_ref, *rest, n, norm):
    o_ref, ab_ref = rest[-2], rest[-1]
    u = u_ref[...]
    ab_ref[0:n] = _dot(u, cc_ref[...]).astype(BF16)
    ab_ref[n:] = _dot(u, sc_ref[...]).astype(BF16)
    o_ref[...] = (_dot(cs_ref[...], ab_ref[...]) * norm).astype(BF16)


def _fourier(qkvf, prev, n, row_blk0, batch, width, cc, sc, cs, name):
    m_rows = qkvf.shape[0]
    gw = width // FOURIER_GROUPS
    kern = functools.partial(_fourier_kernel, n=n, norm=1.0 / math.sqrt(n * gw))
    in_specs = [pl.BlockSpec((n, gw), lambda b, g: (row_blk0 + b, g)),
                pl.BlockSpec((gw, gw), lambda b, g: (0, 0)),
                pl.BlockSpec((gw, gw), lambda b, g: (0, 0)),
                pl.BlockSpec((n, 2 * n), lambda b, g: (0, 0), pipeline_mode=pl.Buffered(1))]
    args = [qkvf, cc, sc, cs]
    aliases = {}
    if prev is not None:
        in_specs.append(pl.BlockSpec(memory_space=pl.ANY))
        args.append(prev)
        aliases = {4: 0}
    return pl.pallas_call(
        kern,
        grid=(batch, FOURIER_GROUPS),
        in_specs=in_specs,
        out_specs=pl.BlockSpec((n, gw), lambda b, g: (row_blk0 + b, g)),
        out_shape=jax.ShapeDtypeStruct((m_rows, width), BF16),
        scratch_shapes=[pltpu.VMEM((2 * n, gw), BF16)],
        input_output_aliases=aliases,
        compiler_params=_cparams(2),
        name=name,
    )(*args)


def _mm_res_kernel(*refs, n_a, gate_idx):
    a_refs, w_refs = refs[:n_a], refs[n_a:2 * n_a]
    x_ref, mod_ref, o_ref = refs[2 * n_a], refs[2 * n_a + 1], refs[-1]
    acc = _dot(a_refs[0][...], w_refs[0][...].astype(BF16))
    for a_ref, w_ref in zip(a_refs[1:], w_refs[1:]):
        acc = acc + _dot(a_ref[...], w_ref[...].astype(BF16))
    m = mod_ref[0]
    o_ref[...] = x_ref[...] + m[gate_idx:gate_idx + 1] * acc


def _mm_res(a_list, w, x, mod, gate_idx, dims, rows, tm, tn, name, out_rows=None, prev=None):
    d = w.shape[1]
    seq_tiles = dims["seq"] // tm
    n_a = len(a_list)
    ctx_only = prev is not None
    row_blk0 = dims["lat"] // tm if ctx_only else 0
    out_rows = prev.shape[0] if ctx_only else (out_rows or rows)

    def sel(i):
        return dims["batch"] if ctx_only else jnp.minimum(i // seq_tiles, dims["batch"])

    in_specs, w_args, k0 = [], [], 0
    for a in a_list:
        in_specs.append(pl.BlockSpec((tm, a.shape[1]), lambda i, j: (row_blk0 + i, 0)))
    for a in a_list:
        ka = a.shape[1]
        assert k0 % ka == 0
        in_specs.append(pl.BlockSpec((ka, tn), lambda i, j, kb=k0 // ka: (kb, j)))
        w_args.append(w)
        k0 += ka
    assert k0 == w.shape[0]
    in_specs += [pl.BlockSpec((tm, tn), lambda i, j: (i, j)),
                 pl.BlockSpec((1, 6, tn), lambda i, j: (sel(i), 0, j))]
    args = [*a_list, *w_args, x, mod]
    aliases = {}
    if prev is not None:
        in_specs.append(pl.BlockSpec(memory_space=pl.ANY))
        args.append(prev)
        aliases = {len(args) - 1: 0}
    return pl.pallas_call(
        functools.partial(_mm_res_kernel, n_a=n_a, gate_idx=gate_idx),
        grid=(rows // tm, d // tn),
        in_specs=in_specs,
        out_specs=pl.BlockSpec((tm, tn), lambda i, j: (row_blk0 + i, j)),
        out_shape=jax.ShapeDtypeStruct((out_rows, d), F32),
        input_output_aliases=aliases,
        compiler_params=_cparams(2),
        name=name,
    )(*args)


def _glu_kernel(x_ref, g_ref, mod_ref, wg_ref, wu_ref, o_ref, h_ref):
    @pl.when(pl.program_id(1) == 0)
    def _():
        m = mod_ref[0]
        h_ref[...] = _norm_mod(x_ref[...], g_ref[...], m[3:4], m[4:5]).astype(BF16)

    h = h_ref[...]
    gate = _dot(h, wg_ref[...].astype(BF16))
    up = _dot(h, wu_ref[...].astype(BF16))
    o_ref[...] = (_silu(gate) * up).astype(BF16)


def _glu(x, g, mod, wg, wu, dims, tm, tf):
    m_rows, d = x.shape
    f = wg.shape[1]
    seq_tiles = dims["seq"] // tm

    def sel(i):
        return jnp.minimum(i // seq_tiles, dims["batch"])

    return pl.pallas_call(
        _glu_kernel,
        grid=(m_rows // tm, f // tf),
        in_specs=[pl.BlockSpec((tm, d), lambda i, j: (i, 0)),
                  pl.BlockSpec((1, d), lambda i, j: (0, 0)),
                  pl.BlockSpec((1, 6, d), lambda i, j: (sel(i), 0, 0)),
                  pl.BlockSpec((d, tf), lambda i, j: (0, j)),
                  pl.BlockSpec((d, tf), lambda i, j: (0, j))],
        out_specs=pl.BlockSpec((tm, tf), lambda i, j: (i, j)),
        out_shape=jax.ShapeDtypeStruct((m_rows, f), BF16),
        scratch_shapes=[pltpu.VMEM((tm, d), BF16)],
        compiler_params=_cparams(2),
        name="ffn_glu",
    )(x, g.reshape(1, d), mod, wg, wu)


def _mla_proj_kernel(x_ref, g_ref, mod_ref, wd_ref, qg_ref, kvg_ref, wuq_ref, wukv_ref,
                     tq_ref, tk_ref, q_ref, kv_ref, kr_ref, *, n_lat_tiles, q_lora, kv_lora, heads):
    i = pl.program_id(0)
    m = mod_ref[0]
    h = _norm_mod(x_ref[...], g_ref[...], m[0:1], m[1:2]).astype(BF16)
    t = _dot(h, wd_ref[...])
    ckv = _rms(t[:, q_lora:q_lora + kv_lora], kvg_ref[...]).astype(BF16)
    kv_ref[...] = _dot(ckv, wukv_ref[...]).astype(BF16)
    kr = t[:, q_lora + kv_lora:]
    kr_ref[...] = _apply_rope(kr, tk_ref, MLA_ROPE // 4).astype(BF16)

    @pl.when(i < n_lat_tiles)
    def _():
        cq = _rms(t[:, :q_lora], qg_ref[...]).astype(BF16)
        q = _dot(cq, wuq_ref[...])
        scale = (MLA_NOPE + MLA_ROPE) ** -0.5 * LOG2E
        for hh in range(heads):
            c0 = hh * 2 * LANES
            q_ref[:, c0:c0 + LANES] = (q[:, c0:c0 + LANES] * scale).astype(BF16)
            q_ref[:, c0 + LANES:c0 + 2 * LANES] = _apply_rope(
                q[:, c0 + LANES:c0 + 2 * LANES], tq_ref, MLA_ROPE // 4).astype(BF16)


def _mla_proj(x, g, mod, wd, qg, kvg, wuq, wukv, tab_q, tab_k, dims, tm):
    m_rows, d = x.shape
    seq, lat = dims["seq"], dims["lat"]
    q_lora, kv_lora = qg.shape[0], kvg.shape[0]
    heads = wukv.shape[1] // (MLA_NOPE + MLA_V)
    lat_tiles, seq_tiles = lat // tm, seq // tm

    def sel(i):
        return jnp.minimum(i // seq_tiles, dims["batch"])

    def tab_map(i):
        return (0, jnp.where(i < lat_tiles, i % seq_tiles, seq_tiles), 0)

    kern = functools.partial(_mla_proj_kernel, n_lat_tiles=lat_tiles, q_lora=q_lora, kv_lora=kv_lora,
                             heads=heads)
    const = lambda i: (0, 0)
    return pl.pallas_call(
        kern,
        grid=(m_rows // tm,),
        in_specs=[pl.BlockSpec((tm, d), lambda i: (i, 0)),
                  pl.BlockSpec((1, d), const),
                  pl.BlockSpec((1, 6, d), lambda i: (sel(i), 0, 0)),
                  pl.BlockSpec(wd.shape, const),
                  pl.BlockSpec((1, q_lora), const),
                  pl.BlockSpec((1, kv_lora), const),
                  pl.BlockSpec(wuq.shape, const),
                  pl.BlockSpec(wukv.shape, const),
                  pl.BlockSpec((3, tm, LANES), tab_map),
                  pl.BlockSpec((3, tm, LANES), tab_map)],
        out_specs=[pl.BlockSpec((tm, wuq.shape[1]), lambda i: (jnp.minimum(i, lat_tiles - 1), 0)),
                   pl.BlockSpec((tm, wukv.shape[1]), lambda i: (i, 0)),
                   pl.BlockSpec((tm, LANES), lambda i: (i, 0))],
        out_shape=[jax.ShapeDtypeStruct((lat, wuq.shape[1]), BF16),
                   jax.ShapeDtypeStruct((m_rows, wukv.shape[1]), BF16),
                   jax.ShapeDtypeStruct((m_rows, LANES), BF16)],
        compiler_params=_cparams(1),
        name="od_mla_proj",
    )(x, g.reshape(1, d), mod, wd, qg.reshape(1, q_lora), kvg.reshape(1, kv_lora), wuq, wukv, tab_q, tab_k)


def _mla_attn_kernel(q_ref, knc_ref, knl_ref, krc_ref, krl_ref, vc_ref, vl_ref, o_ref,
                     k_scr, v_scr, s_scr, p_scr, *, n_ctx):
    @pl.when(pl.program_id(2) == 0)
    def _():
        k_scr[0:n_ctx, 0:LANES] = knc_ref[...]
        k_scr[n_ctx:, 0:LANES] = knl_ref[...]
        k_scr[0:n_ctx, LANES:] = krc_ref[...]
        k_scr[n_ctx:, LANES:] = krl_ref[...]
        v_scr[0:n_ctx, 0:LANES] = vc_ref[...]
        v_scr[n_ctx:, 0:LANES] = vl_ref[...]
        v_scr[:, LANES:] = jnp.ones((v_scr.shape[0], LANES), BF16)

    half = q_ref.shape[0] // 2
    halves = [slice(0, half), slice(half, 2 * half)]
    _softmax_numerators([q_ref[r] for r in halves], k_scr, [slice(None)] * 2, s_scr, p_scr, False)
    for i, r in enumerate(halves):
        acc = _dot(p_scr[i], v_scr[...])
        o_ref[r] = (acc[:, :LANES] / acc[:, LANES:LANES + 1]).astype(BF16)


def _mla_attn(q, kv, kr, dims, tq):
    batch, seq, ctx, lat = dims["batch"], dims["seq"], dims["ctx"], dims["lat"]
    heads = q.shape[1] // (2 * LANES)
    nq = seq // tq
    ctx_blk0 = lat // ctx
    return pl.pallas_call(
        functools.partial(_mla_attn_kernel, n_ctx=ctx),
        grid=(batch, heads, nq),
        in_specs=[pl.BlockSpec((tq, 2 * LANES), lambda b, h, i: (b * nq + i, h)),
                  pl.BlockSpec((ctx, LANES), lambda b, h, i: (ctx_blk0 + b, 2 * h)),
                  pl.BlockSpec((seq, LANES), lambda b, h, i: (b, 2 * h)),
                  pl.BlockSpec((ctx, LANES), lambda b, h, i: (ctx_blk0 + b, 0)),
                  pl.BlockSpec((seq, LANES), lambda b, h, i: (b, 0)),
                  pl.BlockSpec((ctx, LANES), lambda b, h, i: (ctx_blk0 + b, 2 * h + 1)),
                  pl.BlockSpec((seq, LANES), lambda b, h, i: (b, 2 * h + 1))],
        out_specs=pl.BlockSpec((tq, LANES), lambda b, h, i: (b * nq + i, h)),
        out_shape=jax.ShapeDtypeStruct((lat, heads * LANES), BF16),
        scratch_shapes=[pltpu.VMEM((ctx + seq, 2 * LANES), BF16), pltpu.VMEM((ctx + seq, 2 * LANES), BF16),
                        pltpu.VMEM((2, tq // 2, ctx + seq), F32), pltpu.VMEM((2, tq // 2, ctx + seq), BF16)],
        compiler_params=_cparams(3),
        name="od_mla_attn",
    )(q, kv, kv, kr, kr, kv, kv)


def _router_kernel(x_ref, g_ref, mod_ref, r_ref, h_ref, idx_ref, w_ref):
    m = mod_ref[0]
    h = _norm_mod(x_ref[...], g_ref[...], m[3:4], m[4:5])
    h_ref[...] = h
    r = r_ref[...]
    h_hi = h.astype(BF16)
    h_lo = (h - h_hi.astype(F32)).astype(BF16)
    r_hi = r.astype(BF16)
    r_lo = (r - r_hi.astype(F32)).astype(BF16)
    logits = _dot(h_hi, r_hi) + (_dot(h_lo, r_hi) + _dot(h_hi, r_lo))
    lane = lax.broadcasted_iota(jnp.int32, logits.shape, 1)
    lane_f = lane.astype(F32)
    neg = jnp.float32(-jnp.inf)
    logits = jnp.where(lane < N_EXPERTS, logits, neg)
    m1 = jnp.max(logits, axis=-1, keepdims=True)
    i1 = jnp.min(jnp.where(logits == m1, lane_f, float(LANES)), axis=-1, keepdims=True)
    rest = jnp.where(lane_f == i1, neg, logits)
    m2 = jnp.max(rest, axis=-1, keepdims=True)
    i2 = jnp.min(jnp.where(rest == m2, lane_f, float(LANES)), axis=-1, keepdims=True)
    e2 = jnp.exp(m2 - m1)
    w1 = 1.0 / (1.0 + e2)
    w2 = e2 / (1.0 + e2)
    idx_ref[...] = jnp.where(lane == 0, i1, jnp.where(lane == 1, i2, 0.0)).astype(jnp.int32)
    w_ref[...] = jnp.where(lane == 0, w1, jnp.where(lane == 1, w2, 0.0))


def _router(x, g, mod, router_pad, dims, tm):
    lat, d = x.shape
    seq_tiles = dims["seq"] // tm
    return pl.pallas_call(
        _router_kernel,
        grid=(lat // tm,),
        in_specs=[pl.BlockSpec((tm, d), lambda i: (i, 0)),
                  pl.BlockSpec((1, d), lambda i: (0, 0)),
                  pl.BlockSpec((1, 6, d), lambda i: (i // seq_tiles, 0, 0)),
                  pl.BlockSpec((d, LANES), lambda i: (0, 0))],
        out_specs=[pl.BlockSpec((tm, d), lambda i: (i, 0)),
                   pl.BlockSpec((tm, LANES), lambda i: (i, 0)),
                   pl.BlockSpec((tm, LANES), lambda i: (i, 0))],
        out_shape=[jax.ShapeDtypeStruct((lat, d), F32),
                   jax.ShapeDtypeStruct((lat, LANES), jnp.int32),
                   jax.ShapeDtypeStruct((lat, LANES), F32)],
        compiler_params=_cparams(1),
        name="moe_router",
    )(x, g.reshape(1, d), mod, router_pad)


def _moe_plan(top_idx, n_items):
    e_flat = top_idx.reshape(-1)
    n_assign = e_flat.shape[0]
    onehot = (e_flat[:, None] == jnp.arange(N_EXPERTS, dtype=jnp.int32)[None, :]).astype(jnp.int32)
    csum = jnp.cumsum(onehot, axis=0)
    counts = csum[-1]
    rank = jnp.sum(csum * onehot, axis=1) - 1
    blocks = (counts + MOE_BLOCK - 1) // MOE_BLOCK
    blk_end = jnp.cumsum(blocks)
    blk_start = blk_end - blocks
    total = blk_end[-1]
    dest = blk_start[e_flat] * MOE_BLOCK + rank
    row_tok = jnp.zeros((n_items * MOE_BLOCK,), jnp.int32).at[dest].set(
        jnp.arange(n_assign, dtype=jnp.int32) // TOP_K)
    p = jnp.arange(n_items, dtype=jnp.int32)
    pc = jnp.minimum(p, total - 1)
    item_e = jnp.sum((pc[:, None] >= blk_end[None, :]).astype(jnp.int32), axis=1)
    rows_left = counts[item_e] - (pc - blk_start[item_e]) * MOE_BLOCK
    nact = jnp.clip((rows_left + MOE_SUB - 1) // MOE_SUB, 0, MOE_BLOCK // MOE_SUB)
    nact = jnp.where(p < total, nact, 0).astype(jnp.int32)
    out_blk = jnp.where(p < total, p, n_items).astype(jnp.int32)
    items = (item_e.astype(jnp.int32), pc.astype(jnp.int32), out_blk, nact)
    return dest.astype(jnp.int32), row_tok, items, (blocks, blk_start, blk_end, total)


def _moe_steps(items, runs, n_items, nj):
    _, _, _, nact = items
    blocks, blk_start, blk_end, total = runs
    s = jnp.arange(n_items * nj, dtype=jnp.int32)
    live = s < total * nj
    sc = jnp.minimum(s, total * nj - 1)
    e = jnp.sum((sc[:, None] >= (blk_end * nj)[None, :]).astype(jnp.int32), axis=1)
    t = sc - blk_start[e] * nj
    j = t // blocks[e]
    r = t % blocks[e]
    blk = blk_start[e] + jnp.where(j % 2 == 0, r, blocks[e] - 1 - r)
    out_blk = jnp.where(live, blk, n_items)
    out_j = jnp.where(live, j, 0)
    step_nact = jnp.where(live, nact[blk], 0)
    prev_e = jnp.concatenate([jnp.full((1,), -1, jnp.int32), e[:-1]])
    prev_j = jnp.concatenate([jnp.full((1,), -1, jnp.int32), j[:-1]])
    first = jnp.logical_and(live, jnp.logical_or(e != prev_e, j != prev_j))
    slot = (jnp.cumsum(first.astype(jnp.int32)) - 1) % 2
    ids = jnp.arange(N_EXPERTS, dtype=jnp.int32)
    later = jnp.logical_and(ids[None, :] > ids[:, None], blocks[None, :] > 0)
    next_expert = jnp.min(jnp.where(later, ids[None, :], N_EXPERTS), axis=1)
    wraps = j + 1 >= nj
    next_e = jnp.where(wraps, next_expert[e], e)
    next_j = jnp.where(wraps, 0, j + 1)
    has_next = jnp.logical_and(first, next_e < N_EXPERTS)
    next_e = jnp.minimum(next_e, N_EXPERTS - 1)
    return tuple(a.astype(jnp.int32) for a in (e, blk, j, out_blk, out_j, step_nact,
                                                first, slot, next_e, next_j, has_next))


def _gather_kernel(tok_ref, nact_ref, h_ref, o_ref, buf, sem):
    p = pl.program_id(0)
    nact = nact_ref[p]
    base = p * MOE_BLOCK

    def copy(r, t):
        return pltpu.make_async_copy(h_ref.at[pl.ds(t, 1)], buf.at[pl.ds(r, 1)], sem)

    def start(grp, c):
        for u in range(GATHER_UNROLL):
            r = grp * GATHER_UNROLL + u
            copy(r, tok_ref[base + r]).start()
        return c

    def wait(grp, c):
        for u in range(GATHER_UNROLL):
            copy(grp * GATHER_UNROLL + u, 0).wait()
        return c

    n_grp = nact * (MOE_SUB // GATHER_UNROLL)
    lax.fori_loop(0, n_grp, start, 0)
    lax.fori_loop(0, n_grp, wait, 0)
    for s in range(MOE_BLOCK // MOE_SUB):
        rows = pl.ds(s * MOE_SUB, MOE_SUB)

        @pl.when(s < nact)
        def _():
            o_ref[rows] = buf[rows].astype(BF16)

        @pl.when(s >= nact)
        def _():
            o_ref[rows] = jnp.zeros((MOE_SUB, o_ref.shape[1]), BF16)


def _moe_gather(h, row_tok, nact, n_items):
    d = h.shape[1]
    return pl.pallas_call(
        _gather_kernel,
        grid_spec=pltpu.PrefetchScalarGridSpec(
            num_scalar_prefetch=2,
            grid=(n_items,),
            in_specs=[pl.BlockSpec(memory_space=pl.ANY)],
            out_specs=pl.BlockSpec((MOE_BLOCK, d), lambda p, tok, na: (p, 0)),
            scratch_shapes=[pltpu.VMEM((MOE_BLOCK, d), F32), pltpu.SemaphoreType.DMA(())]),
        out_shape=jax.ShapeDtypeStruct((n_items * MOE_BLOCK, d), BF16),
        compiler_params=_cparams(1),
        name="moe_gather",
    )(row_tok, nact, h)


def _for_active_rows(nact, in_ref, o_ref, fn):
    n_sub = MOE_BLOCK // MOE_SUB
    for k in range(n_sub + 1):
        @pl.when(nact == k)
        def _():
            if k > 0:
                o_ref[0:k * MOE_SUB] = fn(in_ref[0:k * MOE_SUB])
            if k < n_sub:
                o_ref[k * MOE_SUB:] = jnp.zeros((MOE_BLOCK - k * MOE_SUB, o_ref.shape[1]), o_ref.dtype)


def _moe_glu_kernel(e_ref, blk_ref, j_ref, oblk_ref, oj_ref, nact_ref, first_ref, slot_ref, ne_ref, nj_ref,
                    more_ref, h_ref, wg_ref, wu_ref, o_ref, wbuf, sem):
    s = pl.program_id(0)
    nact = nact_ref[s]
    slot = slot_ref[s]
    tf = wbuf.shape[3]

    def tile_copies(e, j, to_slot):
        cols = pl.ds(pl.multiple_of(j * tf, tf), tf)
        return [pltpu.make_async_copy(w_ref.at[e, :, cols], wbuf.at[to_slot, i], sem.at[to_slot])
                for i, w_ref in enumerate((wg_ref, wu_ref))]

    @pl.when(s == 0)
    def _():
        for cp in tile_copies(e_ref[0], j_ref[0], 0):
            cp.start()

    @pl.when(first_ref[s] == 1)
    def _():
        for cp in tile_copies(e_ref[s], j_ref[s], slot):
            cp.wait()

        @pl.when(more_ref[s] == 1)
        def _():
            for cp in tile_copies(ne_ref[s], nj_ref[s], 1 - slot):
                cp.start()

    def run(h):
        gate = _dot(h, wbuf[slot, 0].astype(BF16))
        return (_silu(gate) * _dot(h, wbuf[slot, 1].astype(BF16))).astype(BF16)

    _for_active_rows(nact, h_ref, o_ref, run)


def _moe_glu(hs, wg, wu, steps, n_items, tf):
    d, f = wg.shape[1], wg.shape[2]
    nj = f // tf
    n_pre = len(steps)
    return pl.pallas_call(
        _moe_glu_kernel,
        grid_spec=pltpu.PrefetchScalarGridSpec(
            num_scalar_prefetch=n_pre,
            grid=(n_items * nj,),
            in_specs=[pl.BlockSpec((MOE_BLOCK, d), lambda s, *pre: (pre[1][s], 0)),
                      pl.BlockSpec(memory_space=pl.ANY),
                      pl.BlockSpec(memory_space=pl.ANY)],
            out_specs=pl.BlockSpec((MOE_BLOCK, tf), lambda s, *pre: (pre[3][s], pre[4][s])),
            scratch_shapes=[pltpu.VMEM((2, 2, d, tf), F32), pltpu.SemaphoreType.DMA((2,))]),
        out_shape=jax.ShapeDtypeStruct(((n_items + 1) * MOE_BLOCK, f), BF16),
        compiler_params=_cparams(1),
        name="moe_glu",
    )(*steps, hs, wg, wu)


def _moe_down_kernel(e_ref, blk_ref, oblk_ref, nact_ref, a_ref, wd_ref, o_ref):
    nact = nact_ref[pl.program_id(0)]
    _for_active_rows(nact, a_ref, o_ref, lambda a: _dot(a, wd_ref[0].astype(BF16)))


def _moe_down(a, wd, plan, n_items, tn):
    item_e, in_blk, out_blk, nact = plan
    f, d = wd.shape[1], wd.shape[2]
    nj = d // tn

    def w_map(p, j, e, b, ob, na):
        return (e[p], 0, jnp.where(na[p] > 0, j, nj - 1))

    def o_map(p, j, e, b, ob, na):
        return (ob[p], jnp.where(na[p] > 0, j, 0))

    return pl.pallas_call(
        _moe_down_kernel,
        grid_spec=pltpu.PrefetchScalarGridSpec(
            num_scalar_prefetch=4,
            grid=(n_items, nj),
            in_specs=[pl.BlockSpec((MOE_BLOCK, f), lambda p, j, e, b, ob, na: (b[p], 0)),
                      pl.BlockSpec((1, f, tn), w_map)],
            out_specs=pl.BlockSpec((MOE_BLOCK, tn), o_map)),
        out_shape=jax.ShapeDtypeStruct(((n_items + 1) * MOE_BLOCK, d), F32),
        compiler_params=_cparams(2),
        name="moe_down",
    )(item_e, in_blk, out_blk, nact, a, wd)


def _combine_kernel(dest_ref, x_ref, mod_ref, w_ref, fg_ref, y_ref, o_ref, buf, sem, *, tm):
    i = pl.program_id(0)

    def copy(r, k, row):
        return pltpu.make_async_copy(y_ref.at[pl.ds(row, 1)], buf.at[k, pl.ds(r, 1)], sem)

    def start(grp, c):
        for u in range(GATHER_UNROLL):
            r = grp * GATHER_UNROLL + u
            for k in range(TOP_K):
                copy(r, k, dest_ref[(i * tm + r) * TOP_K + k]).start()
        return c

    def wait(grp, c):
        for u in range(GATHER_UNROLL):
            for k in range(TOP_K):
                copy(grp * GATHER_UNROLL + u, k, 0).wait()
        return c

    lax.fori_loop(0, tm // GATHER_UNROLL, start, 0)
    lax.fori_loop(0, tm // GATHER_UNROLL, wait, 0)
    w = w_ref[...]
    moe = w[:, 0:1] * buf[0] + w[:, 1:2] * buf[1]
    m = mod_ref[0]
    o_ref[...] = _rms(x_ref[...] + m[5:6] * moe, fg_ref[...])


def _moe_combine(dest, x, mod, top_w, final_g, ys, dims, tm):
    lat, d = x.shape
    seq_tiles = dims["seq"] // tm
    return pl.pallas_call(
        functools.partial(_combine_kernel, tm=tm),
        grid_spec=pltpu.PrefetchScalarGridSpec(
            num_scalar_prefetch=1,
            grid=(lat // tm,),
            in_specs=[pl.BlockSpec((tm, d), lambda i, dr: (i, 0)),
                      pl.BlockSpec((1, 6, d), lambda i, dr: (i // seq_tiles, 0, 0)),
                      pl.BlockSpec((tm, LANES), lambda i, dr: (i, 0)),
                      pl.BlockSpec((1, d), lambda i, dr: (0, 0)),
                      pl.BlockSpec(memory_space=pl.ANY)],
            out_specs=pl.BlockSpec((tm, d), lambda i, dr: (i, 0)),
            scratch_shapes=[pltpu.VMEM((TOP_K, tm, d), F32), pltpu.SemaphoreType.DMA(())]),
        out_shape=jax.ShapeDtypeStruct((lat, d), F32),
        compiler_params=_cparams(1),
        name="moe_combine",
    )(dest, x, mod, top_w, final_g.reshape(1, d), ys)


def kernel(x, c, ctx, c_ctx, ada_w, ada_b, norm1_g, norm2_g, ev_w_in, ev_w_out, ev_lambda, ev_subln_g, od_w_dq, od_q_norm_g, od_w_uq, od_w_dkv, od_kv_norm_g, od_w_ukv, od_w_o, ffn_w_gate, ffn_w_up, ffn_w_down, moe_router, moe_w_gate, moe_w_up, moe_w_down, final_norm_g):
    batch, seq, d = x.shape
    n_ctx = ctx.shape[1]
    depth = ada_w.shape[0]
    assert depth == 2 and batch < MOD_ROWS and seq % GRID_W == 0
    lat = batch * seq
    dims = dict(batch=batch, seq=seq, ctx=n_ctx, lat=lat)
    tm = 1024
    assert seq % tm == 0 and (batch * n_ctx) % tm == 0

    cond = jnp.concatenate([c, c_ctx[None, :], jnp.zeros((MOD_ROWS - batch - 1, d), F32)], axis=0)
    mod = _ada(cond, ada_w, ada_b).reshape(depth, MOD_ROWS, 6, d)
    x_lat, x_ctx = x.reshape(lat, d), ctx.reshape(batch * n_ctx, d)
    m_rows = lat + batch * n_ctx

    lam_init = 0.8 - 0.6 * math.exp(-0.3 * 0)
    fw = ev_w_in.shape[2] // 4
    tabs_ev = jnp.asarray(np.stack([
        _rope_tables(seq, tm, DIFF_HEAD_DIM // 4, DIFF_HEAD_DIM ** -0.5 * LOG2E),
        _rope_tables(seq, tm, DIFF_HEAD_DIM // 4, 1.0)]))
    w_in = ev_w_in[0].astype(BF16)
    qkvf = _inproj(x_lat, None, norm1_g[0], mod[0], w_in, tabs_ev, dims, tm, 1024)
    qkvf = _inproj(x_ctx, qkvf, norm1_g[0], mod[0], w_in, tabs_ev, dims, tm, 1024)
    o_attn = _diff_attn(qkvf, None, ev_lambda[0], ev_subln_g[0], lam_init, dims, 512, True)
    o_attn = _diff_attn(qkvf, o_attn, ev_lambda[0], ev_subln_g[0], lam_init, dims, n_ctx, False)

    gw = fw // FOURIER_GROUPS
    cc_np, sc_np = _dft_cos_sin(gw)
    cc = jnp.asarray(cc_np.astype(np.float32)).astype(BF16)
    sc = jnp.asarray(sc_np.astype(np.float32)).astype(BF16)
    cn_np, sn_np = _dft_cos_sin(seq)
    cs_lat = jnp.asarray(np.concatenate([cn_np, -sn_np], axis=1).astype(np.float32)).astype(BF16)
    cx_np, sx_np = _dft_cos_sin(n_ctx)
    cs_ctx = jnp.asarray(np.concatenate([cx_np, -sx_np], axis=1).astype(np.float32)).astype(BF16)
    fm = _fourier(qkvf, None, seq, 0, batch, fw, cc, sc, cs_lat, "ev_fourier_lat")
    fm = _fourier(qkvf, fm, n_ctx, lat // n_ctx, batch, fw, cc, sc, cs_ctx, "ev_fourier_ctx")

    w_out = ev_w_out[0].astype(BF16)
    xs = _mm_res([fm, o_attn], w_out, x_lat, mod[0], 2, dims, lat, tm, 1024, "ev_outproj_lat", out_rows=m_rows)
    xs = _mm_res([fm, o_attn], w_out, x_ctx, mod[0], 2, dims, batch * n_ctx, tm, 1024, "ev_outproj_ctx", prev=xs)
    act = _glu(xs, norm2_g[0], mod[0], ffn_w_gate[0], ffn_w_up[0], dims, tm, 512)
    xs = _mm_res([act], ffn_w_down[0].astype(BF16), xs, mod[0], 5, dims, m_rows, tm, 512, "ffn_down")

    heads = od_w_ukv.shape[2] // (MLA_NOPE + MLA_V)
    q_lora = od_w_dq.shape[2]
    wd_cat = jnp.concatenate(
        [od_w_dq[0], od_w_dkv[0], jnp.zeros((d, LANES - MLA_ROPE), F32)], axis=1).astype(BF16)
    wuq = jnp.pad(od_w_uq[0].reshape(q_lora, heads, MLA_NOPE + MLA_ROPE),
                  ((0, 0), (0, 0), (0, 2 * LANES - MLA_NOPE - MLA_ROPE))).reshape(q_lora, heads * 2 * LANES)
    tm_mla = 256
    tab_q = jnp.asarray(_rope_tables(seq, tm_mla, MLA_ROPE // 4, (MLA_NOPE + MLA_ROPE) ** -0.5 * LOG2E))
    tab_k = jnp.asarray(_rope_tables(seq, tm_mla, MLA_ROPE // 4, 1.0))
    q, kv, kr = _mla_proj(xs, norm1_g[1], mod[1], wd_cat, od_q_norm_g[0], od_kv_norm_g[0],
                          wuq.astype(BF16), od_w_ukv[0].astype(BF16), tab_q, tab_k, dims, tm_mla)
    o_mla = _mla_attn(q, kv, kr, dims, 1024)
    xl = _mm_res([o_mla], od_w_o[0].astype(BF16), xs, mod[1], 2, dims, lat, tm, 1024, "od_outproj")

    router_pad = jnp.pad(moe_router[0], ((0, 0), (0, LANES - N_EXPERTS)))
    h2, top_idx, top_w = _router(xl, norm2_g[1], mod[1], router_pad, dims, 512)
    n_items = lat * TOP_K // MOE_BLOCK + N_EXPERTS
    dest, row_tok, items, runs = _moe_plan(top_idx[:, :TOP_K], n_items)
    tf = 512
    steps = _moe_steps(items, runs, n_items, moe_w_gate.shape[3] // tf)
    hs = _moe_gather(h2, row_tok, items[3], n_items)
    act = _moe_glu(hs, moe_w_gate[0], moe_w_up[0], steps, n_items, tf)
    ys = _moe_down(act, moe_w_down[0], items, n_items, 256)
    out = _moe_combine(dest, xl, mod[1], top_w, final_norm_g, ys, dims, 256)
    return out.reshape(batch, seq, d)
```

```python
import functools
import math

import numpy as np
import jax
import jax.numpy as jnp
from jax import lax
from jax.experimental import pallas as pl
from jax.experimental.pallas import tpu as pltpu

F32 = jnp.float32
BF16 = jnp.bfloat16

GRID_W = 64
NORM_EPS = 1e-6
ROPE_BASE = 10000.0
FOURIER_GROUPS = 4
DIFF_HEAD_DIM = 128
MLA_NOPE = 128
MLA_ROPE = 64
MLA_V = 128
N_EXPERTS = 8
TOP_K = 2

LANES = 128
MOD_ROWS = 8
VMEM_LIMIT = 56 * 1024 * 1024
MOE_BLOCK = 1024
MOE_SUB = 256
GATHER_UNROLL = 8
KEY_CHUNK = 256
LOG2E = math.log2(math.e)


def _cparams(n_axes):
    return pltpu.CompilerParams(dimension_semantics=("arbitrary",) * n_axes,
                                vmem_limit_bytes=VMEM_LIMIT)


def _rms(x, g):
    return x * lax.rsqrt(jnp.mean(x * x, axis=-1, keepdims=True) + NORM_EPS) * g


def _norm_mod(x, g, shift, scale):
    return _rms(x, g) * (1.0 + scale) + shift


def _silu(x):
    return x * (1.0 / (1.0 + jnp.exp(-x)))


def _dot(a, b):
    return jnp.dot(a, b, preferred_element_type=F32)


def _dot_nt(a, b):
    return lax.dot_general(a, b, (((1,), (1,)), ((), ())), preferred_element_type=F32)


def _rope_tables(seq, extra_rows, chunk, scale):
    n = np.arange(seq)
    row, col = n // GRID_W, n % GRID_W
    lane = np.arange(LANES)
    a = 2 * chunk
    inv = ROPE_BASE ** (-np.arange(0, a, 2, dtype=np.float64) / a)
    used = lane < 4 * chunk
    freq = inv[lane % chunk]
    pos = np.where(lane[None, :] < 2 * chunk, row[:, None], col[:, None]).astype(np.float64)
    ang = pos * freq[None, :]
    first = (lane // chunk) % 2 == 0
    cos = np.where(used[None, :], np.cos(ang), 0.0)
    sin = np.where(used[None, :], np.sin(ang), 0.0)
    s1 = np.where(first[None, :], -sin, 0.0)
    s2 = np.where(first[None, :], 0.0, sin)
    ident = np.zeros((3, extra_rows, LANES))
    ident[0] = used[None, :].astype(np.float64)
    tab = np.concatenate([np.stack([cos, s1, s2]), ident], axis=1) * scale
    return tab.astype(np.float32)


def _apply_rope(x, tab_ref, chunk):
    return (x * tab_ref[0] + pltpu.roll(x, LANES - chunk, 1) * tab_ref[1]
            + pltpu.roll(x, chunk, 1) * tab_ref[2])


def _dft_cos_sin(n):
    k = np.arange(n)
    ang = 2.0 * np.pi * ((k[:, None] * k[None, :]) % n) / n
    return np.cos(ang), np.sin(ang)


def _ada_kernel(s_ref, w_ref, b_ref, o_ref):
    s = _silu(s_ref[...]).astype(BF16)
    o_ref[0] = _dot(s, w_ref[0].astype(BF16)) + b_ref[0]


def _ada(cond, ada_w, ada_b):
    depth, d, n = ada_w.shape
    tn = 1024
    return pl.pallas_call(
        _ada_kernel,
        grid=(depth, n // tn),
        in_specs=[pl.BlockSpec((MOD_ROWS, d), lambda i, j: (0, 0)),
                  pl.BlockSpec((1, d, tn), lambda i, j: (i, 0, j)),
                  pl.BlockSpec((1, 1, tn), lambda i, j: (i, 0, j))],
        out_specs=pl.BlockSpec((1, MOD_ROWS, tn), lambda i, j: (i, 0, j)),
        out_shape=jax.ShapeDtypeStruct((depth, MOD_ROWS, n), F32),
        compiler_params=_cparams(2),
        name="ada",
    )(cond, ada_w, ada_b.reshape(depth, 1, n))


def _inproj_kernel(x_ref, g_ref, mod_ref, w_ref, tab_ref, *rest):
    o_ref, h_ref = rest[-2:]
    j = pl.program_id(1)
    quarter = pl.num_programs(1) // 4

    @pl.when(j == 0)
    def _():
        m = mod_ref[0]
        h_ref[...] = _norm_mod(x_ref[...], g_ref[...], m[0:1], m[1:2]).astype(BF16)

    is_rope = jnp.logical_and(j >= quarter, j < 3 * quarter)

    @pl.when(is_rope)
    def _():
        res = _dot(h_ref[...], w_ref[...].astype(BF16))
        for c in range(res.shape[1] // LANES):
            sl = slice(c * LANES, (c + 1) * LANES)
            o_ref[:, sl] = _apply_rope(res[:, sl], tab_ref.at[0], DIFF_HEAD_DIM // 4).astype(BF16)

    @pl.when(jnp.logical_not(is_rope))
    def _():
        o_ref[...] = _dot(h_ref[...], w_ref[...].astype(BF16)).astype(BF16)


def _inproj(x, prev, g, mod, w, tabs, dims, tm, tn):
    rows, d = x.shape
    seq, lat, batch = dims["seq"], dims["lat"], dims["batch"]
    m_rows = lat + batch * dims["ctx"]
    n = w.shape[1]
    nj = n // tn
    assert nj % 4 == 0
    lat_tiles, seq_tiles = lat // tm, seq // tm
    is_ctx = prev is not None
    row_blk0 = lat_tiles if is_ctx else 0

    def tab_map(i, j):
        return (jnp.where(j >= nj // 2, 1, 0), 0, seq_tiles if is_ctx else i % seq_tiles, 0)

    in_specs = [pl.BlockSpec((tm, d), lambda i, j: (i, 0)),
                pl.BlockSpec((1, d), lambda i, j: (0, 0)),
                pl.BlockSpec((1, 6, d), lambda i, j: (batch if is_ctx else i // seq_tiles, 0, 0)),
                pl.BlockSpec((d, tn), lambda i, j: (0, j)),
                pl.BlockSpec((1, 3, tm, LANES), tab_map)]
    args = [x, g.reshape(1, d), mod, w, tabs]
    aliases = {}
    if is_ctx:
        in_specs.append(pl.BlockSpec(memory_space=pl.ANY))
        args.append(prev)
        aliases = {len(args) - 1: 0}
    return pl.pallas_call(
        _inproj_kernel,
        grid=(rows // tm, nj),
        in_specs=in_specs,
        out_specs=pl.BlockSpec((tm, tn), lambda i, j: (row_blk0 + i, j)),
        out_shape=jax.ShapeDtypeStruct((m_rows, n), BF16),
        scratch_shapes=[pltpu.VMEM((tm, d), BF16)],
        input_output_aliases=aliases,
        compiler_params=_cparams(2),
        name="ev_inproj_ctx" if is_ctx else "ev_inproj_lat",
    )(*args)


def _softmax_numerators(qs, k_ref, kcols, s_scr, p_scr, want_sum):
    n_keys = k_ref.shape[0]
    chunks = [slice(c, c + KEY_CHUNK) for c in range(0, n_keys, KEY_CHUNK)]
    maxes = []
    for i, (q, cols) in enumerate(zip(qs, kcols)):
        m = None
        for ks in chunks:
            s = _dot_nt(q, k_ref[ks, cols])
            s_scr[i, :, ks] = s
            mc = jnp.max(s, axis=-1, keepdims=True)
            m = mc if m is None else jnp.maximum(m, mc)
        maxes.append(m)
    totals = []
    for i, m in enumerate(maxes):
        total = None
        for ks in chunks:
            e = jnp.exp2(s_scr[i, :, ks] - m)
            if want_sum:
                part = jnp.sum(e, axis=-1, keepdims=True)
                total = part if total is None else total + part
            p_scr[i, :, ks] = e.astype(BF16)
        totals.append(total)
    return totals


def _diff_attn_kernel(lam_ref, q_ref, *rest, lam_init, seg_rows):
    n_seg = len(seg_rows)
    k_refs, v_refs, g_ref = rest[:n_seg], rest[n_seg:2 * n_seg], rest[2 * n_seg]
    o_ref, k_scr, v_scr, s_scr, p_scr = rest[-5:]
    hd = DIFF_HEAD_DIM

    @pl.when(pl.program_id(2) == 0)
    def _():
        r0 = 0
        for k_ref, v_ref, n in zip(k_refs, v_refs, seg_rows):
            k_scr[r0:r0 + n] = k_ref[...]
            v_scr[r0:r0 + n] = v_ref[...]
            r0 += n

    lv = lam_ref[...]
    lam = (jnp.exp(jnp.sum(lv[0:1] * lv[1:2], axis=-1, keepdims=True))
           - jnp.exp(jnp.sum(lv[2:3] * lv[3:4], axis=-1, keepdims=True)) + lam_init)
    q = q_ref[...]
    cols = [slice(c * hd, (c + 1) * hd) for c in range(2)]
    totals = _softmax_numerators([q[:, c] for c in cols], k_scr, cols, s_scr, p_scr, True)
    outs = [_dot(p_scr[c], v_scr[...]) * (1.0 / totals[c]) for c in range(2)]
    o = outs[0] - lam * outs[1]
    o_ref[...] = (_rms(o, g_ref[...]) * (1.0 - lam_init)).astype(BF16)


def _diff_attn(qkvf, prev, lam_vec, subln_g, lam_init, dims, tq, latent):
    m_rows = qkvf.shape[0]
    batch, seq, ctx = dims["batch"], dims["seq"], dims["ctx"]
    hw = 2 * DIFF_HEAD_DIM
    width = qkvf.shape[1] // 4
    heads = width // hw
    ctx_blk0 = batch * seq // ctx
    qcol, kcol, vcol = width // hw, 2 * width // hw, 3 * width // hw
    if latent:
        nq, q_blk0, seg_rows = seq // tq, 0, (ctx, seq)
    else:
        assert tq == ctx
        nq, q_blk0, seg_rows = 1, ctx_blk0, (ctx,)
    n_keys = sum(seg_rows)

    def kv_specs(col):
        specs = [pl.BlockSpec((ctx, hw), lambda b, h, i: (ctx_blk0 + b, col + h))]
        if latent:
            specs.append(pl.BlockSpec((seq, hw), lambda b, h, i: (b, col + h)))
        return specs

    in_specs = ([pl.BlockSpec((4, DIFF_HEAD_DIM), lambda b, h, i: (0, 0)),
                 pl.BlockSpec((tq, hw), lambda b, h, i: (q_blk0 + b * nq + i, qcol + h))]
                + kv_specs(kcol) + kv_specs(vcol)
                + [pl.BlockSpec((1, hw), lambda b, h, i: (0, 0))])
    args = [lam_vec, qkvf] + [qkvf] * (2 * len(seg_rows)) + [subln_g.reshape(1, hw)]
    aliases = {}
    if prev is not None:
        in_specs.append(pl.BlockSpec(memory_space=pl.ANY))
        args.append(prev)
        aliases = {len(args) - 1: 0}
    kern = functools.partial(_diff_attn_kernel, lam_init=lam_init, seg_rows=seg_rows)
    return pl.pallas_call(
        kern,
        grid=(batch, heads, nq),
        in_specs=in_specs,
        out_specs=pl.BlockSpec((tq, hw), lambda b, h, i: (q_blk0 + b * nq + i, h)),
        out_shape=jax.ShapeDtypeStruct((m_rows, width), BF16),
        scratch_shapes=[pltpu.VMEM((n_keys, hw), BF16), pltpu.VMEM((n_keys, hw), BF16),
                        pltpu.VMEM((2, tq, n_keys), F32), pltpu.VMEM((2, tq, n_keys), BF16)],
        input_output_aliases=aliases,
        compiler_params=_cparams(3),
        name="ev_diff_attn_lat" if latent else "ev_diff_attn_ctx",
    )(*args)


def _fourier_kernel(u_ref, cc_ref, sc_ref, cs_ref, *rest, n, norm):
    o_ref, ab_ref = rest[-2], rest[-1]
    u = u_ref[...]
    ab_ref[0:n] = _dot(u, cc_ref[...]).astype(BF16)
    ab_ref[n:] = _dot(u, sc_ref[...]).astype(BF16)
    o_ref[...] = (_dot(cs_ref[...], ab_ref[...]) * norm).astype(BF16)


def _fourier(qkvf, prev, n, row_blk0, batch, width, cc, sc, cs, name):
    m_rows = qkvf.shape[0]
    gw = width // FOURIER_GROUPS
    kern = functools.partial(_fourier_kernel, n=n, norm=1.0 / math.sqrt(n * gw))
    in_specs = [pl.BlockSpec((n, gw), lambda b, g: (row_blk0 + b, g)),
                pl.BlockSpec((gw, gw), lambda b, g: (0, 0)),
                pl.BlockSpec((gw, gw), lambda b, g: (0, 0)),
                pl.BlockSpec((n, 2 * n), lambda b, g: (0, 0), pipeline_mode=pl.Buffered(1))]
    args = [qkvf, cc, sc, cs]
    aliases = {}
    if prev is not None:
        in_specs.append(pl.BlockSpec(memory_space=pl.ANY))
        args.append(prev)
        aliases = {4: 0}
    return pl.pallas_call(
        kern,
        grid=(batch, FOURIER_GROUPS),
        in_specs=in_specs,
        out_specs=pl.BlockSpec((n, gw), lambda b, g: (row_blk0 + b, g)),
        out_shape=jax.ShapeDtypeStruct((m_rows, width), BF16),
        scratch_shapes=[pltpu.VMEM((2 * n, gw), BF16)],
        input_output_aliases=aliases,
        compiler_params=_cparams(2),
        name=name,
    )(*args)


def _mm_res_kernel(*refs, n_a, gate_idx):
    a_refs, w_refs = refs[:n_a], refs[n_a:2 * n_a]
    x_ref, mod_ref, o_ref = refs[2 * n_a], refs[2 * n_a + 1], refs[-1]
    acc = _dot(a_refs[0][...], w_refs[0][...].astype(BF16))
    for a_ref, w_ref in zip(a_refs[1:], w_refs[1:]):
        acc = acc + _dot(a_ref[...], w_ref[...].astype(BF16))
    m = mod_ref[0]
    o_ref[...] = x_ref[...] + m[gate_idx:gate_idx + 1] * acc


def _mm_res(a_list, w, x, mod, gate_idx, dims, rows, tm, tn, name, out_rows=None, prev=None):
    d = w.shape[1]
    seq_tiles = dims["seq"] // tm
    n_a = len(a_list)
    ctx_only = prev is not None
    row_blk0 = dims["lat"] // tm if ctx_only else 0
    out_rows = prev.shape[0] if ctx_only else (out_rows or rows)

    def sel(i):
        return dims["batch"] if ctx_only else jnp.minimum(i // seq_tiles, dims["batch"])

    in_specs, w_args, k0 = [], [], 0
    for a in a_list:
        in_specs.append(pl.BlockSpec((tm, a.shape[1]), lambda i, j: (row_blk0 + i, 0)))
    for a in a_list:
        ka = a.shape[1]
        assert k0 % ka == 0
        in_specs.append(pl.BlockSpec((ka, tn), lambda i, j, kb=k0 // ka: (kb, j)))
        w_args.append(w)
        k0 += ka
    assert k0 == w.shape[0]
    in_specs += [pl.BlockSpec((tm, tn), lambda i, j: (i, j)),
                 pl.BlockSpec((1, 6, tn), lambda i, j: (sel(i), 0, j))]
    args = [*a_list, *w_args, x, mod]
    aliases = {}
    if prev is not None:
        in_specs.append(pl.BlockSpec(memory_space=pl.ANY))
        args.append(prev)
        aliases = {len(args) - 1: 0}
    return pl.pallas_call(
        functools.partial(_mm_res_kernel, n_a=n_a, gate_idx=gate_idx),
        grid=(rows // tm, d // tn),
        in_specs=in_specs,
        out_specs=pl.BlockSpec((tm, tn), lambda i, j: (row_blk0 + i, j)),
        out_shape=jax.ShapeDtypeStruct((out_rows, d), F32),
        input_output_aliases=aliases,
        compiler_params=_cparams(2),
        name=name,
    )(*args)


def _glu_kernel(x_ref, g_ref, mod_ref, wg_ref, wu_ref, o_ref, h_ref):
    @pl.when(pl.program_id(1) == 0)
    def _():
        m = mod_ref[0]
        h_ref[...] = _norm_mod(x_ref[...], g_ref[...], m[3:4], m[4:5]).astype(BF16)

    h = h_ref[...]
    gate = _dot(h, wg_ref[...].astype(BF16))
    up = _dot(h, wu_ref[...].astype(BF16))
    o_ref[...] = (_silu(gate) * up).astype(BF16)


def _glu(x, g, mod, wg, wu, dims, tm, tf):
    m_rows, d = x.shape
    f = wg.shape[1]
    seq_tiles = dims["seq"] // tm

    def sel(i):
        return jnp.minimum(i // seq_tiles, dims["batch"])

    return pl.pallas_call(
        _glu_kernel,
        grid=(m_rows // tm, f // tf),
        in_specs=[pl.BlockSpec((tm, d), lambda i, j: (i, 0)),
                  pl.BlockSpec((1, d), lambda i, j: (0, 0)),
                  pl.BlockSpec((1, 6, d), lambda i, j: (sel(i), 0, 0)),
                  pl.BlockSpec((d, tf), lambda i, j: (0, j)),
                  pl.BlockSpec((d, tf), lambda i, j: (0, j))],
        out_specs=pl.BlockSpec((tm, tf), lambda i, j: (i, j)),
        out_shape=jax.ShapeDtypeStruct((m_rows, f), BF16),
        scratch_shapes=[pltpu.VMEM((tm, d), BF16)],
        compiler_params=_cparams(2),
        name="ffn_glu",
    )(x, g.reshape(1, d), mod, wg, wu)


def _mla_proj_kernel(x_ref, g_ref, mod_ref, wd_ref, qg_ref, kvg_ref, wuq_ref, wukv_ref,
                     tq_ref, tk_ref, q_ref, kv_ref, kr_ref, *, n_lat_tiles, q_lora, kv_lora, heads):
    i = pl.program_id(0)
    m = mod_ref[0]
    h = _norm_mod(x_ref[...], g_ref[...], m[0:1], m[1:2]).astype(BF16)
    t = _dot(h, wd_ref[...])
    ckv = _rms(t[:, q_lora:q_lora + kv_lora], kvg_ref[...]).astype(BF16)
    kv_ref[...] = _dot(ckv, wukv_ref[...]).astype(BF16)
    kr = t[:, q_lora + kv_lora:]
    kr_ref[...] = _apply_rope(kr, tk_ref, MLA_ROPE // 4).astype(BF16)

    @pl.when(i < n_lat_tiles)
    def _():
        cq = _rms(t[:, :q_lora], qg_ref[...]).astype(BF16)
        q = _dot(cq, wuq_ref[...])
        scale = (MLA_NOPE + MLA_ROPE) ** -0.5 * LOG2E
        for hh in range(heads):
            c0 = hh * 2 * LANES
            q_ref[:, c0:c0 + LANES] = (q[:, c0:c0 + LANES] * scale).astype(BF16)
            q_ref[:, c0 + LANES:c0 + 2 * LANES] = _apply_rope(
                q[:, c0 + LANES:c0 + 2 * LANES], tq_ref, MLA_ROPE // 4).astype(BF16)


def _mla_proj(x, g, mod, wd, qg, kvg, wuq, wukv, tab_q, tab_k, dims, tm):
    m_rows, d = x.shape
    seq, lat = dims["seq"], dims["lat"]
    q_lora, kv_lora = qg.shape[0], kvg.shape[0]
    heads = wukv.shape[1] // (MLA_NOPE + MLA_V)
    lat_tiles, seq_tiles = lat // tm, seq // tm

    def sel(i):
        return jnp.minimum(i // seq_tiles, dims["batch"])

    def tab_map(i):
        return (0, jnp.where(i < lat_tiles, i % seq_tiles, seq_tiles), 0)

    kern = functools.partial(_mla_proj_kernel, n_lat_tiles=lat_tiles, q_lora=q_lora, kv_lora=kv_lora,
                             heads=heads)
    const = lambda i: (0, 0)
    return pl.pallas_call(
        kern,
        grid=(m_rows // tm,),
        in_specs=[pl.BlockSpec((tm, d), lambda i: (i, 0)),
                  pl.BlockSpec((1, d), const),
                  pl.BlockSpec((1, 6, d), lambda i: (sel(i), 0, 0)),
                  pl.BlockSpec(wd.shape, const),
                  pl.BlockSpec((1, q_lora), const),
                  pl.BlockSpec((1, kv_lora), const),
                  pl.BlockSpec(wuq.shape, const),
                  pl.BlockSpec(wukv.shape, const),
                  pl.BlockSpec((3, tm, LANES), tab_map),
                  pl.BlockSpec((3, tm, LANES), tab_map)],
        out_specs=[pl.BlockSpec((tm, wuq.shape[1]), lambda i: (jnp.minimum(i, lat_tiles - 1), 0)),
                   pl.BlockSpec((tm, wukv.shape[1]), lambda i: (i, 0)),
                   pl.BlockSpec((tm, LANES), lambda i: (i, 0))],
        out_shape=[jax.ShapeDtypeStruct((lat, wuq.shape[1]), BF16),
                   jax.ShapeDtypeStruct((m_rows, wukv.shape[1]), BF16),
                   jax.ShapeDtypeStruct((m_rows, LANES), BF16)],
        compiler_params=_cparams(1),
        name="od_mla_proj",
    )(x, g.reshape(1, d), mod, wd, qg.reshape(1, q_lora), kvg.reshape(1, kv_lora), wuq, wukv, tab_q, tab_k)


def _mla_attn_kernel(q_ref, knc_ref, knl_ref, krc_ref, krl_ref, vc_ref, vl_ref, o_ref,
                     k_scr, v_scr, s_scr, p_scr, *, n_ctx):
    @pl.when(pl.program_id(2) == 0)
    def _():
        k_scr[0:n_ctx, 0:LANES] = knc_ref[...]
        k_scr[n_ctx:, 0:LANES] = knl_ref[...]
        k_scr[0:n_ctx, LANES:] = krc_ref[...]
        k_scr[n_ctx:, LANES:] = krl_ref[...]
        v_scr[0:n_ctx, 0:LANES] = vc_ref[...]
        v_scr[n_ctx:, 0:LANES] = vl_ref[...]
        v_scr[:, LANES:] = jnp.ones((v_scr.shape[0], LANES), BF16)

    half = q_ref.shape[0] // 2
    halves = [slice(0, half), slice(half, 2 * half)]
    _softmax_numerators([q_ref[r] for r in halves], k_scr, [slice(None)] * 2, s_scr, p_scr, False)
    for i, r in enumerate(halves):
        acc = _dot(p_scr[i], v_scr[...])
        o_ref[r] = (acc[:, :LANES] / acc[:, LANES:LANES + 1]).astype(BF16)


def _mla_attn(q, kv, kr, dims, tq):
    batch, seq, ctx, lat = dims["batch"], dims["seq"], dims["ctx"], dims["lat"]
    heads = q.shape[1] // (2 * LANES)
    nq = seq // tq
    ctx_blk0 = lat // ctx
    return pl.pallas_call(
        functools.partial(_mla_attn_kernel, n_ctx=ctx),
        grid=(batch, heads, nq),
        in_specs=[pl.BlockSpec((tq, 2 * LANES), lambda b, h, i: (b * nq + i, h)),
                  pl.BlockSpec((ctx, LANES), lambda b, h, i: (ctx_blk0 + b, 2 * h)),
                  pl.BlockSpec((seq, LANES), lambda b, h, i: (b, 2 * h)),
                  pl.BlockSpec((ctx, LANES), lambda b, h, i: (ctx_blk0 + b, 0)),
                  pl.BlockSpec((seq, LANES), lambda b, h, i: (b, 0)),
                  pl.BlockSpec((ctx, LANES), lambda b, h, i: (ctx_blk0 + b, 2 * h + 1)),
                  pl.BlockSpec((seq, LANES), lambda b, h, i: (b, 2 * h + 1))],
        out_specs=pl.BlockSpec((tq, LANES), lambda b, h, i: (b * nq + i, h)),
        out_shape=jax.ShapeDtypeStruct((lat, heads * LANES), BF16),
        scratch_shapes=[pltpu.VMEM((ctx + seq, 2 * LANES), BF16), pltpu.VMEM((ctx + seq, 2 * LANES), BF16),
                        pltpu.VMEM((2, tq // 2, ctx + seq), F32), pltpu.VMEM((2, tq // 2, ctx + seq), BF16)],
        compiler_params=_cparams(3),
        name="od_mla_attn",
    )(q, kv, kv, kr, kr, kv, kv)


def _router_kernel(x_ref, g_ref, mod_ref, r_ref, h_ref, idx_ref, w_ref):
    m = mod_ref[0]
    h = _norm_mod(x_ref[...], g_ref[...], m[3:4], m[4:5])
    h_ref[...] = h
    r = r_ref[...]
    h_hi = h.astype(BF16)
    h_lo = (h - h_hi.astype(F32)).astype(BF16)
    r_hi = r.astype(BF16)
    r_lo = (r - r_hi.astype(F32)).astype(BF16)
    logits = _dot(h_hi, r_hi) + (_dot(h_lo, r_hi) + _dot(h_hi, r_lo))
    lane = lax.broadcasted_iota(jnp.int32, logits.shape, 1)
    lane_f = lane.astype(F32)
    neg = jnp.float32(-jnp.inf)
    logits = jnp.where(lane < N_EXPERTS, logits, neg)
    m1 = jnp.max(logits, axis=-1, keepdims=True)
    i1 = jnp.min(jnp.where(logits == m1, lane_f, float(LANES)), axis=-1, keepdims=True)
    rest = jnp.where(lane_f == i1, neg, logits)
    m2 = jnp.max(rest, axis=-1, keepdims=True)
    i2 = jnp.min(jnp.where(rest == m2, lane_f, float(LANES)), axis=-1, keepdims=True)
    e2 = jnp.exp(m2 - m1)
    w1 = 1.0 / (1.0 + e2)
    w2 = e2 / (1.0 + e2)
    idx_ref[...] = jnp.where(lane == 0, i1, jnp.where(lane == 1, i2, 0.0)).astype(jnp.int32)
    w_ref[...] = jnp.where(lane == 0, w1, jnp.where(lane == 1, w2, 0.0))


def _router(x, g, mod, router_pad, dims, tm):
    lat, d = x.shape
    seq_tiles = dims["seq"] // tm
    return pl.pallas_call(
        _router_kernel,
        grid=(lat // tm,),
        in_specs=[pl.BlockSpec((tm, d), lambda i: (i, 0)),
                  pl.BlockSpec((1, d), lambda i: (0, 0)),
                  pl.BlockSpec((1, 6, d), lambda i: (i // seq_tiles, 0, 0)),
                  pl.BlockSpec((d, LANES), lambda i: (0, 0))],
        out_specs=[pl.BlockSpec((tm, d), lambda i: (i, 0)),
                   pl.BlockSpec((tm, LANES), lambda i: (i, 0)),
                   pl.BlockSpec((tm, LANES), lambda i: (i, 0))],
        out_shape=[jax.ShapeDtypeStruct((lat, d), F32),
                   jax.ShapeDtypeStruct((lat, LANES), jnp.int32),
                   jax.ShapeDtypeStruct((lat, LANES), F32)],
        compiler_params=_cparams(1),
        name="moe_router",
    )(x, g.reshape(1, d), mod, router_pad)


def _moe_plan(top_idx, n_items):
    e_flat = top_idx.reshape(-1)
    n_assign = e_flat.shape[0]
    onehot = (e_flat[:, None] == jnp.arange(N_EXPERTS, dtype=jnp.int32)[None, :]).astype(jnp.int32)
    csum = jnp.cumsum(onehot, axis=0)
    counts = csum[-1]
    rank = jnp.sum(csum * onehot, axis=1) - 1
    blocks = (counts + MOE_BLOCK - 1) // MOE_BLOCK
    blk_end = jnp.cumsum(blocks)
    blk_start = blk_end - blocks
    total = blk_end[-1]
    dest = blk_start[e_flat] * MOE_BLOCK + rank
    row_tok = jnp.zeros((n_items * MOE_BLOCK,), jnp.int32).at[dest].set(
        jnp.arange(n_assign, dtype=jnp.int32) // TOP_K)
    p = jnp.arange(n_items, dtype=jnp.int32)
    pc = jnp.minimum(p, total - 1)
    item_e = jnp.sum((pc[:, None] >= blk_end[None, :]).astype(jnp.int32), axis=1)
    rows_left = counts[item_e] - (pc - blk_start[item_e]) * MOE_BLOCK
    nact = jnp.clip((rows_left + MOE_SUB - 1) // MOE_SUB, 0, MOE_BLOCK // MOE_SUB)
    nact = jnp.where(p < total, nact, 0).astype(jnp.int32)
    out_blk = jnp.where(p < total, p, n_items).astype(jnp.int32)
    items = (item_e.astype(jnp.int32), pc.astype(jnp.int32), out_blk, nact)
    return dest.astype(jnp.int32), row_tok, items, (blocks, blk_start, blk_end, total)


def _moe_steps(items, runs, n_items, nj):
    _, _, _, nact = items
    blocks, blk_start, blk_end, total = runs
    s = jnp.arange(n_items * nj, dtype=jnp.int32)
    live = s < total * nj
    sc = jnp.minimum(s, total * nj - 1)
    e = jnp.sum((sc[:, None] >= (blk_end * nj)[None, :]).astype(jnp.int32), axis=1)
    t = sc - blk_start[e] * nj
    j = t // blocks[e]
    r = t % blocks[e]
    blk = blk_start[e] + jnp.where(j % 2 == 0, r, blocks[e] - 1 - r)
    out_blk = jnp.where(live, blk, n_items)
    out_j = jnp.where(live, j, 0)
    step_nact = jnp.where(live, nact[blk], 0)
    prev_e = jnp.concatenate([jnp.full((1,), -1, jnp.int32), e[:-1]])
    prev_j = jnp.concatenate([jnp.full((1,), -1, jnp.int32), j[:-1]])
    first = jnp.logical_and(live, jnp.logical_or(e != prev_e, j != prev_j))
    slot = (jnp.cumsum(first.astype(jnp.int32)) - 1) % 2
    ids = jnp.arange(N_EXPERTS, dtype=jnp.int32)
    later = jnp.logical_and(ids[None, :] > ids[:, None], blocks[None, :] > 0)
    next_expert = jnp.min(jnp.where(later, ids[None, :], N_EXPERTS), axis=1)
    wraps = j + 1 >= nj
    next_e = jnp.where(wraps, next_expert[e], e)
    next_j = jnp.where(wraps, 0, j + 1)
    has_next = jnp.logical_and(first, next_e < N_EXPERTS)
    next_e = jnp.minimum(next_e, N_EXPERTS - 1)
    return tuple(a.astype(jnp.int32) for a in (e, blk, j, out_blk, out_j, step_nact,
                                                first, slot, next_e, next_j, has_next))


def _gather_kernel(tok_ref, nact_ref, h_ref, o_ref, buf, sem):
    p = pl.program_id(0)
    last = pl.num_programs(0) - 1
    nact = nact_ref[p]
    slot = p % 2

    def copy(to_slot, r, t):
        return pltpu.make_async_copy(h_ref.at[pl.ds(t, 1)], buf.at[to_slot, pl.ds(r, 1)], sem.at[to_slot])

    def issue(item, to_slot):
        def body(grp, c):
            for u in range(GATHER_UNROLL):
                r = grp * GATHER_UNROLL + u
                copy(to_slot, r, tok_ref[item * MOE_BLOCK + r]).start()
            return c

        lax.fori_loop(0, nact_ref[item] * (MOE_SUB // GATHER_UNROLL), body, 0)

    def drain(grp, c):
        for u in range(GATHER_UNROLL):
            copy(slot, grp * GATHER_UNROLL + u, 0).wait()
        return c

    @pl.when(p == 0)
    def _():
        issue(0, 0)

    @pl.when(p < last)
    def _():
        issue(jnp.minimum(p + 1, last), 1 - slot)

    lax.fori_loop(0, nact * (MOE_SUB // GATHER_UNROLL), drain, 0)
    for s in range(MOE_BLOCK // MOE_SUB):
        rows = pl.ds(s * MOE_SUB, MOE_SUB)

        @pl.when(s < nact)
        def _():
            o_ref[rows] = buf[slot, rows].astype(BF16)

        @pl.when(s >= nact)
        def _():
            o_ref[rows] = jnp.zeros((MOE_SUB, o_ref.shape[1]), BF16)


def _moe_gather(h, row_tok, nact, n_items):
    d = h.shape[1]
    return pl.pallas_call(
        _gather_kernel,
        grid_spec=pltpu.PrefetchScalarGridSpec(
            num_scalar_prefetch=2,
            grid=(n_items,),
            in_specs=[pl.BlockSpec(memory_space=pl.ANY)],
            out_specs=pl.BlockSpec((MOE_BLOCK, d), lambda p, tok, na: (p, 0)),
            scratch_shapes=[pltpu.VMEM((2, MOE_BLOCK, d), F32), pltpu.SemaphoreType.DMA((2,))]),
        out_shape=jax.ShapeDtypeStruct((n_items * MOE_BLOCK, d), BF16),
        compiler_params=_cparams(1),
        name="moe_gather",
    )(row_tok, nact, h)


def _for_active_rows(nact, in_ref, o_ref, fn):
    n_sub = MOE_BLOCK // MOE_SUB
    for k in range(n_sub + 1):
        @pl.when(nact == k)
        def _():
            if k > 0:
                o_ref[0:k * MOE_SUB] = fn(in_ref[0:k * MOE_SUB])
            if k < n_sub:
                o_ref[k * MOE_SUB:] = jnp.zeros((MOE_BLOCK - k * MOE_SUB, o_ref.shape[1]), o_ref.dtype)


def _moe_glu_kernel(e_ref, blk_ref, j_ref, oblk_ref, oj_ref, nact_ref, first_ref, slot_ref, ne_ref, nj_ref,
                    more_ref, h_ref, wg_ref, wu_ref, o_ref, wbuf, sem):
    s = pl.program_id(0)
    nact = nact_ref[s]
    slot = slot_ref[s]
    tf = wbuf.shape[3]

    def tile_copies(e, j, to_slot):
        cols = pl.ds(pl.multiple_of(j * tf, tf), tf)
        return [pltpu.make_async_copy(w_ref.at[e, :, cols], wbuf.at[to_slot, i], sem.at[to_slot])
                for i, w_ref in enumerate((wg_ref, wu_ref))]

    @pl.when(s == 0)
    def _():
        for cp in tile_copies(e_ref[0], j_ref[0], 0):
            cp.start()

    @pl.when(first_ref[s] == 1)
    def _():
        for cp in tile_copies(e_ref[s], j_ref[s], slot):
            cp.wait()

        @pl.when(more_ref[s] == 1)
        def _():
            for cp in tile_copies(ne_ref[s], nj_ref[s], 1 - slot):
                cp.start()

    def run(h):
        gate = _dot(h, wbuf[slot, 0].astype(BF16))
        return (_silu(gate) * _dot(h, wbuf[slot, 1].astype(BF16))).astype(BF16)

    _for_active_rows(nact, h_ref, o_ref, run)


def _moe_glu(hs, wg, wu, steps, n_items, tf):
    d, f = wg.shape[1], wg.shape[2]
    nj = f // tf
    n_pre = len(steps)
    return pl.pallas_call(
        _moe_glu_kernel,
        grid_spec=pltpu.PrefetchScalarGridSpec(
            num_scalar_prefetch=n_pre,
            grid=(n_items * nj,),
            in_specs=[pl.BlockSpec((MOE_BLOCK, d), lambda s, *pre: (pre[1][s], 0)),
                      pl.BlockSpec(memory_space=pl.ANY),
                      pl.BlockSpec(memory_space=pl.ANY)],
            out_specs=pl.BlockSpec((MOE_BLOCK, tf), lambda s, *pre: (pre[3][s], pre[4][s])),
            scratch_shapes=[pltpu.VMEM((2, 2, d, tf), F32), pltpu.SemaphoreType.DMA((2,))]),
        out_shape=jax.ShapeDtypeStruct(((n_items + 1) * MOE_BLOCK, f), BF16),
        compiler_params=_cparams(1),
        name="moe_glu",
    )(*steps, hs, wg, wu)


def _moe_down_kernel(e_ref, blk_ref, oblk_ref, nact_ref, a_ref, wd_ref, o_ref):
    nact = nact_ref[pl.program_id(0)]
    _for_active_rows(nact, a_ref, o_ref, lambda a: _dot(a, wd_ref[0].astype(BF16)))


def _moe_down(a, wd, plan, n_items, tn):
    item_e, in_blk, out_blk, nact = plan
    f, d = wd.shape[1], wd.shape[2]
    nj = d // tn

    def w_map(p, j, e, b, ob, na):
        return (e[p], 0, jnp.where(na[p] > 0, j, nj - 1))

    def o_map(p, j, e, b, ob, na):
        return (ob[p], jnp.where(na[p] > 0, j, 0))

    return pl.pallas_call(
        _moe_down_kernel,
        grid_spec=pltpu.PrefetchScalarGridSpec(
            num_scalar_prefetch=4,
            grid=(n_items, nj),
            in_specs=[pl.BlockSpec((MOE_BLOCK, f), lambda p, j, e, b, ob, na: (b[p], 0)),
                      pl.BlockSpec((1, f, tn), w_map)],
            out_specs=pl.BlockSpec((MOE_BLOCK, tn), o_map)),
        out_shape=jax.ShapeDtypeStruct(((n_items + 1) * MOE_BLOCK, d), F32),
        compiler_params=_cparams(2),
        name="moe_down",
    )(item_e, in_blk, out_blk, nact, a, wd)


def _combine_kernel(dest_ref, x_ref, mod_ref, w_ref, fg_ref, y_ref, o_ref, buf, sem, *, tm):
    i = pl.program_id(0)
    last = pl.num_programs(0) - 1
    slot = i % 2

    def copy(to_slot, r, k, row):
        return pltpu.make_async_copy(y_ref.at[pl.ds(row, 1)], buf.at[to_slot, k, pl.ds(r, 1)],
                                     sem.at[to_slot])

    def issue(tile, to_slot):
        def body(grp, c):
            for u in range(GATHER_UNROLL):
                r = grp * GATHER_UNROLL + u
                for k in range(TOP_K):
                    copy(to_slot, r, k, dest_ref[(tile * tm + r) * TOP_K + k]).start()
            return c

        lax.fori_loop(0, tm // GATHER_UNROLL, body, 0)

    def drain(grp, c):
        for u in range(GATHER_UNROLL):
            for k in range(TOP_K):
                copy(slot, grp * GATHER_UNROLL + u, k, 0).wait()
        return c

    @pl.when(i == 0)
    def _():
        issue(0, 0)

    @pl.when(i < last)
    def _():
        issue(jnp.minimum(i + 1, last), 1 - slot)

    lax.fori_loop(0, tm // GATHER_UNROLL, drain, 0)
    w = w_ref[...]
    moe = w[:, 0:1] * buf[slot, 0] + w[:, 1:2] * buf[slot, 1]
    m = mod_ref[0]
    o_ref[...] = _rms(x_ref[...] + m[5:6] * moe, fg_ref[...])


def _moe_combine(dest, x, mod, top_w, final_g, ys, dims, tm):
    lat, d = x.shape
    seq_tiles = dims["seq"] // tm
    return pl.pallas_call(
        functools.partial(_combine_kernel, tm=tm),
        grid_spec=pltpu.PrefetchScalarGridSpec(
            num_scalar_prefetch=1,
            grid=(lat // tm,),
            in_specs=[pl.BlockSpec((tm, d), lambda i, dr: (i, 0)),
                      pl.BlockSpec((1, 6, d), lambda i, dr: (i // seq_tiles, 0, 0)),
                      pl.BlockSpec((tm, LANES), lambda i, dr: (i, 0)),
                      pl.BlockSpec((1, d), lambda i, dr: (0, 0)),
                      pl.BlockSpec(memory_space=pl.ANY)],
            out_specs=pl.BlockSpec((tm, d), lambda i, dr: (i, 0)),
            scratch_shapes=[pltpu.VMEM((2, TOP_K, tm, d), F32), pltpu.SemaphoreType.DMA((2,))]),
        out_shape=jax.ShapeDtypeStruct((lat, d), F32),
        compiler_params=_cparams(1),
        name="moe_combine",
    )(dest, x, mod, top_w, final_g.reshape(1, d), ys)


def kernel(x, c, ctx, c_ctx, ada_w, ada_b, norm1_g, norm2_g, ev_w_in, ev_w_out, ev_lambda, ev_subln_g, od_w_dq, od_q_norm_g, od_w_uq, od_w_dkv, od_kv_norm_g, od_w_ukv, od_w_o, ffn_w_gate, ffn_w_up, ffn_w_down, moe_router, moe_w_gate, moe_w_up, moe_w_down, final_norm_g):
    batch, seq, d = x.shape
    n_ctx = ctx.shape[1]
    depth = ada_w.shape[0]
    assert depth == 2 and batch < MOD_ROWS and seq % GRID_W == 0
    lat = batch * seq
    dims = dict(batch=batch, seq=seq, ctx=n_ctx, lat=lat)
    tm = 1024
    assert seq % tm == 0 and (batch * n_ctx) % tm == 0

    cond = jnp.concatenate([c, c_ctx[None, :], jnp.zeros((MOD_ROWS - batch - 1, d), F32)], axis=0)
    mod = _ada(cond, ada_w, ada_b).reshape(depth, MOD_ROWS, 6, d)
    x_lat, x_ctx = x.reshape(lat, d), ctx.reshape(batch * n_ctx, d)
    m_rows = lat + batch * n_ctx

    lam_init = 0.8 - 0.6 * math.exp(-0.3 * 0)
    fw = ev_w_in.shape[2] // 4
    tabs_ev = jnp.asarray(np.stack([
        _rope_tables(seq, tm, DIFF_HEAD_DIM // 4, DIFF_HEAD_DIM ** -0.5 * LOG2E),
        _rope_tables(seq, tm, DIFF_HEAD_DIM // 4, 1.0)]))
    w_in = ev_w_in[0].astype(BF16)
    qkvf = _inproj(x_lat, None, norm1_g[0], mod[0], w_in, tabs_ev, dims, tm, 1024)
    qkvf = _inproj(x_ctx, qkvf, norm1_g[0], mod[0], w_in, tabs_ev, dims, tm, 1024)
    o_attn = _diff_attn(qkvf, None, ev_lambda[0], ev_subln_g[0], lam_init, dims, 512, True)
    o_attn = _diff_attn(qkvf, o_attn, ev_lambda[0], ev_subln_g[0], lam_init, dims, n_ctx, False)

    gw = fw // FOURIER_GROUPS
    cc_np, sc_np = _dft_cos_sin(gw)
    cc = jnp.asarray(cc_np.astype(np.float32)).astype(BF16)
    sc = jnp.asarray(sc_np.astype(np.float32)).astype(BF16)
    cn_np, sn_np = _dft_cos_sin(seq)
    cs_lat = jnp.asarray(np.concatenate([cn_np, -sn_np], axis=1).astype(np.float32)).astype(BF16)
    cx_np, sx_np = _dft_cos_sin(n_ctx)
    cs_ctx = jnp.asarray(np.concatenate([cx_np, -sx_np], axis=1).astype(np.float32)).astype(BF16)
    fm = _fourier(qkvf, None, seq, 0, batch, fw, cc, sc, cs_lat, "ev_fourier_lat")
    fm = _fourier(qkvf, fm, n_ctx, lat // n_ctx, batch, fw, cc, sc, cs_ctx, "ev_fourier_ctx")

    w_out = ev_w_out[0].astype(BF16)
    xs = _mm_res([fm, o_attn], w_out, x_lat, mod[0], 2, dims, lat, tm, 1024, "ev_outproj_lat", out_rows=m_rows)
    xs = _mm_res([fm, o_attn], w_out, x_ctx, mod[0], 2, dims, batch * n_ctx, tm, 1024, "ev_outproj_ctx", prev=xs)
    act = _glu(xs, norm2_g[0], mod[0], ffn_w_gate[0], ffn_w_up[0], dims, tm, 512)
    xs = _mm_res([act], ffn_w_down[0].astype(BF16), xs, mod[0], 5, dims, m_rows, tm, 512, "ffn_down")

    heads = od_w_ukv.shape[2] // (MLA_NOPE + MLA_V)
    q_lora = od_w_dq.shape[2]
    wd_cat = jnp.concatenate(
        [od_w_dq[0], od_w_dkv[0], jnp.zeros((d, LANES - MLA_ROPE), F32)], axis=1).astype(BF16)
    wuq = jnp.pad(od_w_uq[0].reshape(q_lora, heads, MLA_NOPE + MLA_ROPE),
                  ((0, 0), (0, 0), (0, 2 * LANES - MLA_NOPE - MLA_ROPE))).reshape(q_lora, heads * 2 * LANES)
    tm_mla = 256
    tab_q = jnp.asarray(_rope_tables(seq, tm_mla, MLA_ROPE // 4, (MLA_NOPE + MLA_ROPE) ** -0.5 * LOG2E))
    tab_k = jnp.asarray(_rope_tables(seq, tm_mla, MLA_ROPE // 4, 1.0))
    q, kv, kr = _mla_proj(xs, norm1_g[1], mod[1], wd_cat, od_q_norm_g[0], od_kv_norm_g[0],
                          wuq.astype(BF16), od_w_ukv[0].astype(BF16), tab_q, tab_k, dims, tm_mla)
    o_mla = _mla_attn(q, kv, kr, dims, 1024)
    xl = _mm_res([o_mla], od_w_o[0].astype(BF16), xs, mod[1], 2, dims, lat, tm, 1024, "od_outproj")

    router_pad = jnp.pad(moe_router[0], ((0, 0), (0, LANES - N_EXPERTS)))
    h2, top_idx, top_w = _router(xl, norm2_g[1], mod[1], router_pad, dims, 512)
    n_items = lat * TOP_K // MOE_BLOCK + N_EXPERTS
    dest, row_tok, items, runs = _moe_plan(top_idx[:, :TOP_K], n_items)
    tf = 512
    steps = _moe_steps(items, runs, n_items, moe_w_gate.shape[3] // tf)
    hs = _moe_gather(h2, row_tok, items[3], n_items)
    act = _moe_glu(hs, moe_w_gate[0], moe_w_up[0], steps, n_items, tf)
    ys = _moe_down(act, moe_w_down[0], items, n_items, 256)
    out = _moe_combine(dest, xl, mod[1], top_w, final_norm_g, ys, dims, 256)
    return out.reshape(batch, seq, d)
```

```python
import functools
import math

import numpy as np
import jax
import jax.numpy as jnp
from jax import lax
from jax.experimental import pallas as pl
from jax.experimental.pallas import tpu as pltpu

F32 = jnp.float32
BF16 = jnp.bfloat16

GRID_W = 64
NORM_EPS = 1e-6
ROPE_BASE = 10000.0
FOURIER_GROUPS = 4
DIFF_HEAD_DIM = 128
MLA_NOPE = 128
MLA_ROPE = 64
MLA_V = 128
N_EXPERTS = 8
TOP_K = 2

LANES = 128
MOD_ROWS = 8
VMEM_LIMIT = 56 * 1024 * 1024
MOE_BLOCK = 1024
MOE_SUB = 256
GATHER_UNROLL = 8
KEY_CHUNK = 256
LOG2E = math.log2(math.e)


def _cparams(n_axes):
    return pltpu.CompilerParams(dimension_semantics=("arbitrary",) * n_axes,
                                vmem_limit_bytes=VMEM_LIMIT)


def _rms(x, g):
    return x * lax.rsqrt(jnp.mean(x * x, axis=-1, keepdims=True) + NORM_EPS) * g


def _norm_mod(x, g, shift, scale):
    return _rms(x, g) * (1.0 + scale) + shift


def _silu(x):
    return x * (1.0 / (1.0 + jnp.exp(-x)))


def _dot(a, b):
    return jnp.dot(a, b, preferred_element_type=F32)


def _dot_nt(a, b):
    return lax.dot_general(a, b, (((1,), (1,)), ((), ())), preferred_element_type=F32)


def _rope_tables(seq, extra_rows, chunk, scale):
    n = np.arange(seq)
    row, col = n // GRID_W, n % GRID_W
    lane = np.arange(LANES)
    a = 2 * chunk
    inv = ROPE_BASE ** (-np.arange(0, a, 2, dtype=np.float64) / a)
    used = lane < 4 * chunk
    freq = inv[lane % chunk]
    pos = np.where(lane[None, :] < 2 * chunk, row[:, None], col[:, None]).astype(np.float64)
    ang = pos * freq[None, :]
    first = (lane // chunk) % 2 == 0
    cos = np.where(used[None, :], np.cos(ang), 0.0)
    sin = np.where(used[None, :], np.sin(ang), 0.0)
    s1 = np.where(first[None, :], -sin, 0.0)
    s2 = np.where(first[None, :], 0.0, sin)
    ident = np.zeros((3, extra_rows, LANES))
    ident[0] = used[None, :].astype(np.float64)
    tab = np.concatenate([np.stack([cos, s1, s2]), ident], axis=1) * scale
    return tab.astype(np.float32)


def _apply_rope(x, tab_ref, chunk):
    return (x * tab_ref[0] + pltpu.roll(x, LANES - chunk, 1) * tab_ref[1]
            + pltpu.roll(x, chunk, 1) * tab_ref[2])


def _dft_cos_sin(n):
    k = np.arange(n)
    ang = 2.0 * np.pi * ((k[:, None] * k[None, :]) % n) / n
    return np.cos(ang), np.sin(ang)


def _ada_kernel(s_ref, w_ref, b_ref, o_ref):
    s = _silu(s_ref[...]).astype(BF16)
    o_ref[0] = _dot(s, w_ref[0].astype(BF16)) + b_ref[0]


def _ada(cond, ada_w, ada_b):
    depth, d, n = ada_w.shape
    tn = 1024
    return pl.pallas_call(
        _ada_kernel,
        grid=(depth, n // tn),
        in_specs=[pl.BlockSpec((MOD_ROWS, d), lambda i, j: (0, 0)),
                  pl.BlockSpec((1, d, tn), lambda i, j: (i, 0, j)),
                  pl.BlockSpec((1, 1, tn), lambda i, j: (i, 0, j))],
        out_specs=pl.BlockSpec((1, MOD_ROWS, tn), lambda i, j: (i, 0, j)),
        out_shape=jax.ShapeDtypeStruct((depth, MOD_ROWS, n), F32),
        compiler_params=_cparams(2),
        name="ada",
    )(cond, ada_w, ada_b.reshape(depth, 1, n))


def _inproj_kernel(x_ref, g_ref, mod_ref, w_ref, tab_ref, *rest):
    o_ref, h_ref = rest[-2:]
    j = pl.program_id(1)
    quarter = pl.num_programs(1) // 4

    @pl.when(j == 0)
    def _():
        m = mod_ref[0]
        h_ref[...] = _norm_mod(x_ref[...], g_ref[...], m[0:1], m[1:2]).astype(BF16)

    is_rope = jnp.logical_and(j >= quarter, j < 3 * quarter)

    @pl.when(is_rope)
    def _():
        res = _dot(h_ref[...], w_ref[...].astype(BF16))
        for c in range(res.shape[1] // LANES):
            sl = slice(c * LANES, (c + 1) * LANES)
            o_ref[:, sl] = _apply_rope(res[:, sl], tab_ref.at[0], DIFF_HEAD_DIM // 4).astype(BF16)

    @pl.when(jnp.logical_not(is_rope))
    def _():
        o_ref[...] = _dot(h_ref[...], w_ref[...].astype(BF16)).astype(BF16)


def _inproj(x, prev, g, mod, w, tabs, dims, tm, tn):
    rows, d = x.shape
    seq, lat, batch = dims["seq"], dims["lat"], dims["batch"]
    m_rows = lat + batch * dims["ctx"]
    n = w.shape[1]
    nj = n // tn
    assert nj % 4 == 0
    lat_tiles, seq_tiles = lat // tm, seq // tm
    is_ctx = prev is not None
    row_blk0 = lat_tiles if is_ctx else 0

    def tab_map(i, j):
        return (jnp.where(j >= nj // 2, 1, 0), 0, seq_tiles if is_ctx else i % seq_tiles, 0)

    in_specs = [pl.BlockSpec((tm, d), lambda i, j: (i, 0)),
                pl.BlockSpec((1, d), lambda i, j: (0, 0)),
                pl.BlockSpec((1, 6, d), lambda i, j: (batch if is_ctx else i // seq_tiles, 0, 0)),
                pl.BlockSpec((d, tn), lambda i, j: (0, j)),
                pl.BlockSpec((1, 3, tm, LANES), tab_map)]
    args = [x, g.reshape(1, d), mod, w, tabs]
    aliases = {}
    if is_ctx:
        in_specs.append(pl.BlockSpec(memory_space=pl.ANY))
        args.append(prev)
        aliases = {len(args) - 1: 0}
    return pl.pallas_call(
        _inproj_kernel,
        grid=(rows // tm, nj),
        in_specs=in_specs,
        out_specs=pl.BlockSpec((tm, tn), lambda i, j: (row_blk0 + i, j)),
        out_shape=jax.ShapeDtypeStruct((m_rows, n), BF16),
        scratch_shapes=[pltpu.VMEM((tm, d), BF16)],
        input_output_aliases=aliases,
        compiler_params=_cparams(2),
        name="ev_inproj_ctx" if is_ctx else "ev_inproj_lat",
    )(*args)


def _softmax_numerators(qs, k_ref, kcols, s_scr, p_scr, want_sum):
    n_keys = k_ref.shape[0]
    chunks = [slice(c, c + KEY_CHUNK) for c in range(0, n_keys, KEY_CHUNK)]
    maxes = []
    for i, (q, cols) in enumerate(zip(qs, kcols)):
        m = None
        for ks in chunks:
            s = _dot_nt(q, k_ref[ks, cols])
            s_scr[i, :, ks] = s
            mc = jnp.max(s, axis=-1, keepdims=True)
            m = mc if m is None else jnp.maximum(m, mc)
        maxes.append(m)
    totals = []
    for i, m in enumerate(maxes):
        total = None
        for ks in chunks:
            e = jnp.exp2(s_scr[i, :, ks] - m)
            if want_sum:
                part = jnp.sum(e, axis=-1, keepdims=True)
                total = part if total is None else total + part
            p_scr[i, :, ks] = e.astype(BF16)
        totals.append(total)
    return totals


def _diff_attn_kernel(lam_ref, q_ref, *rest, lam_init, seg_rows):
    n_seg = len(seg_rows)
    k_refs, v_refs, g_ref = rest[:n_seg], rest[n_seg:2 * n_seg], rest[2 * n_seg]
    o_ref, k_scr, v_scr, s_scr, p_scr = rest[-5:]
    hd = DIFF_HEAD_DIM

    @pl.when(pl.program_id(2) == 0)
    def _():
        r0 = 0
        for k_ref, v_ref, n in zip(k_refs, v_refs, seg_rows):
            k_scr[r0:r0 + n] = k_ref[...]
            v_scr[r0:r0 + n] = v_ref[...]
            r0 += n

    lv = lam_ref[...]
    lam = (jnp.exp(jnp.sum(lv[0:1] * lv[1:2], axis=-1, keepdims=True))
           - jnp.exp(jnp.sum(lv[2:3] * lv[3:4], axis=-1, keepdims=True)) + lam_init)
    q = q_ref[...]
    cols = [slice(c * hd, (c + 1) * hd) for c in range(2)]
    totals = _softmax_numerators([q[:, c] for c in cols], k_scr, cols, s_scr, p_scr, True)
    outs = [_dot(p_scr[c], v_scr[...]) * (1.0 / totals[c]) for c in range(2)]
    o = outs[0] - lam * outs[1]
    o_ref[...] = (_rms(o, g_ref[...]) * (1.0 - lam_init)).astype(BF16)


def _diff_attn(qkvf, prev, lam_vec, subln_g, lam_init, dims, tq, latent):
    m_rows = qkvf.shape[0]
    batch, seq, ctx = dims["batch"], dims["seq"], dims["ctx"]
    hw = 2 * DIFF_HEAD_DIM
    width = qkvf.shape[1] // 4
    heads = width // hw
    ctx_blk0 = batch * seq // ctx
    qcol, kcol, vcol = width // hw, 2 * width // hw, 3 * width // hw
    if latent:
        nq, q_blk0, seg_rows = seq // tq, 0, (ctx, seq)
    else:
        assert tq == ctx
        nq, q_blk0, seg_rows = 1, ctx_blk0, (ctx,)
    n_keys = sum(seg_rows)

    def kv_specs(col):
        specs = [pl.BlockSpec((ctx, hw), lambda b, h, i: (ctx_blk0 + b, col + h))]
        if latent:
            specs.append(pl.BlockSpec((seq, hw), lambda b, h, i: (b, col + h)))
        return specs

    in_specs = ([pl.BlockSpec((4, DIFF_HEAD_DIM), lambda b, h, i: (0, 0)),
                 pl.BlockSpec((tq, hw), lambda b, h, i: (q_blk0 + b * nq + i, qcol + h))]
                + kv_specs(kcol) + kv_specs(vcol)
                + [pl.BlockSpec((1, hw), lambda b, h, i: (0, 0))])
    args = [lam_vec, qkvf] + [qkvf] * (2 * len(seg_rows)) + [subln_g.reshape(1, hw)]
    aliases = {}
    if prev is not None:
        in_specs.append(pl.BlockSpec(memory_space=pl.ANY))
        args.append(prev)
        aliases = {len(args) - 1: 0}
    kern = functools.partial(_diff_attn_kernel, lam_init=lam_init, seg_rows=seg_rows)
    return pl.pallas_call(
        kern,
        grid=(batch, heads, nq),
        in_specs=in_specs,
        out_specs=pl.BlockSpec((tq, hw), lambda b, h, i: (q_blk0 + b * nq + i, h)),
        out_shape=jax.ShapeDtypeStruct((m_rows, width), BF16),
        scratch_shapes=[pltpu.VMEM((n_keys, hw), BF16), pltpu.VMEM((n_keys, hw), BF16),
                        pltpu.VMEM((2, tq, n_keys), F32), pltpu.VMEM((2, tq, n_keys), BF16)],
        input_output_aliases=aliases,
        compiler_params=_cparams(3),
        name="ev_diff_attn_lat" if latent else "ev_diff_attn_ctx",
    )(*args)


def _fourier_kernel(u_ref, cc_ref, sc_ref, cs_ref, *rest, n, norm):
    o_ref, ab_ref = rest[-2], rest[-1]
    u = u_ref[...]
    ab_ref[0:n] = _dot(u, cc_ref[...]).astype(BF16)
    ab_ref[n:] = _dot(u, sc_ref[...]).astype(BF16)
    o_ref[...] = (_dot(cs_ref[...], ab_ref[...]) * norm).astype(BF16)


def _fourier(qkvf, prev, n, row_blk0, batch, width, cc, sc, cs, name):
    m_rows = qkvf.shape[0]
    gw = width // FOURIER_GROUPS
    kern = functools.partial(_fourier_kernel, n=n, norm=1.0 / math.sqrt(n * gw))
    in_specs = [pl.BlockSpec((n, gw), lambda b, g: (row_blk0 + b, g)),
                pl.BlockSpec((gw, gw), lambda b, g: (0, 0)),
                pl.BlockSpec((gw, gw), lambda b, g: (0, 0)),
                pl.BlockSpec((n, 2 * n), lambda b, g: (0, 0), pipeline_mode=pl.Buffered(1))]
    args = [qkvf, cc, sc, cs]
    aliases = {}
    if prev is not None:
        in_specs.append(pl.BlockSpec(memory_space=pl.ANY))
        args.append(prev)
        aliases = {4: 0}
    return pl.pallas_call(
        kern,
        grid=(batch, FOURIER_GROUPS),
        in_specs=in_specs,
        out_specs=pl.BlockSpec((n, gw), lambda b, g: (row_blk0 + b, g)),
        out_shape=jax.ShapeDtypeStruct((m_rows, width), BF16),
        scratch_shapes=[pltpu.VMEM((2 * n, gw), BF16)],
        input_output_aliases=aliases,
        compiler_params=_cparams(2),
        name=name,
    )(*args)


def _mm_res_kernel(*refs, n_a, gate_idx):
    a_refs, w_refs = refs[:n_a], refs[n_a:2 * n_a]
    x_ref, mod_ref, o_ref = refs[2 * n_a], refs[2 * n_a + 1], refs[-1]
    acc = _dot(a_refs[0][...], w_refs[0][...].astype(BF16))
    for a_ref, w_ref in zip(a_refs[1:], w_refs[1:]):
        acc = acc + _dot(a_ref[...], w_ref[...].astype(BF16))
    m = mod_ref[0]
    o_ref[...] = x_ref[...] + m[gate_idx:gate_idx + 1] * acc


def _mm_res(a_list, w, x, mod, gate_idx, dims, rows, tm, tn, name, out_rows=None, prev=None):
    d = w.shape[1]
    seq_tiles = dims["seq"] // tm
    n_a = len(a_list)
    ctx_only = prev is not None
    row_blk0 = dims["lat"] // tm if ctx_only else 0
    out_rows = prev.shape[0] if ctx_only else (out_rows or rows)

    def sel(i):
        return dims["batch"] if ctx_only else jnp.minimum(i // seq_tiles, dims["batch"])

    in_specs, w_args, k0 = [], [], 0
    for a in a_list:
        in_specs.append(pl.BlockSpec((tm, a.shape[1]), lambda i, j: (row_blk0 + i, 0)))
    for a in a_list:
        ka = a.shape[1]
        assert k0 % ka == 0
        in_specs.append(pl.BlockSpec((ka, tn), lambda i, j, kb=k0 // ka: (kb, j)))
        w_args.append(w)
        k0 += ka
    assert k0 == w.shape[0]
    in_specs += [pl.BlockSpec((tm, tn), lambda i, j: (i, j)),
                 pl.BlockSpec((1, 6, tn), lambda i, j: (sel(i), 0, j))]
    args = [*a_list, *w_args, x, mod]
    aliases = {}
    if prev is not None:
        in_specs.append(pl.BlockSpec(memory_space=pl.ANY))
        args.append(prev)
        aliases = {len(args) - 1: 0}
    return pl.pallas_call(
        functools.partial(_mm_res_kernel, n_a=n_a, gate_idx=gate_idx),
        grid=(rows // tm, d // tn),
        in_specs=in_specs,
        out_specs=pl.BlockSpec((tm, tn), lambda i, j: (row_blk0 + i, j)),
        out_shape=jax.ShapeDtypeStruct((out_rows, d), F32),
        input_output_aliases=aliases,
        compiler_params=_cparams(2),
        name=name,
    )(*args)


def _glu_kernel(x_ref, g_ref, mod_ref, wg_ref, wu_ref, o_ref, h_ref):
    @pl.when(pl.program_id(1) == 0)
    def _():
        m = mod_ref[0]
        h_ref[...] = _norm_mod(x_ref[...], g_ref[...], m[3:4], m[4:5]).astype(BF16)

    h = h_ref[...]
    gate = _dot(h, wg_ref[...].astype(BF16))
    up = _dot(h, wu_ref[...].astype(BF16))
    o_ref[...] = (_silu(gate) * up).astype(BF16)


def _glu(x, g, mod, wg, wu, dims, tm, tf):
    m_rows, d = x.shape
    f = wg.shape[1]
    seq_tiles = dims["seq"] // tm

    def sel(i):
        return jnp.minimum(i // seq_tiles, dims["batch"])

    return pl.pallas_call(
        _glu_kernel,
        grid=(m_rows // tm, f // tf),
        in_specs=[pl.BlockSpec((tm, d), lambda i, j: (i, 0)),
                  pl.BlockSpec((1, d), lambda i, j: (0, 0)),
                  pl.BlockSpec((1, 6, d), lambda i, j: (sel(i), 0, 0)),
                  pl.BlockSpec((d, tf), lambda i, j: (0, j)),
                  pl.BlockSpec((d, tf), lambda i, j: (0, j))],
        out_specs=pl.BlockSpec((tm, tf), lambda i, j: (i, j)),
        out_shape=jax.ShapeDtypeStruct((m_rows, f), BF16),
        scratch_shapes=[pltpu.VMEM((tm, d), BF16)],
        compiler_params=_cparams(2),
        name="ffn_glu",
    )(x, g.reshape(1, d), mod, wg, wu)


def _mla_proj_kernel(x_ref, g_ref, mod_ref, wd_ref, qg_ref, kvg_ref, wuq_ref, wukv_ref,
                     tq_ref, tk_ref, q_ref, kv_ref, kr_ref, *, n_lat_tiles, q_lora, kv_lora, heads):
    i = pl.program_id(0)
    m = mod_ref[0]
    h = _norm_mod(x_ref[...], g_ref[...], m[0:1], m[1:2]).astype(BF16)
    t = _dot(h, wd_ref[...])
    ckv = _rms(t[:, q_lora:q_lora + kv_lora], kvg_ref[...]).astype(BF16)
    kv_ref[...] = _dot(ckv, wukv_ref[...]).astype(BF16)
    kr = t[:, q_lora + kv_lora:]
    kr_ref[...] = _apply_rope(kr, tk_ref, MLA_ROPE // 4).astype(BF16)

    @pl.when(i < n_lat_tiles)
    def _():
        cq = _rms(t[:, :q_lora], qg_ref[...]).astype(BF16)
        q = _dot(cq, wuq_ref[...])
        scale = (MLA_NOPE + MLA_ROPE) ** -0.5 * LOG2E
        for hh in range(heads):
            c0 = hh * 2 * LANES
            q_ref[:, c0:c0 + LANES] = (q[:, c0:c0 + LANES] * scale).astype(BF16)
            q_ref[:, c0 + LANES:c0 + 2 * LANES] = _apply_rope(
                q[:, c0 + LANES:c0 + 2 * LANES], tq_ref, MLA_ROPE // 4).astype(BF16)


def _mla_proj(x, g, mod, wd, qg, kvg, wuq, wukv, tab_q, tab_k, dims, tm):
    m_rows, d = x.shape
    seq, lat = dims["seq"], dims["lat"]
    q_lora, kv_lora = qg.shape[0], kvg.shape[0]
    heads = wukv.shape[1] // (MLA_NOPE + MLA_V)
    lat_tiles, seq_tiles = lat // tm, seq // tm

    def sel(i):
        return jnp.minimum(i // seq_tiles, dims["batch"])

    def tab_map(i):
        return (0, jnp.where(i < lat_tiles, i % seq_tiles, seq_tiles), 0)

    kern = functools.partial(_mla_proj_kernel, n_lat_tiles=lat_tiles, q_lora=q_lora, kv_lora=kv_lora,
                             heads=heads)
    const = lambda i: (0, 0)
    return pl.pallas_call(
        kern,
        grid=(m_rows // tm,),
        in_specs=[pl.BlockSpec((tm, d), lambda i: (i, 0)),
                  pl.BlockSpec((1, d), const),
                  pl.BlockSpec((1, 6, d), lambda i: (sel(i), 0, 0)),
                  pl.BlockSpec(wd.shape, const),
                  pl.BlockSpec((1, q_lora), const),
                  pl.BlockSpec((1, kv_lora), const),
                  pl.BlockSpec(wuq.shape, const),
                  pl.BlockSpec(wukv.shape, const),
                  pl.BlockSpec((3, tm, LANES), tab_map),
                  pl.BlockSpec((3, tm, LANES), tab_map)],
        out_specs=[pl.BlockSpec((tm, wuq.shape[1]), lambda i: (jnp.minimum(i, lat_tiles - 1), 0)),
                   pl.BlockSpec((tm, wukv.shape[1]), lambda i: (i, 0)),
                   pl.BlockSpec((tm, LANES), lambda i: (i, 0))],
        out_shape=[jax.ShapeDtypeStruct((lat, wuq.shape[1]), BF16),
                   jax.ShapeDtypeStruct((m_rows, wukv.shape[1]), BF16),
                   jax.ShapeDtypeStruct((m_rows, LANES), BF16)],
        compiler_params=_cparams(1),
        name="od_mla_proj",
    )(x, g.reshape(1, d), mod, wd, qg.reshape(1, q_lora), kvg.reshape(1, kv_lora), wuq, wukv, tab_q, tab_k)


def _mla_attn_kernel(q_ref, knc_ref, knl_ref, krc_ref, krl_ref, vc_ref, vl_ref, o_ref,
                     k_scr, v_scr, s_scr, p_scr, *, n_ctx):
    @pl.when(pl.program_id(2) == 0)
    def _():
        k_scr[0:n_ctx, 0:LANES] = knc_ref[...]
        k_scr[n_ctx:, 0:LANES] = knl_ref[...]
        k_scr[0:n_ctx, LANES:] = krc_ref[...]
        k_scr[n_ctx:, LANES:] = krl_ref[...]
        v_scr[0:n_ctx, 0:LANES] = vc_ref[...]
        v_scr[n_ctx:, 0:LANES] = vl_ref[...]
        v_scr[:, LANES:] = jnp.ones((v_scr.shape[0], LANES), BF16)

    half = q_ref.shape[0] // 2
    halves = [slice(0, half), slice(half, 2 * half)]
    _softmax_numerators([q_ref[r] for r in halves], k_scr, [slice(None)] * 2, s_scr, p_scr, False)
    for i, r in enumerate(halves):
        acc = _dot(p_scr[i], v_scr[...])
        o_ref[r] = (acc[:, :LANES] / acc[:, LANES:LANES + 1]).astype(BF16)


def _mla_attn(q, kv, kr, dims, tq):
    batch, seq, ctx, lat = dims["batch"], dims["seq"], dims["ctx"], dims["lat"]
    heads = q.shape[1] // (2 * LANES)
    nq = seq // tq
    ctx_blk0 = lat // ctx
    return pl.pallas_call(
        functools.partial(_mla_attn_kernel, n_ctx=ctx),
        grid=(batch, heads, nq),
        in_specs=[pl.BlockSpec((tq, 2 * LANES), lambda b, h, i: (b * nq + i, h)),
                  pl.BlockSpec((ctx, LANES), lambda b, h, i: (ctx_blk0 + b, 2 * h)),
                  pl.BlockSpec((seq, LANES), lambda b, h, i: (b, 2 * h)),
                  pl.BlockSpec((ctx, LANES), lambda b, h, i: (ctx_blk0 + b, 0)),
                  pl.BlockSpec((seq, LANES), lambda b, h, i: (b, 0)),
                  pl.BlockSpec((ctx, LANES), lambda b, h, i: (ctx_blk0 + b, 2 * h + 1)),
                  pl.BlockSpec((seq, LANES), lambda b, h, i: (b, 2 * h + 1))],
        out_specs=pl.BlockSpec((tq, LANES), lambda b, h, i: (b * nq + i, h)),
        out_shape=jax.ShapeDtypeStruct((lat, heads * LANES), BF16),
        scratch_shapes=[pltpu.VMEM((ctx + seq, 2 * LANES), BF16), pltpu.VMEM((ctx + seq, 2 * LANES), BF16),
                        pltpu.VMEM((2, tq // 2, ctx + seq), F32), pltpu.VMEM((2, tq // 2, ctx + seq), BF16)],
        compiler_params=_cparams(3),
        name="od_mla_attn",
    )(q, kv, kv, kr, kr, kv, kv)


def _router_kernel(x_ref, g_ref, mod_ref, r_ref, h_ref, idx_ref, w_ref):
    m = mod_ref[0]
    h = _norm_mod(x_ref[...], g_ref[...], m[3:4], m[4:5])
    h_ref[...] = h
    r = r_ref[...]
    h_hi = h.astype(BF16)
    h_lo = (h - h_hi.astype(F32)).astype(BF16)
    r_hi = r.astype(BF16)
    r_lo = (r - r_hi.astype(F32)).astype(BF16)
    logits = _dot(h_hi, r_hi) + (_dot(h_lo, r_hi) + _dot(h_hi, r_lo))
    lane = lax.broadcasted_iota(jnp.int32, logits.shape, 1)
    lane_f = lane.astype(F32)
    neg = jnp.float32(-jnp.inf)
    logits = jnp.where(lane < N_EXPERTS, logits, neg)
    m1 = jnp.max(logits, axis=-1, keepdims=True)
    i1 = jnp.min(jnp.where(logits == m1, lane_f, float(LANES)), axis=-1, keepdims=True)
    rest = jnp.where(lane_f == i1, neg, logits)
    m2 = jnp.max(rest, axis=-1, keepdims=True)
    i2 = jnp.min(jnp.where(rest == m2, lane_f, float(LANES)), axis=-1, keepdims=True)
    e2 = jnp.exp(m2 - m1)
    w1 = 1.0 / (1.0 + e2)
    w2 = e2 / (1.0 + e2)
    idx_ref[...] = jnp.where(lane == 0, i1, jnp.where(lane == 1, i2, 0.0)).astype(jnp.int32)
    w_ref[...] = jnp.where(lane == 0, w1, jnp.where(lane == 1, w2, 0.0))


def _router(x, g, mod, router_pad, dims, tm):
    lat, d = x.shape
    seq_tiles = dims["seq"] // tm
    return pl.pallas_call(
        _router_kernel,
        grid=(lat // tm,),
        in_specs=[pl.BlockSpec((tm, d), lambda i: (i, 0)),
                  pl.BlockSpec((1, d), lambda i: (0, 0)),
                  pl.BlockSpec((1, 6, d), lambda i: (i // seq_tiles, 0, 0)),
                  pl.BlockSpec((d, LANES), lambda i: (0, 0))],
        out_specs=[pl.BlockSpec((tm, d), lambda i: (i, 0)),
                   pl.BlockSpec((tm, LANES), lambda i: (i, 0)),
                   pl.BlockSpec((tm, LANES), lambda i: (i, 0))],
        out_shape=[jax.ShapeDtypeStruct((lat, d), F32),
                   jax.ShapeDtypeStruct((lat, LANES), jnp.int32),
                   jax.ShapeDtypeStruct((lat, LANES), F32)],
        compiler_params=_cparams(1),
        name="moe_router",
    )(x, g.reshape(1, d), mod, router_pad)


def _moe_plan(top_idx, n_items):
    e_flat = top_idx.reshape(-1)
    n_assign = e_flat.shape[0]
    onehot = (e_flat[:, None] == jnp.arange(N_EXPERTS, dtype=jnp.int32)[None, :]).astype(jnp.int32)
    csum = jnp.cumsum(onehot, axis=0)
    counts = csum[-1]
    rank = jnp.sum(csum * onehot, axis=1) - 1
    blocks = (counts + MOE_BLOCK - 1) // MOE_BLOCK
    blk_end = jnp.cumsum(blocks)
    blk_start = blk_end - blocks
    total = blk_end[-1]
    dest = blk_start[e_flat] * MOE_BLOCK + rank
    row_tok = jnp.zeros((n_items * MOE_BLOCK,), jnp.int32).at[dest].set(
        jnp.arange(n_assign, dtype=jnp.int32) // TOP_K)
    p = jnp.arange(n_items, dtype=jnp.int32)
    pc = jnp.minimum(p, total - 1)
    item_e = jnp.sum((pc[:, None] >= blk_end[None, :]).astype(jnp.int32), axis=1)
    rows_left = counts[item_e] - (pc - blk_start[item_e]) * MOE_BLOCK
    nact = jnp.clip((rows_left + MOE_SUB - 1) // MOE_SUB, 0, MOE_BLOCK // MOE_SUB)
    nact = jnp.where(p < total, nact, 0).astype(jnp.int32)
    out_blk = jnp.where(p < total, p, n_items).astype(jnp.int32)
    items = (item_e.astype(jnp.int32), pc.astype(jnp.int32), out_blk, nact)
    return dest.astype(jnp.int32), row_tok, items, (blocks, blk_start, blk_end, total)


def _moe_steps(items, runs, n_items, nj):
    _, _, _, nact = items
    blocks, blk_start, blk_end, total = runs
    s = jnp.arange(n_items * nj, dtype=jnp.int32)
    live = s < total * nj
    sc = jnp.minimum(s, total * nj - 1)
    e = jnp.sum((sc[:, None] >= (blk_end * nj)[None, :]).astype(jnp.int32), axis=1)
    t = sc - blk_start[e] * nj
    j = t // blocks[e]
    r = t % blocks[e]
    blk = blk_start[e] + jnp.where(j % 2 == 0, r, blocks[e] - 1 - r)
    out_blk = jnp.where(live, blk, n_items)
    out_j = jnp.where(live, j, 0)
    step_nact = jnp.where(live, nact[blk], 0)
    prev_e = jnp.concatenate([jnp.full((1,), -1, jnp.int32), e[:-1]])
    prev_j = jnp.concatenate([jnp.full((1,), -1, jnp.int32), j[:-1]])
    first = jnp.logical_and(live, jnp.logical_or(e != prev_e, j != prev_j))
    slot = (jnp.cumsum(first.astype(jnp.int32)) - 1) % 2
    ids = jnp.arange(N_EXPERTS, dtype=jnp.int32)
    later = jnp.logical_and(ids[None, :] > ids[:, None], blocks[None, :] > 0)
    next_expert = jnp.min(jnp.where(later, ids[None, :], N_EXPERTS), axis=1)
    wraps = j + 1 >= nj
    next_e = jnp.where(wraps, next_expert[e], e)
    next_j = jnp.where(wraps, 0, j + 1)
    has_next = jnp.logical_and(first, next_e < N_EXPERTS)
    next_e = jnp.minimum(next_e, N_EXPERTS - 1)
    return tuple(a.astype(jnp.int32) for a in (e, blk, j, out_blk, out_j, step_nact,
                                                first, slot, next_e, next_j, has_next))


def _gather_kernel(tok_ref, nact_ref, h_ref, o_ref, buf, sem):
    p = pl.program_id(0)
    last = pl.num_programs(0) - 1
    nact = nact_ref[p]
    slot = p % 2

    def copy(to_slot, r, t):
        return pltpu.make_async_copy(h_ref.at[pl.ds(t, 1)], buf.at[to_slot, pl.ds(r, 1)], sem.at[to_slot])

    def issue(item, to_slot):
        def body(grp, c):
            for u in range(GATHER_UNROLL):
                r = grp * GATHER_UNROLL + u
                copy(to_slot, r, tok_ref[item * MOE_BLOCK + r]).start()
            return c

        lax.fori_loop(0, nact_ref[item] * (MOE_SUB // GATHER_UNROLL), body, 0)

    def drain(grp, c):
        for u in range(GATHER_UNROLL):
            copy(slot, grp * GATHER_UNROLL + u, 0).wait()
        return c

    @pl.when(p == 0)
    def _():
        issue(0, 0)

    @pl.when(p < last)
    def _():
        issue(jnp.minimum(p + 1, last), 1 - slot)

    lax.fori_loop(0, nact * (MOE_SUB // GATHER_UNROLL), drain, 0)
    for s in range(MOE_BLOCK // MOE_SUB):
        rows = pl.ds(s * MOE_SUB, MOE_SUB)

        @pl.when(s < nact)
        def _():
            o_ref[rows] = buf[slot, rows].astype(BF16)

        @pl.when(s >= nact)
        def _():
            o_ref[rows] = jnp.zeros((MOE_SUB, o_ref.shape[1]), BF16)


def _moe_gather(h, row_tok, nact, n_items):
    d = h.shape[1]
    return pl.pallas_call(
        _gather_kernel,
        grid_spec=pltpu.PrefetchScalarGridSpec(
            num_scalar_prefetch=2,
            grid=(n_items,),
            in_specs=[pl.BlockSpec(memory_space=pl.ANY)],
            out_specs=pl.BlockSpec((MOE_BLOCK, d), lambda p, tok, na: (p, 0)),
            scratch_shapes=[pltpu.VMEM((2, MOE_BLOCK, d), F32), pltpu.SemaphoreType.DMA((2,))]),
        out_shape=jax.ShapeDtypeStruct((n_items * MOE_BLOCK, d), BF16),
        compiler_params=_cparams(1),
        name="moe_gather",
    )(row_tok, nact, h)


def _for_active_rows(nact, in_ref, o_ref, fn):
    n_sub = MOE_BLOCK // MOE_SUB
    for k in range(n_sub + 1):
        @pl.when(nact == k)
        def _():
            if k > 0:
                o_ref[0:k * MOE_SUB] = fn(in_ref[0:k * MOE_SUB])
            if k < n_sub:
                o_ref[k * MOE_SUB:] = jnp.zeros((MOE_BLOCK - k * MOE_SUB, o_ref.shape[1]), o_ref.dtype)


def _moe_glu_kernel(e_ref, blk_ref, j_ref, oblk_ref, oj_ref, nact_ref, first_ref, slot_ref, ne_ref, nj_ref,
                    more_ref, h_ref, wg_ref, wu_ref, o_ref, wbuf, sem):
    s = pl.program_id(0)
    nact = nact_ref[s]
    slot = slot_ref[s]
    tf = wbuf.shape[3]

    def tile_copies(e, j, to_slot):
        cols = pl.ds(pl.multiple_of(j * tf, tf), tf)
        return [pltpu.make_async_copy(w_ref.at[e, :, cols], wbuf.at[to_slot, i], sem.at[to_slot])
                for i, w_ref in enumerate((wg_ref, wu_ref))]

    @pl.when(s == 0)
    def _():
        for cp in tile_copies(e_ref[0], j_ref[0], 0):
            cp.start()

    @pl.when(first_ref[s] == 1)
    def _():
        for cp in tile_copies(e_ref[s], j_ref[s], slot):
            cp.wait()

        @pl.when(more_ref[s] == 1)
        def _():
            for cp in tile_copies(ne_ref[s], nj_ref[s], 1 - slot):
                cp.start()

    def run(h):
        gate = _dot(h, wbuf[slot, 0].astype(BF16))
        return (_silu(gate) * _dot(h, wbuf[slot, 1].astype(BF16))).astype(BF16)

    _for_active_rows(nact, h_ref, o_ref, run)


def _moe_glu(hs, wg, wu, steps, n_items, tf):
    d, f = wg.shape[1], wg.shape[2]
    nj = f // tf
    n_pre = len(steps)
    return pl.pallas_call(
        _moe_glu_kernel,
        grid_spec=pltpu.PrefetchScalarGridSpec(
            num_scalar_prefetch=n_pre,
            grid=(n_items * nj,),
            in_specs=[pl.BlockSpec((MOE_BLOCK, d), lambda s, *pre: (pre[1][s], 0)),
                      pl.BlockSpec(memory_space=pl.ANY),
                      pl.BlockSpec(memory_space=pl.ANY)],
            out_specs=pl.BlockSpec((MOE_BLOCK, tf), lambda s, *pre: (pre[3][s], pre[4][s])),
            scratch_shapes=[pltpu.VMEM((2, 2, d, tf), F32), pltpu.SemaphoreType.DMA((2,))]),
        out_shape=jax.ShapeDtypeStruct(((n_items + 1) * MOE_BLOCK, f), BF16),
        compiler_params=_cparams(1),
        name="moe_glu",
    )(*steps, hs, wg, wu)


def _moe_down_kernel(e_ref, blk_ref, oblk_ref, nact_ref, a_ref, wd_ref, o_ref):
    nact = nact_ref[pl.program_id(0)]
    _for_active_rows(nact, a_ref, o_ref, lambda a: _dot(a, wd_ref[0].astype(BF16)))


def _moe_down(a, wd, plan, n_items, tn):
    item_e, in_blk, out_blk, nact = plan
    f, d = wd.shape[1], wd.shape[2]
    nj = d // tn

    def w_map(p, j, e, b, ob, na):
        return (e[p], 0, jnp.where(na[p] > 0, j, nj - 1))

    def o_map(p, j, e, b, ob, na):
        return (ob[p], jnp.where(na[p] > 0, j, 0))

    return pl.pallas_call(
        _moe_down_kernel,
        grid_spec=pltpu.PrefetchScalarGridSpec(
            num_scalar_prefetch=4,
            grid=(n_items, nj),
            in_specs=[pl.BlockSpec((MOE_BLOCK, f), lambda p, j, e, b, ob, na: (b[p], 0)),
                      pl.BlockSpec((1, f, tn), w_map)],
            out_specs=pl.BlockSpec((MOE_BLOCK, tn), o_map)),
        out_shape=jax.ShapeDtypeStruct(((n_items + 1) * MOE_BLOCK, d), F32),
        compiler_params=_cparams(2),
        name="moe_down",
    )(item_e, in_blk, out_blk, nact, a, wd)


def _combine_kernel(dest_ref, x_ref, mod_ref, w_ref, fg_ref, y_ref, o_ref, buf, sem, *, tm):
    i = pl.program_id(0)
    last = pl.num_programs(0) - 1
    slot = i % 2

    def copy(to_slot, r, k, row):
        return pltpu.make_async_copy(y_ref.at[pl.ds(row, 1)], buf.at[to_slot, k, pl.ds(r, 1)],
                                     sem.at[to_slot])

    def issue(tile, to_slot):
        def body(grp, c):
            for u in range(GATHER_UNROLL):
                r = grp * GATHER_UNROLL + u
                for k in range(TOP_K):
                    copy(to_slot, r, k, dest_ref[(tile * tm + r) * TOP_K + k]).start()
            return c

        lax.fori_loop(0, tm // GATHER_UNROLL, body, 0)

    def drain(grp, c):
        for u in range(GATHER_UNROLL):
            for k in range(TOP_K):
                copy(slot, grp * GATHER_UNROLL + u, k, 0).wait()
        return c

    @pl.when(i == 0)
    def _():
        issue(0, 0)

    @pl.when(i < last)
    def _():
        issue(jnp.minimum(i + 1, last), 1 - slot)

    lax.fori_loop(0, tm // GATHER_UNROLL, drain, 0)
    w = w_ref[...]
    moe = w[:, 0:1] * buf[slot, 0] + w[:, 1:2] * buf[slot, 1]
    m = mod_ref[0]
    o_ref[...] = _rms(x_ref[...] + m[5:6] * moe, fg_ref[...])


def _moe_combine(dest, x, mod, top_w, final_g, ys, dims, tm):
    lat, d = x.shape
    seq_tiles = dims["seq"] // tm
    return pl.pallas_call(
        functools.partial(_combine_kernel, tm=tm),
        grid_spec=pltpu.PrefetchScalarGridSpec(
            num_scalar_prefetch=1,
            grid=(lat // tm,),
            in_specs=[pl.BlockSpec((tm, d), lambda i, dr: (i, 0)),
                      pl.BlockSpec((1, 6, d), lambda i, dr: (i // seq_tiles, 0, 0)),
                      pl.BlockSpec((tm, LANES), lambda i, dr: (i, 0)),
                      pl.BlockSpec((1, d), lambda i, dr: (0, 0)),
                      pl.BlockSpec(memory_space=pl.ANY)],
            out_specs=pl.BlockSpec((tm, d), lambda i, dr: (i, 0)),
            scratch_shapes=[pltpu.VMEM((2, TOP_K, tm, d), F32), pltpu.SemaphoreType.DMA((2,))]),
        out_shape=jax.ShapeDtypeStruct((lat, d), F32),
        compiler_params=_cparams(1),
        name="moe_combine",
    )(dest, x, mod, top_w, final_g.reshape(1, d), ys)


def kernel(x, c, ctx, c_ctx, ada_w, ada_b, norm1_g, norm2_g, ev_w_in, ev_w_out, ev_lambda, ev_subln_g, od_w_dq, od_q_norm_g, od_w_uq, od_w_dkv, od_kv_norm_g, od_w_ukv, od_w_o, ffn_w_gate, ffn_w_up, ffn_w_down, moe_router, moe_w_gate, moe_w_up, moe_w_down, final_norm_g):
    batch, seq, d = x.shape
    n_ctx = ctx.shape[1]
    depth = ada_w.shape[0]
    assert depth == 2 and batch < MOD_ROWS and seq % GRID_W == 0
    lat = batch * seq
    dims = dict(batch=batch, seq=seq, ctx=n_ctx, lat=lat)
    tm = 1024
    assert seq % tm == 0 and (batch * n_ctx) % tm == 0

    cond = jnp.concatenate([c, c_ctx[None, :], jnp.zeros((MOD_ROWS - batch - 1, d), F32)], axis=0)
    mod = _ada(cond, ada_w, ada_b).reshape(depth, MOD_ROWS, 6, d)
    x_lat, x_ctx = x.reshape(lat, d), ctx.reshape(batch * n_ctx, d)
    m_rows = lat + batch * n_ctx

    lam_init = 0.8 - 0.6 * math.exp(-0.3 * 0)
    fw = ev_w_in.shape[2] // 4
    tabs_ev = jnp.asarray(np.stack([
        _rope_tables(seq, tm, DIFF_HEAD_DIM // 4, DIFF_HEAD_DIM ** -0.5 * LOG2E),
        _rope_tables(seq, tm, DIFF_HEAD_DIM // 4, 1.0)]))
    w_in = ev_w_in[0].astype(BF16)
    qkvf = _inproj(x_lat, None, norm1_g[0], mod[0], w_in, tabs_ev, dims, tm, 1024)
    qkvf = _inproj(x_ctx, qkvf, norm1_g[0], mod[0], w_in, tabs_ev, dims, tm, 1024)
    o_attn = _diff_attn(qkvf, None, ev_lambda[0], ev_subln_g[0], lam_init, dims, 1024, True)
    o_attn = _diff_attn(qkvf, o_attn, ev_lambda[0], ev_subln_g[0], lam_init, dims, n_ctx, False)

    gw = fw // FOURIER_GROUPS
    cc_np, sc_np = _dft_cos_sin(gw)
    cc = jnp.asarray(cc_np.astype(np.float32)).astype(BF16)
    sc = jnp.asarray(sc_np.astype(np.float32)).astype(BF16)
    cn_np, sn_np = _dft_cos_sin(seq)
    cs_lat = jnp.asarray(np.concatenate([cn_np, -sn_np], axis=1).astype(np.float32)).astype(BF16)
    cx_np, sx_np = _dft_cos_sin(n_ctx)
    cs_ctx = jnp.asarray(np.concatenate([cx_np, -sx_np], axis=1).astype(np.float32)).astype(BF16)
    fm = _fourier(qkvf, None, seq, 0, batch, fw, cc, sc, cs_lat, "ev_fourier_lat")
    fm = _fourier(qkvf, fm, n_ctx, lat // n_ctx, batch, fw, cc, sc, cs_ctx, "ev_fourier_ctx")

    w_out = ev_w_out[0].astype(BF16)
    xs = _mm_res([fm, o_attn], w_out, x_lat, mod[0], 2, dims, lat, tm, 1024, "ev_outproj_lat", out_rows=m_rows)
    xs = _mm_res([fm, o_attn], w_out, x_ctx, mod[0], 2, dims, batch * n_ctx, tm, 1024, "ev_outproj_ctx", prev=xs)
    act = _glu(xs, norm2_g[0], mod[0], ffn_w_gate[0], ffn_w_up[0], dims, tm, 512)
    xs = _mm_res([act], ffn_w_down[0].astype(BF16), xs, mod[0], 5, dims, m_rows, tm, 512, "ffn_down")

    heads = od_w_ukv.shape[2] // (MLA_NOPE + MLA_V)
    q_lora = od_w_dq.shape[2]
    wd_cat = jnp.concatenate(
        [od_w_dq[0], od_w_dkv[0], jnp.zeros((d, LANES - MLA_ROPE), F32)], axis=1).astype(BF16)
    wuq = jnp.pad(od_w_uq[0].reshape(q_lora, heads, MLA_NOPE + MLA_ROPE),
                  ((0, 0), (0, 0), (0, 2 * LANES - MLA_NOPE - MLA_ROPE))).reshape(q_lora, heads * 2 * LANES)
    tm_mla = 256
    tab_q = jnp.asarray(_rope_tables(seq, tm_mla, MLA_ROPE // 4, (MLA_NOPE + MLA_ROPE) ** -0.5 * LOG2E))
    tab_k = jnp.asarray(_rope_tables(seq, tm_mla, MLA_ROPE // 4, 1.0))
    q, kv, kr = _mla_proj(xs, norm1_g[1], mod[1], wd_cat, od_q_norm_g[0], od_kv_norm_g[0],
                          wuq.astype(BF16), od_w_ukv[0].astype(BF16), tab_q, tab_k, dims, tm_mla)
    o_mla = _mla_attn(q, kv, kr, dims, 2048)
    xl = _mm_res([o_mla], od_w_o[0].astype(BF16), xs, mod[1], 2, dims, lat, tm, 1024, "od_outproj")

    router_pad = jnp.pad(moe_router[0], ((0, 0), (0, LANES - N_EXPERTS)))
    h2, top_idx, top_w = _router(xl, norm2_g[1], mod[1], router_pad, dims, 512)
    n_items = lat * TOP_K // MOE_BLOCK + N_EXPERTS
    dest, row_tok, items, runs = _moe_plan(top_idx[:, :TOP_K], n_items)
    tf = 512
    steps = _moe_steps(items, runs, n_items, moe_w_gate.shape[3] // tf)
    hs = _moe_gather(h2, row_tok, items[3], n_items)
    act = _moe_glu(hs, moe_w_gate[0], moe_w_up[0], steps, n_items, tf)
    ys = _moe_down(act, moe_w_down[0], items, n_items, 256)
    out = _moe_combine(dest, xl, mod[1], top_w, final_norm_g, ys, dims, 256)
    return out.reshape(batch, seq, d)
```

```python
import functools
import math

import numpy as np
import jax
import jax.numpy as jnp
from jax import lax
from jax.experimental import pallas as pl
from jax.experimental.pallas import tpu as pltpu

F32 = jnp.float32
BF16 = jnp.bfloat16

GRID_W = 64
NORM_EPS = 1e-6
ROPE_BASE = 10000.0
FOURIER_GROUPS = 4
DIFF_HEAD_DIM = 128
MLA_NOPE = 128
MLA_ROPE = 64
MLA_V = 128
N_EXPERTS = 8
TOP_K = 2

LANES = 128
MOD_ROWS = 8
VMEM_LIMIT = 56 * 1024 * 1024
MOE_BLOCK = 1024
MOE_SUB = 128
GATHER_UNROLL = 8
KEY_CHUNK = 256
LOG2E = math.log2(math.e)


def _cparams(n_axes):
    return pltpu.CompilerParams(dimension_semantics=("arbitrary",) * n_axes,
                                vmem_limit_bytes=VMEM_LIMIT)


def _rms(x, g):
    return x * lax.rsqrt(jnp.mean(x * x, axis=-1, keepdims=True) + NORM_EPS) * g


def _norm_mod(x, g, shift, scale):
    return _rms(x, g) * (1.0 + scale) + shift


def _silu(x):
    return x * (1.0 / (1.0 + jnp.exp(-x)))


def _dot(a, b):
    return jnp.dot(a, b, preferred_element_type=F32)


def _dot_nt(a, b):
    return lax.dot_general(a, b, (((1,), (1,)), ((), ())), preferred_element_type=F32)


def _rope_tables(seq, extra_rows, chunk, scale):
    n = np.arange(seq)
    row, col = n // GRID_W, n % GRID_W
    lane = np.arange(LANES)
    a = 2 * chunk
    inv = ROPE_BASE ** (-np.arange(0, a, 2, dtype=np.float64) / a)
    used = lane < 4 * chunk
    freq = inv[lane % chunk]
    pos = np.where(lane[None, :] < 2 * chunk, row[:, None], col[:, None]).astype(np.float64)
    ang = pos * freq[None, :]
    first = (lane // chunk) % 2 == 0
    cos = np.where(used[None, :], np.cos(ang), 0.0)
    sin = np.where(used[None, :], np.sin(ang), 0.0)
    s1 = np.where(first[None, :], -sin, 0.0)
    s2 = np.where(first[None, :], 0.0, sin)
    ident = np.zeros((3, extra_rows, LANES))
    ident[0] = used[None, :].astype(np.float64)
    tab = np.concatenate([np.stack([cos, s1, s2]), ident], axis=1) * scale
    return tab.astype(np.float32)


def _apply_rope(x, tab_ref, chunk):
    return (x * tab_ref[0] + pltpu.roll(x, LANES - chunk, 1) * tab_ref[1]
            + pltpu.roll(x, chunk, 1) * tab_ref[2])


def _dft_cos_sin(n):
    k = np.arange(n)
    ang = 2.0 * np.pi * ((k[:, None] * k[None, :]) % n) / n
    return np.cos(ang), np.sin(ang)


def _ada_kernel(s_ref, w_ref, b_ref, o_ref):
    s = _silu(s_ref[...]).astype(BF16)
    o_ref[0] = _dot(s, w_ref[0].astype(BF16)) + b_ref[0]


def _ada(cond, ada_w, ada_b):
    depth, d, n = ada_w.shape
    tn = 1024
    return pl.pallas_call(
        _ada_kernel,
        grid=(depth, n // tn),
        in_specs=[pl.BlockSpec((MOD_ROWS, d), lambda i, j: (0, 0)),
                  pl.BlockSpec((1, d, tn), lambda i, j: (i, 0, j)),
                  pl.BlockSpec((1, 1, tn), lambda i, j: (i, 0, j))],
        out_specs=pl.BlockSpec((1, MOD_ROWS, tn), lambda i, j: (i, 0, j)),
        out_shape=jax.ShapeDtypeStruct((depth, MOD_ROWS, n), F32),
        compiler_params=_cparams(2),
        name="ada",
    )(cond, ada_w, ada_b.reshape(depth, 1, n))


def _inproj_kernel(x_ref, g_ref, mod_ref, w_ref, tab_ref, *rest):
    o_ref, h_ref = rest[-2:]
    j = pl.program_id(1)
    quarter = pl.num_programs(1) // 4

    @pl.when(j == 0)
    def _():
        m = mod_ref[0]
        h_ref[...] = _norm_mod(x_ref[...], g_ref[...], m[0:1], m[1:2]).astype(BF16)

    is_rope = jnp.logical_and(j >= quarter, j < 3 * quarter)

    @pl.when(is_rope)
    def _():
        res = _dot(h_ref[...], w_ref[...].astype(BF16))
        for c in range(res.shape[1] // LANES):
            sl = slice(c * LANES, (c + 1) * LANES)
            o_ref[:, sl] = _apply_rope(res[:, sl], tab_ref.at[0], DIFF_HEAD_DIM // 4).astype(BF16)

    @pl.when(jnp.logical_not(is_rope))
    def _():
        o_ref[...] = _dot(h_ref[...], w_ref[...].astype(BF16)).astype(BF16)


def _inproj(x, prev, g, mod, w, tabs, dims, tm, tn):
    rows, d = x.shape
    seq, lat, batch = dims["seq"], dims["lat"], dims["batch"]
    m_rows = lat + batch * dims["ctx"]
    n = w.shape[1]
    nj = n // tn
    assert nj % 4 == 0
    lat_tiles, seq_tiles = lat // tm, seq // tm
    is_ctx = prev is not None
    row_blk0 = lat_tiles if is_ctx else 0

    def tab_map(i, j):
        return (jnp.where(j >= nj // 2, 1, 0), 0, seq_tiles if is_ctx else i % seq_tiles, 0)

    in_specs = [pl.BlockSpec((tm, d), lambda i, j: (i, 0)),
                pl.BlockSpec((1, d), lambda i, j: (0, 0)),
                pl.BlockSpec((1, 6, d), lambda i, j: (batch if is_ctx else i // seq_tiles, 0, 0)),
                pl.BlockSpec((d, tn), lambda i, j: (0, j)),
                pl.BlockSpec((1, 3, tm, LANES), tab_map)]
    args = [x, g.reshape(1, d), mod, w, tabs]
    aliases = {}
    if is_ctx:
        in_specs.append(pl.BlockSpec(memory_space=pl.ANY))
        args.append(prev)
        aliases = {len(args) - 1: 0}
    return pl.pallas_call(
        _inproj_kernel,
        grid=(rows // tm, nj),
        in_specs=in_specs,
        out_specs=pl.BlockSpec((tm, tn), lambda i, j: (row_blk0 + i, j)),
        out_shape=jax.ShapeDtypeStruct((m_rows, n), BF16),
        scratch_shapes=[pltpu.VMEM((tm, d), BF16)],
        input_output_aliases=aliases,
        compiler_params=_cparams(2),
        name="ev_inproj_ctx" if is_ctx else "ev_inproj_lat",
    )(*args)


def _softmax_numerators(qs, k_ref, kcols, s_scr, p_scr, want_sum):
    n_keys = k_ref.shape[0]
    chunks = [slice(c, min(c + KEY_CHUNK, n_keys)) for c in range(0, n_keys, KEY_CHUNK)]
    maxes = []
    for i, (q, cols) in enumerate(zip(qs, kcols)):
        m = None
        for ks in chunks:
            s = _dot_nt(q, k_ref[ks, cols])
            s_scr[i, :, ks] = s
            mc = jnp.max(s, axis=-1, keepdims=True)
            m = mc if m is None else jnp.maximum(m, mc)
        maxes.append(m)
    totals = []
    for i, m in enumerate(maxes):
        total = None
        for ks in chunks:
            e = jnp.exp2(s_scr[i, :, ks] - m)
            if want_sum:
                part = jnp.sum(e, axis=-1, keepdims=True)
                total = part if total is None else total + part
            p_scr[i, :, ks] = e.astype(BF16)
        totals.append(total)
    return totals


def _diff_attn_kernel(lam_ref, q_ref, *rest, lam_init, seg_rows):
    n_seg = len(seg_rows)
    k_refs, v_refs, g_ref = rest[:n_seg], rest[n_seg:2 * n_seg], rest[2 * n_seg]
    o_ref, k_scr, v_scr, s_scr, p_scr = rest[-5:]
    hd = DIFF_HEAD_DIM

    @pl.when(pl.program_id(2) == 0)
    def _():
        r0 = 0
        for k_ref, v_ref, n in zip(k_refs, v_refs, seg_rows):
            k_scr[r0:r0 + n] = k_ref[...]
            v_scr[r0:r0 + n] = v_ref[...]
            r0 += n

    lv = lam_ref[...]
    lam = (jnp.exp(jnp.sum(lv[0:1] * lv[1:2], axis=-1, keepdims=True))
           - jnp.exp(jnp.sum(lv[2:3] * lv[3:4], axis=-1, keepdims=True)) + lam_init)
    q = q_ref[...]
    cols = [slice(c * hd, (c + 1) * hd) for c in range(2)]
    totals = _softmax_numerators([q[:, c] for c in cols], k_scr, cols, s_scr, p_scr, True)
    outs = [_dot(p_scr[c], v_scr[...]) * (1.0 / totals[c]) for c in range(2)]
    o = outs[0] - lam * outs[1]
    o_ref[...] = (_rms(o, g_ref[...]) * (1.0 - lam_init)).astype(BF16)


def _diff_attn(qkvf, prev, lam_vec, subln_g, lam_init, dims, tq, latent):
    m_rows = qkvf.shape[0]
    batch, seq, ctx = dims["batch"], dims["seq"], dims["ctx"]
    hw = 2 * DIFF_HEAD_DIM
    width = qkvf.shape[1] // 4
    heads = width // hw
    ctx_blk0 = batch * seq // ctx
    qcol, kcol, vcol = width // hw, 2 * width // hw, 3 * width // hw
    if latent:
        nq, q_blk0, seg_rows = seq // tq, 0, (ctx, seq)
    else:
        assert tq == ctx
        nq, q_blk0, seg_rows = 1, ctx_blk0, (ctx,)
    n_keys = sum(seg_rows)

    def kv_specs(col):
        specs = [pl.BlockSpec((ctx, hw), lambda b, h, i: (ctx_blk0 + b, col + h))]
        if latent:
            specs.append(pl.BlockSpec((seq, hw), lambda b, h, i: (b, col + h)))
        return specs

    in_specs = ([pl.BlockSpec((4, DIFF_HEAD_DIM), lambda b, h, i: (0, 0)),
                 pl.BlockSpec((tq, hw), lambda b, h, i: (q_blk0 + b * nq + i, qcol + h))]
                + kv_specs(kcol) + kv_specs(vcol)
                + [pl.BlockSpec((1, hw), lambda b, h, i: (0, 0))])
    args = [lam_vec, qkvf] + [qkvf] * (2 * len(seg_rows)) + [subln_g.reshape(1, hw)]
    aliases = {}
    if prev is not None:
        in_specs.append(pl.BlockSpec(memory_space=pl.ANY))
        args.append(prev)
        aliases = {len(args) - 1: 0}
    kern = functools.partial(_diff_attn_kernel, lam_init=lam_init, seg_rows=seg_rows)
    return pl.pallas_call(
        kern,
        grid=(batch, heads, nq),
        in_specs=in_specs,
        out_specs=pl.BlockSpec((tq, hw), lambda b, h, i: (q_blk0 + b * nq + i, h)),
        out_shape=jax.ShapeDtypeStruct((m_rows, width), BF16),
        scratch_shapes=[pltpu.VMEM((n_keys, hw), BF16), pltpu.VMEM((n_keys, hw), BF16),
                        pltpu.VMEM((2, tq, n_keys), F32), pltpu.VMEM((2, tq, n_keys), BF16)],
        input_output_aliases=aliases,
        compiler_params=_cparams(3),
        name="ev_diff_attn_lat" if latent else "ev_diff_attn_ctx",
    )(*args)


def _fourier_kernel(u_ref, cc_ref, sc_ref, cs_ref, *rest, n, norm):
    o_ref, ab_ref = rest[-2], rest[-1]
    u = u_ref[...]
    ab_ref[0:n] = _dot(u, cc_ref[...]).astype(BF16)
    ab_ref[n:] = _dot(u, sc_ref[...]).astype(BF16)
    o_ref[...] = (_dot(cs_ref[...], ab_ref[...]) * norm).astype(BF16)


def _fourier(qkvf, prev, n, row_blk0, batch, width, cc, sc, cs, name):
    m_rows = qkvf.shape[0]
    gw = width // FOURIER_GROUPS
    kern = functools.partial(_fourier_kernel, n=n, norm=1.0 / math.sqrt(n * gw))
    in_specs = [pl.BlockSpec((n, gw), lambda b, g: (row_blk0 + b, g)),
                pl.BlockSpec((gw, gw), lambda b, g: (0, 0)),
                pl.BlockSpec((gw, gw), lambda b, g: (0, 0)),
                pl.BlockSpec((n, 2 * n), lambda b, g: (0, 0), pipeline_mode=pl.Buffered(1))]
    args = [qkvf, cc, sc, cs]
    aliases = {}
    if prev is not None:
        in_specs.append(pl.BlockSpec(memory_space=pl.ANY))
        args.append(prev)
        aliases = {4: 0}
    return pl.pallas_call(
        kern,
        grid=(batch, FOURIER_GROUPS),
        in_specs=in_specs,
        out_specs=pl.BlockSpec((n, gw), lambda b, g: (row_blk0 + b, g)),
        out_shape=jax.ShapeDtypeStruct((m_rows, width), BF16),
        scratch_shapes=[pltpu.VMEM((2 * n, gw), BF16)],
        input_output_aliases=aliases,
        compiler_params=_cparams(2),
        name=name,
    )(*args)


def _mm_res_kernel(*refs, n_a, gate_idx):
    a_refs, w_refs = refs[:n_a], refs[n_a:2 * n_a]
    x_ref, mod_ref, o_ref = refs[2 * n_a], refs[2 * n_a + 1], refs[-1]
    acc = _dot(a_refs[0][...], w_refs[0][...].astype(BF16))
    for a_ref, w_ref in zip(a_refs[1:], w_refs[1:]):
        acc = acc + _dot(a_ref[...], w_ref[...].astype(BF16))
    m = mod_ref[0]
    o_ref[...] = x_ref[...] + m[gate_idx:gate_idx + 1] * acc


def _mm_res(a_list, w, x, mod, gate_idx, dims, rows, tm, tn, name, out_rows=None, prev=None):
    d = w.shape[1]
    seq_tiles = dims["seq"] // tm
    n_a = len(a_list)
    ctx_only = prev is not None
    row_blk0 = dims["lat"] // tm if ctx_only else 0
    out_rows = prev.shape[0] if ctx_only else (out_rows or rows)

    def sel(i):
        return dims["batch"] if ctx_only else jnp.minimum(i // seq_tiles, dims["batch"])

    in_specs, w_args, k0 = [], [], 0
    for a in a_list:
        in_specs.append(pl.BlockSpec((tm, a.shape[1]), lambda i, j: (row_blk0 + i, 0)))
    for a in a_list:
        ka = a.shape[1]
        assert k0 % ka == 0
        in_specs.append(pl.BlockSpec((ka, tn), lambda i, j, kb=k0 // ka: (kb, j)))
        w_args.append(w)
        k0 += ka
    assert k0 == w.shape[0]
    in_specs += [pl.BlockSpec((tm, tn), lambda i, j: (i, j)),
                 pl.BlockSpec((1, 6, tn), lambda i, j: (sel(i), 0, j))]
    args = [*a_list, *w_args, x, mod]
    aliases = {}
    if prev is not None:
        in_specs.append(pl.BlockSpec(memory_space=pl.ANY))
        args.append(prev)
        aliases = {len(args) - 1: 0}
    return pl.pallas_call(
        functools.partial(_mm_res_kernel, n_a=n_a, gate_idx=gate_idx),
        grid=(rows // tm, d // tn),
        in_specs=in_specs,
        out_specs=pl.BlockSpec((tm, tn), lambda i, j: (row_blk0 + i, j)),
        out_shape=jax.ShapeDtypeStruct((out_rows, d), F32),
        input_output_aliases=aliases,
        compiler_params=_cparams(2),
        name=name,
    )(*args)


def _glu_kernel(x_ref, g_ref, mod_ref, wg_ref, wu_ref, o_ref, h_ref):
    @pl.when(pl.program_id(1) == 0)
    def _():
        m = mod_ref[0]
        h_ref[...] = _norm_mod(x_ref[...], g_ref[...], m[3:4], m[4:5]).astype(BF16)

    h = h_ref[...]
    gate = _dot(h, wg_ref[...].astype(BF16))
    up = _dot(h, wu_ref[...].astype(BF16))
    o_ref[...] = (_silu(gate) * up).astype(BF16)


def _glu(x, g, mod, wg, wu, dims, tm, tf):
    m_rows, d = x.shape
    f = wg.shape[1]
    seq_tiles = dims["seq"] // tm

    def sel(i):
        return jnp.minimum(i // seq_tiles, dims["batch"])

    return pl.pallas_call(
        _glu_kernel,
        grid=(m_rows // tm, f // tf),
        in_specs=[pl.BlockSpec((tm, d), lambda i, j: (i, 0)),
                  pl.BlockSpec((1, d), lambda i, j: (0, 0)),
                  pl.BlockSpec((1, 6, d), lambda i, j: (sel(i), 0, 0)),
                  pl.BlockSpec((d, tf), lambda i, j: (0, j)),
                  pl.BlockSpec((d, tf), lambda i, j: (0, j))],
        out_specs=pl.BlockSpec((tm, tf), lambda i, j: (i, j)),
        out_shape=jax.ShapeDtypeStruct((m_rows, f), BF16),
        scratch_shapes=[pltpu.VMEM((tm, d), BF16)],
        compiler_params=_cparams(2),
        name="ffn_glu",
    )(x, g.reshape(1, d), mod, wg, wu)


def _mla_proj_kernel(x_ref, g_ref, mod_ref, wd_ref, qg_ref, kvg_ref, wuq_ref, wukv_ref,
                     tq_ref, tk_ref, q_ref, kv_ref, kr_ref, *, n_lat_tiles, q_lora, kv_lora, heads):
    i = pl.program_id(0)
    m = mod_ref[0]
    h = _norm_mod(x_ref[...], g_ref[...], m[0:1], m[1:2]).astype(BF16)
    t = _dot(h, wd_ref[...])
    ckv = _rms(t[:, q_lora:q_lora + kv_lora], kvg_ref[...]).astype(BF16)
    kv_ref[...] = _dot(ckv, wukv_ref[...]).astype(BF16)
    kr = t[:, q_lora + kv_lora:]
    kr_ref[...] = _apply_rope(kr, tk_ref, MLA_ROPE // 4).astype(BF16)

    @pl.when(i < n_lat_tiles)
    def _():
        cq = _rms(t[:, :q_lora], qg_ref[...]).astype(BF16)
        q = _dot(cq, wuq_ref[...])
        scale = (MLA_NOPE + MLA_ROPE) ** -0.5 * LOG2E
        for hh in range(heads):
            c0 = hh * 2 * LANES
            q_ref[:, c0:c0 + LANES] = (q[:, c0:c0 + LANES] * scale).astype(BF16)
            q_ref[:, c0 + LANES:c0 + 2 * LANES] = _apply_rope(
                q[:, c0 + LANES:c0 + 2 * LANES], tq_ref, MLA_ROPE // 4).astype(BF16)


def _mla_proj(x, g, mod, wd, qg, kvg, wuq, wukv, tab_q, tab_k, dims, tm):
    m_rows, d = x.shape
    seq, lat = dims["seq"], dims["lat"]
    q_lora, kv_lora = qg.shape[0], kvg.shape[0]
    heads = wukv.shape[1] // (MLA_NOPE + MLA_V)
    lat_tiles, seq_tiles = lat // tm, seq // tm

    def sel(i):
        return jnp.minimum(i // seq_tiles, dims["batch"])

    def tab_map(i):
        return (0, jnp.where(i < lat_tiles, i % seq_tiles, seq_tiles), 0)

    kern = functools.partial(_mla_proj_kernel, n_lat_tiles=lat_tiles, q_lora=q_lora, kv_lora=kv_lora,
                             heads=heads)
    const = lambda i: (0, 0)
    return pl.pallas_call(
        kern,
        grid=(m_rows // tm,),
        in_specs=[pl.BlockSpec((tm, d), lambda i: (i, 0)),
                  pl.BlockSpec((1, d), const),
                  pl.BlockSpec((1, 6, d), lambda i: (sel(i), 0, 0)),
                  pl.BlockSpec(wd.shape, const),
                  pl.BlockSpec((1, q_lora), const),
                  pl.BlockSpec((1, kv_lora), const),
                  pl.BlockSpec(wuq.shape, const),
                  pl.BlockSpec(wukv.shape, const),
                  pl.BlockSpec((3, tm, LANES), tab_map),
                  pl.BlockSpec((3, tm, LANES), tab_map)],
        out_specs=[pl.BlockSpec((tm, wuq.shape[1]), lambda i: (jnp.minimum(i, lat_tiles - 1), 0)),
                   pl.BlockSpec((tm, wukv.shape[1]), lambda i: (i, 0)),
                   pl.BlockSpec((tm, LANES), lambda i: (i, 0))],
        out_shape=[jax.ShapeDtypeStruct((lat, wuq.shape[1]), BF16),
                   jax.ShapeDtypeStruct((m_rows, wukv.shape[1]), BF16),
                   jax.ShapeDtypeStruct((m_rows, LANES), BF16)],
        compiler_params=_cparams(1),
        name="od_mla_proj",
    )(x, g.reshape(1, d), mod, wd, qg.reshape(1, q_lora), kvg.reshape(1, kv_lora), wuq, wukv, tab_q, tab_k)


def _mla_attn_kernel(q_ref, knc_ref, knl_ref, krc_ref, krl_ref, vc_ref, vl_ref, o_ref,
                     k_scr, v_scr, s_scr, p_scr, *, n_ctx):
    @pl.when(pl.program_id(2) == 0)
    def _():
        k_scr[0:n_ctx, 0:LANES] = knc_ref[...]
        k_scr[n_ctx:, 0:LANES] = knl_ref[...]
        k_scr[0:n_ctx, LANES:] = krc_ref[...]
        k_scr[n_ctx:, LANES:] = krl_ref[...]
        v_scr[0:n_ctx, 0:LANES] = vc_ref[...]
        v_scr[n_ctx:, 0:LANES] = vl_ref[...]
        v_scr[:, LANES:] = jnp.ones((v_scr.shape[0], LANES), BF16)

    half = q_ref.shape[0] // 2
    halves = [slice(0, half), slice(half, 2 * half)]
    _softmax_numerators([q_ref[r] for r in halves], k_scr, [slice(None)] * 2, s_scr, p_scr, False)
    for i, r in enumerate(halves):
        acc = _dot(p_scr[i], v_scr[...])
        o_ref[r] = (acc[:, :LANES] / acc[:, LANES:LANES + 1]).astype(BF16)


def _mla_attn(q, kv, kr, dims, tq):
    batch, seq, ctx, lat = dims["batch"], dims["seq"], dims["ctx"], dims["lat"]
    heads = q.shape[1] // (2 * LANES)
    nq = seq // tq
    ctx_blk0 = lat // ctx
    return pl.pallas_call(
        functools.partial(_mla_attn_kernel, n_ctx=ctx),
        grid=(batch, heads, nq),
        in_specs=[pl.BlockSpec((tq, 2 * LANES), lambda b, h, i: (b * nq + i, h)),
                  pl.BlockSpec((ctx, LANES), lambda b, h, i: (ctx_blk0 + b, 2 * h)),
                  pl.BlockSpec((seq, LANES), lambda b, h, i: (b, 2 * h)),
                  pl.BlockSpec((ctx, LANES), lambda b, h, i: (ctx_blk0 + b, 0)),
                  pl.BlockSpec((seq, LANES), lambda b, h, i: (b, 0)),
                  pl.BlockSpec((ctx, LANES), lambda b, h, i: (ctx_blk0 + b, 2 * h + 1)),
                  pl.BlockSpec((seq, LANES), lambda b, h, i: (b, 2 * h + 1))],
        out_specs=pl.BlockSpec((tq, LANES), lambda b, h, i: (b * nq + i, h)),
        out_shape=jax.ShapeDtypeStruct((lat, heads * LANES), BF16),
        scratch_shapes=[pltpu.VMEM((ctx + seq, 2 * LANES), BF16), pltpu.VMEM((ctx + seq, 2 * LANES), BF16),
                        pltpu.VMEM((2, tq // 2, ctx + seq), F32), pltpu.VMEM((2, tq // 2, ctx + seq), BF16)],
        compiler_params=_cparams(3),
        name="od_mla_attn",
    )(q, kv, kv, kr, kr, kv, kv)


def _router_kernel(x_ref, g_ref, mod_ref, r_ref, h_ref, idx_ref, w_ref):
    m = mod_ref[0]
    h = _norm_mod(x_ref[...], g_ref[...], m[3:4], m[4:5])
    h_ref[...] = h
    r = r_ref[...]
    h_hi = h.astype(BF16)
    h_lo = (h - h_hi.astype(F32)).astype(BF16)
    r_hi = r.astype(BF16)
    r_lo = (r - r_hi.astype(F32)).astype(BF16)
    logits = _dot(h_hi, r_hi) + (_dot(h_lo, r_hi) + _dot(h_hi, r_lo))
    lane = lax.broadcasted_iota(jnp.int32, logits.shape, 1)
    lane_f = lane.astype(F32)
    neg = jnp.float32(-jnp.inf)
    logits = jnp.where(lane < N_EXPERTS, logits, neg)
    m1 = jnp.max(logits, axis=-1, keepdims=True)
    i1 = jnp.min(jnp.where(logits == m1, lane_f, float(LANES)), axis=-1, keepdims=True)
    rest = jnp.where(lane_f == i1, neg, logits)
    m2 = jnp.max(rest, axis=-1, keepdims=True)
    i2 = jnp.min(jnp.where(rest == m2, lane_f, float(LANES)), axis=-1, keepdims=True)
    e2 = jnp.exp(m2 - m1)
    w1 = 1.0 / (1.0 + e2)
    w2 = e2 / (1.0 + e2)
    idx_ref[...] = jnp.where(lane == 0, i1, jnp.where(lane == 1, i2, 0.0)).astype(jnp.int32)
    w_ref[...] = jnp.where(lane == 0, w1, jnp.where(lane == 1, w2, 0.0))


def _router(x, g, mod, router_pad, dims, tm):
    lat, d = x.shape
    seq_tiles = dims["seq"] // tm
    return pl.pallas_call(
        _router_kernel,
        grid=(lat // tm,),
        in_specs=[pl.BlockSpec((tm, d), lambda i: (i, 0)),
                  pl.BlockSpec((1, d), lambda i: (0, 0)),
                  pl.BlockSpec((1, 6, d), lambda i: (i // seq_tiles, 0, 0)),
                  pl.BlockSpec((d, LANES), lambda i: (0, 0))],
        out_specs=[pl.BlockSpec((tm, d), lambda i: (i, 0)),
                   pl.BlockSpec((tm, LANES), lambda i: (i, 0)),
                   pl.BlockSpec((tm, LANES), lambda i: (i, 0))],
        out_shape=[jax.ShapeDtypeStruct((lat, d), F32),
                   jax.ShapeDtypeStruct((lat, LANES), jnp.int32),
                   jax.ShapeDtypeStruct((lat, LANES), F32)],
        compiler_params=_cparams(1),
        name="moe_router",
    )(x, g.reshape(1, d), mod, router_pad)


def _moe_plan(top_idx, n_items):
    e_flat = top_idx.reshape(-1)
    n_assign = e_flat.shape[0]
    onehot = (e_flat[:, None] == jnp.arange(N_EXPERTS, dtype=jnp.int32)[None, :]).astype(jnp.int32)
    csum = jnp.cumsum(onehot, axis=0)
    counts = csum[-1]
    rank = jnp.sum(csum * onehot, axis=1) - 1
    blocks = (counts + MOE_BLOCK - 1) // MOE_BLOCK
    blk_end = jnp.cumsum(blocks)
    blk_start = blk_end - blocks
    total = blk_end[-1]
    dest = blk_start[e_flat] * MOE_BLOCK + rank
    row_tok = jnp.zeros((n_items * MOE_BLOCK,), jnp.int32).at[dest].set(
        jnp.arange(n_assign, dtype=jnp.int32) // TOP_K)
    p = jnp.arange(n_items, dtype=jnp.int32)
    pc = jnp.minimum(p, total - 1)
    item_e = jnp.sum((pc[:, None] >= blk_end[None, :]).astype(jnp.int32), axis=1)
    rows_left = counts[item_e] - (pc - blk_start[item_e]) * MOE_BLOCK
    nact = jnp.clip((rows_left + MOE_SUB - 1) // MOE_SUB, 0, MOE_BLOCK // MOE_SUB)
    nact = jnp.where(p < total, nact, 0).astype(jnp.int32)
    out_blk = jnp.where(p < total, p, n_items).astype(jnp.int32)
    items = (item_e.astype(jnp.int32), pc.astype(jnp.int32), out_blk, nact)
    return dest.astype(jnp.int32), row_tok, items, (blocks, blk_start, blk_end, total)


def _moe_steps(items, runs, n_items, nj):
    _, _, _, nact = items
    blocks, blk_start, blk_end, total = runs
    s = jnp.arange(n_items * nj, dtype=jnp.int32)
    live = s < total * nj
    sc = jnp.minimum(s, total * nj - 1)
    e = jnp.sum((sc[:, None] >= (blk_end * nj)[None, :]).astype(jnp.int32), axis=1)
    t = sc - blk_start[e] * nj
    j = t // blocks[e]
    r = t % blocks[e]
    blk = blk_start[e] + jnp.where(j % 2 == 0, r, blocks[e] - 1 - r)
    out_blk = jnp.where(live, blk, n_items)
    out_j = jnp.where(live, j, 0)
    step_nact = jnp.where(live, nact[blk], 0)
    prev_e = jnp.concatenate([jnp.full((1,), -1, jnp.int32), e[:-1]])
    prev_j = jnp.concatenate([jnp.full((1,), -1, jnp.int32), j[:-1]])
    first = jnp.logical_and(live, jnp.logical_or(e != prev_e, j != prev_j))
    slot = (jnp.cumsum(first.astype(jnp.int32)) - 1) % 2
    ids = jnp.arange(N_EXPERTS, dtype=jnp.int32)
    later = jnp.logical_and(ids[None, :] > ids[:, None], blocks[None, :] > 0)
    next_expert = jnp.min(jnp.where(later, ids[None, :], N_EXPERTS), axis=1)
    wraps = j + 1 >= nj
    next_e = jnp.where(wraps, next_expert[e], e)
    next_j = jnp.where(wraps, 0, j + 1)
    has_next = jnp.logical_and(first, next_e < N_EXPERTS)
    next_e = jnp.minimum(next_e, N_EXPERTS - 1)
    return tuple(a.astype(jnp.int32) for a in (e, blk, j, out_blk, out_j, step_nact,
                                                first, slot, next_e, next_j, has_next))


def _gather_kernel(tok_ref, nact_ref, h_ref, o_ref, buf, sem):
    p = pl.program_id(0)
    last = pl.num_programs(0) - 1
    nact = nact_ref[p]
    slot = p % 2

    def copy(to_slot, r, t):
        return pltpu.make_async_copy(h_ref.at[pl.ds(t, 1)], buf.at[to_slot, pl.ds(r, 1)], sem.at[to_slot])

    def issue(item, to_slot):
        def body(grp, c):
            for u in range(GATHER_UNROLL):
                r = grp * GATHER_UNROLL + u
                copy(to_slot, r, tok_ref[item * MOE_BLOCK + r]).start()
            return c

        lax.fori_loop(0, nact_ref[item] * (MOE_SUB // GATHER_UNROLL), body, 0)

    def drain(grp, c):
        for u in range(GATHER_UNROLL):
            copy(slot, grp * GATHER_UNROLL + u, 0).wait()
        return c

    @pl.when(p == 0)
    def _():
        issue(0, 0)

    @pl.when(p < last)
    def _():
        issue(jnp.minimum(p + 1, last), 1 - slot)

    lax.fori_loop(0, nact * (MOE_SUB // GATHER_UNROLL), drain, 0)
    for s in range(MOE_BLOCK // MOE_SUB):
        rows = pl.ds(s * MOE_SUB, MOE_SUB)

        @pl.when(s < nact)
        def _():
            o_ref[rows] = buf[slot, rows].astype(BF16)

        @pl.when(s >= nact)
        def _():
            o_ref[rows] = jnp.zeros((MOE_SUB, o_ref.shape[1]), BF16)


def _moe_gather(h, row_tok, nact, n_items):
    d = h.shape[1]
    return pl.pallas_call(
        _gather_kernel,
        grid_spec=pltpu.PrefetchScalarGridSpec(
            num_scalar_prefetch=2,
            grid=(n_items,),
            in_specs=[pl.BlockSpec(memory_space=pl.ANY)],
            out_specs=pl.BlockSpec((MOE_BLOCK, d), lambda p, tok, na: (p, 0)),
            scratch_shapes=[pltpu.VMEM((2, MOE_BLOCK, d), F32), pltpu.SemaphoreType.DMA((2,))]),
        out_shape=jax.ShapeDtypeStruct((n_items * MOE_BLOCK, d), BF16),
        compiler_params=_cparams(1),
        name="moe_gather",
    )(row_tok, nact, h)


def _for_active_rows(nact, in_ref, o_ref, fn):
    n_sub = MOE_BLOCK // MOE_SUB
    for k in range(n_sub + 1):
        @pl.when(nact == k)
        def _():
            if k > 0:
                o_ref[0:k * MOE_SUB] = fn(in_ref[0:k * MOE_SUB])
            if k < n_sub:
                o_ref[k * MOE_SUB:] = jnp.zeros((MOE_BLOCK - k * MOE_SUB, o_ref.shape[1]), o_ref.dtype)


def _moe_glu_kernel(e_ref, blk_ref, j_ref, oblk_ref, oj_ref, nact_ref, first_ref, slot_ref, ne_ref, nj_ref,
                    more_ref, h_ref, wg_ref, wu_ref, o_ref, wbuf, sem):
    s = pl.program_id(0)
    nact = nact_ref[s]
    slot = slot_ref[s]
    tf = wbuf.shape[3]

    def tile_copies(e, j, to_slot):
        cols = pl.ds(pl.multiple_of(j * tf, tf), tf)
        return [pltpu.make_async_copy(w_ref.at[e, :, cols], wbuf.at[to_slot, i], sem.at[to_slot])
                for i, w_ref in enumerate((wg_ref, wu_ref))]

    @pl.when(s == 0)
    def _():
        for cp in tile_copies(e_ref[0], j_ref[0], 0):
            cp.start()

    @pl.when(first_ref[s] == 1)
    def _():
        for cp in tile_copies(e_ref[s], j_ref[s], slot):
            cp.wait()

        @pl.when(more_ref[s] == 1)
        def _():
            for cp in tile_copies(ne_ref[s], nj_ref[s], 1 - slot):
                cp.start()

    def run(h):
        gate = _dot(h, wbuf[slot, 0].astype(BF16))
        return (_silu(gate) * _dot(h, wbuf[slot, 1].astype(BF16))).astype(BF16)

    _for_active_rows(nact, h_ref, o_ref, run)


def _moe_glu(hs, wg, wu, steps, n_items, tf):
    d, f = wg.shape[1], wg.shape[2]
    nj = f // tf
    n_pre = len(steps)
    return pl.pallas_call(
        _moe_glu_kernel,
        grid_spec=pltpu.PrefetchScalarGridSpec(
            num_scalar_prefetch=n_pre,
            grid=(n_items * nj,),
            in_specs=[pl.BlockSpec((MOE_BLOCK, d), lambda s, *pre: (pre[1][s], 0)),
                      pl.BlockSpec(memory_space=pl.ANY),
                      pl.BlockSpec(memory_space=pl.ANY)],
            out_specs=pl.BlockSpec((MOE_BLOCK, tf), lambda s, *pre: (pre[3][s], pre[4][s])),
            scratch_shapes=[pltpu.VMEM((2, 2, d, tf), F32), pltpu.SemaphoreType.DMA((2,))]),
        out_shape=jax.ShapeDtypeStruct(((n_items + 1) * MOE_BLOCK, f), BF16),
        compiler_params=_cparams(1),
        name="moe_glu",
    )(*steps, hs, wg, wu)


def _moe_down_kernel(e_ref, blk_ref, oblk_ref, nact_ref, a_ref, wd_ref, o_ref):
    nact = nact_ref[pl.program_id(0)]
    _for_active_rows(nact, a_ref, o_ref, lambda a: _dot(a, wd_ref[0].astype(BF16)))


def _moe_down(a, wd, plan, n_items, tn):
    item_e, in_blk, out_blk, nact = plan
    f, d = wd.shape[1], wd.shape[2]
    nj = d // tn

    def w_map(p, j, e, b, ob, na):
        return (e[p], 0, jnp.where(na[p] > 0, j, nj - 1))

    def o_map(p, j, e, b, ob, na):
        return (ob[p], jnp.where(na[p] > 0, j, 0))

    return pl.pallas_call(
        _moe_down_kernel,
        grid_spec=pltpu.PrefetchScalarGridSpec(
            num_scalar_prefetch=4,
            grid=(n_items, nj),
            in_specs=[pl.BlockSpec((MOE_BLOCK, f), lambda p, j, e, b, ob, na: (b[p], 0)),
                      pl.BlockSpec((1, f, tn), w_map)],
            out_specs=pl.BlockSpec((MOE_BLOCK, tn), o_map)),
        out_shape=jax.ShapeDtypeStruct(((n_items + 1) * MOE_BLOCK, d), F32),
        compiler_params=_cparams(2),
        name="moe_down",
    )(item_e, in_blk, out_blk, nact, a, wd)


def _combine_kernel(dest_ref, x_ref, mod_ref, w_ref, fg_ref, y_ref, o_ref, buf, sem, *, tm):
    i = pl.program_id(0)
    last = pl.num_programs(0) - 1
    slot = i % 2

    def copy(to_slot, r, k, row):
        return pltpu.make_async_copy(y_ref.at[pl.ds(row, 1)], buf.at[to_slot, k, pl.ds(r, 1)],
                                     sem.at[to_slot])

    def issue(tile, to_slot):
        def body(grp, c):
            for u in range(GATHER_UNROLL):
                r = grp * GATHER_UNROLL + u
                for k in range(TOP_K):
                    copy(to_slot, r, k, dest_ref[(tile * tm + r) * TOP_K + k]).start()
            return c

        lax.fori_loop(0, tm // GATHER_UNROLL, body, 0)

    def drain(grp, c):
        for u in range(GATHER_UNROLL):
            for k in range(TOP_K):
                copy(slot, grp * GATHER_UNROLL + u, k, 0).wait()
        return c

    @pl.when(i == 0)
    def _():
        issue(0, 0)

    @pl.when(i < last)
    def _():
        issue(jnp.minimum(i + 1, last), 1 - slot)

    lax.fori_loop(0, tm // GATHER_UNROLL, drain, 0)
    w = w_ref[...]
    moe = w[:, 0:1] * buf[slot, 0] + w[:, 1:2] * buf[slot, 1]
    m = mod_ref[0]
    o_ref[...] = _rms(x_ref[...] + m[5:6] * moe, fg_ref[...])


def _moe_combine(dest, x, mod, top_w, final_g, ys, dims, tm):
    lat, d = x.shape
    seq_tiles = dims["seq"] // tm
    return pl.pallas_call(
        functools.partial(_combine_kernel, tm=tm),
        grid_spec=pltpu.PrefetchScalarGridSpec(
            num_scalar_prefetch=1,
            grid=(lat // tm,),
            in_specs=[pl.BlockSpec((tm, d), lambda i, dr: (i, 0)),
                      pl.BlockSpec((1, 6, d), lambda i, dr: (i // seq_tiles, 0, 0)),
                      pl.BlockSpec((tm, LANES), lambda i, dr: (i, 0)),
                      pl.BlockSpec((1, d), lambda i, dr: (0, 0)),
                      pl.BlockSpec(memory_space=pl.ANY)],
            out_specs=pl.BlockSpec((tm, d), lambda i, dr: (i, 0)),
            scratch_shapes=[pltpu.VMEM((2, TOP_K, tm, d), F32), pltpu.SemaphoreType.DMA((2,))]),
        out_shape=jax.ShapeDtypeStruct((lat, d), F32),
        compiler_params=_cparams(1),
        name="moe_combine",
    )(dest, x, mod, top_w, final_g.reshape(1, d), ys)


def kernel(x, c, ctx, c_ctx, ada_w, ada_b, norm1_g, norm2_g, ev_w_in, ev_w_out, ev_lambda, ev_subln_g, od_w_dq, od_q_norm_g, od_w_uq, od_w_dkv, od_kv_norm_g, od_w_ukv, od_w_o, ffn_w_gate, ffn_w_up, ffn_w_down, moe_router, moe_w_gate, moe_w_up, moe_w_down, final_norm_g):
    batch, seq, d = x.shape
    n_ctx = ctx.shape[1]
    depth = ada_w.shape[0]
    assert depth == 2 and batch < MOD_ROWS and seq % GRID_W == 0
    lat = batch * seq
    dims = dict(batch=batch, seq=seq, ctx=n_ctx, lat=lat)
    tm = 1024
    assert seq % tm == 0 and (batch * n_ctx) % tm == 0

    cond = jnp.concatenate([c, c_ctx[None, :], jnp.zeros((MOD_ROWS - batch - 1, d), F32)], axis=0)
    mod = _ada(cond, ada_w, ada_b).reshape(depth, MOD_ROWS, 6, d)
    x_lat, x_ctx = x.reshape(lat, d), ctx.reshape(batch * n_ctx, d)
    m_rows = lat + batch * n_ctx

    lam_init = 0.8 - 0.6 * math.exp(-0.3 * 0)
    fw = ev_w_in.shape[2] // 4
    tabs_ev = jnp.asarray(np.stack([
        _rope_tables(seq, tm, DIFF_HEAD_DIM // 4, DIFF_HEAD_DIM ** -0.5 * LOG2E),
        _rope_tables(seq, tm, DIFF_HEAD_DIM // 4, 1.0)]))
    w_in = ev_w_in[0].astype(BF16)
    qkvf = _inproj(x_lat, None, norm1_g[0], mod[0], w_in, tabs_ev, dims, tm, 1024)
    qkvf = _inproj(x_ctx, qkvf, norm1_g[0], mod[0], w_in, tabs_ev, dims, tm, 1024)
    o_attn = _diff_attn(qkvf, None, ev_lambda[0], ev_subln_g[0], lam_init, dims, 1024, True)
    o_attn = _diff_attn(qkvf, o_attn, ev_lambda[0], ev_subln_g[0], lam_init, dims, n_ctx, False)

    gw = fw // FOURIER_GROUPS
    cc_np, sc_np = _dft_cos_sin(gw)
    cc = jnp.asarray(cc_np.astype(np.float32)).astype(BF16)
    sc = jnp.asarray(sc_np.astype(np.float32)).astype(BF16)
    cn_np, sn_np = _dft_cos_sin(seq)
    cs_lat = jnp.asarray(np.concatenate([cn_np, -sn_np], axis=1).astype(np.float32)).astype(BF16)
    cx_np, sx_np = _dft_cos_sin(n_ctx)
    cs_ctx = jnp.asarray(np.concatenate([cx_np, -sx_np], axis=1).astype(np.float32)).astype(BF16)
    fm = _fourier(qkvf, None, seq, 0, batch, fw, cc, sc, cs_lat, "ev_fourier_lat")
    fm = _fourier(qkvf, fm, n_ctx, lat // n_ctx, batch, fw, cc, sc, cs_ctx, "ev_fourier_ctx")

    w_out = ev_w_out[0].astype(BF16)
    xs = _mm_res([fm, o_attn], w_out, x_lat, mod[0], 2, dims, lat, tm, 1024, "ev_outproj_lat", out_rows=m_rows)
    xs = _mm_res([fm, o_attn], w_out, x_ctx, mod[0], 2, dims, batch * n_ctx, tm, 1024, "ev_outproj_ctx", prev=xs)
    act = _glu(xs, norm2_g[0], mod[0], ffn_w_gate[0], ffn_w_up[0], dims, tm, 512)
    xs = _mm_res([act], ffn_w_down[0].astype(BF16), xs, mod[0], 5, dims, m_rows, tm, 512, "ffn_down")

    heads = od_w_ukv.shape[2] // (MLA_NOPE + MLA_V)
    q_lora = od_w_dq.shape[2]
    wd_cat = jnp.concatenate(
        [od_w_dq[0], od_w_dkv[0], jnp.zeros((d, LANES - MLA_ROPE), F32)], axis=1).astype(BF16)
    wuq = jnp.pad(od_w_uq[0].reshape(q_lora, heads, MLA_NOPE + MLA_ROPE),
                  ((0, 0), (0, 0), (0, 2 * LANES - MLA_NOPE - MLA_ROPE))).reshape(q_lora, heads * 2 * LANES)
    tm_mla = 256
    tab_q = jnp.asarray(_rope_tables(seq, tm_mla, MLA_ROPE // 4, (MLA_NOPE + MLA_ROPE) ** -0.5 * LOG2E))
    tab_k = jnp.asarray(_rope_tables(seq, tm_mla, MLA_ROPE // 4, 1.0))
    q, kv, kr = _mla_proj(xs, norm1_g[1], mod[1], wd_cat, od_q_norm_g[0], od_kv_norm_g[0],
                          wuq.astype(BF16), od_w_ukv[0].astype(BF16), tab_q, tab_k, dims, tm_mla)
    o_mla = _mla_attn(q, kv, kr, dims, 2048)
    xl = _mm_res([o_mla], od_w_o[0].astype(BF16), xs, mod[1], 2, dims, lat, tm, 1024, "od_outproj")

    router_pad = jnp.pad(moe_router[0], ((0, 0), (0, LANES - N_EXPERTS)))
    h2, top_idx, top_w = _router(xl, norm2_g[1], mod[1], router_pad, dims, 512)
    n_items = lat * TOP_K // MOE_BLOCK + N_EXPERTS
    dest, row_tok, items, runs = _moe_plan(top_idx[:, :TOP_K], n_items)
    tf = 512
    steps = _moe_steps(items, runs, n_items, moe_w_gate.shape[3] // tf)
    hs = _moe_gather(h2, row_tok, items[3], n_items)
    act = _moe_glu(hs, moe_w_gate[0], moe_w_up[0], steps, n_items, tf)
    ys = _moe_down(act, moe_w_down[0], items, n_items, 256)
    out = _moe_combine(dest, xl, mod[1], top_w, final_norm_g, ys, dims, 256)
    return out.reshape(batch, seq, d)
```

```python
import functools
import math

import numpy as np
import jax
import jax.numpy as jnp
from jax import lax
from jax.experimental import pallas as pl
from jax.experimental.pallas import tpu as pltpu

F32 = jnp.float32
BF16 = jnp.bfloat16

GRID_W = 64
NORM_EPS = 1e-6
ROPE_BASE = 10000.0
FOURIER_GROUPS = 4
DIFF_HEAD_DIM = 128
MLA_NOPE = 128
MLA_ROPE = 64
MLA_V = 128
N_EXPERTS = 8
TOP_K = 2

LANES = 128
MOD_ROWS = 8
VMEM_LIMIT = 56 * 1024 * 1024
MOE_BLOCK = 1024
MOE_SUB = 64
GATHER_UNROLL = 8
KEY_CHUNK = 256
LOG2E = math.log2(math.e)


def _cparams(n_axes):
    return pltpu.CompilerParams(dimension_semantics=("arbitrary",) * n_axes,
                                vmem_limit_bytes=VMEM_LIMIT)


def _rms(x, g):
    return x * lax.rsqrt(jnp.mean(x * x, axis=-1, keepdims=True) + NORM_EPS) * g


def _norm_mod(x, g, shift, scale):
    return _rms(x, g) * (1.0 + scale) + shift


def _silu(x):
    return x * (1.0 / (1.0 + jnp.exp(-x)))


def _dot(a, b):
    return jnp.dot(a, b, preferred_element_type=F32)


def _dot_nt(a, b):
    return lax.dot_general(a, b, (((1,), (1,)), ((), ())), preferred_element_type=F32)


def _rope_tables(seq, extra_rows, chunk, scale):
    n = np.arange(seq)
    row, col = n // GRID_W, n % GRID_W
    lane = np.arange(LANES)
    a = 2 * chunk
    inv = ROPE_BASE ** (-np.arange(0, a, 2, dtype=np.float64) / a)
    used = lane < 4 * chunk
    freq = inv[lane % chunk]
    pos = np.where(lane[None, :] < 2 * chunk, row[:, None], col[:, None]).astype(np.float64)
    ang = pos * freq[None, :]
    first = (lane // chunk) % 2 == 0
    cos = np.where(used[None, :], np.cos(ang), 0.0)
    sin = np.where(used[None, :], np.sin(ang), 0.0)
    s1 = np.where(first[None, :], -sin, 0.0)
    s2 = np.where(first[None, :], 0.0, sin)
    ident = np.zeros((3, extra_rows, LANES))
    ident[0] = used[None, :].astype(np.float64)
    tab = np.concatenate([np.stack([cos, s1, s2]), ident], axis=1) * scale
    return tab.astype(np.float32)


def _apply_rope(x, tab_ref, chunk):
    return (x * tab_ref[0] + pltpu.roll(x, LANES - chunk, 1) * tab_ref[1]
            + pltpu.roll(x, chunk, 1) * tab_ref[2])


def _dft_cos_sin(n):
    k = np.arange(n)
    ang = 2.0 * np.pi * ((k[:, None] * k[None, :]) % n) / n
    return np.cos(ang), np.sin(ang)


def _ada_kernel(s_ref, w_ref, b_ref, o_ref):
    s = _silu(s_ref[...]).astype(BF16)
    o_ref[0] = _dot(s, w_ref[0].astype(BF16)) + b_ref[0]


def _ada(cond, ada_w, ada_b):
    depth, d, n = ada_w.shape
    tn = 1024
    return pl.pallas_call(
        _ada_kernel,
        grid=(depth, n // tn),
        in_specs=[pl.BlockSpec((MOD_ROWS, d), lambda i, j: (0, 0)),
                  pl.BlockSpec((1, d, tn), lambda i, j: (i, 0, j)),
                  pl.BlockSpec((1, 1, tn), lambda i, j: (i, 0, j))],
        out_specs=pl.BlockSpec((1, MOD_ROWS, tn), lambda i, j: (i, 0, j)),
        out_shape=jax.ShapeDtypeStruct((depth, MOD_ROWS, n), F32),
        compiler_params=_cparams(2),
        name="ada",
    )(cond, ada_w, ada_b.reshape(depth, 1, n))


def _inproj_kernel(x_ref, g_ref, mod_ref, w_ref, tab_ref, *rest):
    o_ref, h_ref = rest[-2:]
    j = pl.program_id(1)
    quarter = pl.num_programs(1) // 4

    @pl.when(j == 0)
    def _():
        m = mod_ref[0]
        h_ref[...] = _norm_mod(x_ref[...], g_ref[...], m[0:1], m[1:2]).astype(BF16)

    is_rope = jnp.logical_and(j >= quarter, j < 3 * quarter)

    @pl.when(is_rope)
    def _():
        res = _dot(h_ref[...], w_ref[...].astype(BF16))
        for c in range(res.shape[1] // LANES):
            sl = slice(c * LANES, (c + 1) * LANES)
            o_ref[:, sl] = _apply_rope(res[:, sl], tab_ref.at[0], DIFF_HEAD_DIM // 4).astype(BF16)

    @pl.when(jnp.logical_not(is_rope))
    def _():
        o_ref[...] = _dot(h_ref[...], w_ref[...].astype(BF16)).astype(BF16)


def _inproj(x, prev, g, mod, w, tabs, dims, tm, tn):
    rows, d = x.shape
    seq, lat, batch = dims["seq"], dims["lat"], dims["batch"]
    m_rows = lat + batch * dims["ctx"]
    n = w.shape[1]
    nj = n // tn
    assert nj % 4 == 0
    lat_tiles, seq_tiles = lat // tm, seq // tm
    is_ctx = prev is not None
    row_blk0 = lat_tiles if is_ctx else 0

    def tab_map(i, j):
        return (jnp.where(j >= nj // 2, 1, 0), 0, seq_tiles if is_ctx else i % seq_tiles, 0)

    in_specs = [pl.BlockSpec((tm, d), lambda i, j: (i, 0)),
                pl.BlockSpec((1, d), lambda i, j: (0, 0)),
                pl.BlockSpec((1, 6, d), lambda i, j: (batch if is_ctx else i // seq_tiles, 0, 0)),
                pl.BlockSpec((d, tn), lambda i, j: (0, j)),
                pl.BlockSpec((1, 3, tm, LANES), tab_map)]
    args = [x, g.reshape(1, d), mod, w, tabs]
    aliases = {}
    if is_ctx:
        in_specs.append(pl.BlockSpec(memory_space=pl.ANY))
        args.append(prev)
        aliases = {len(args) - 1: 0}
    return pl.pallas_call(
        _inproj_kernel,
        grid=(rows // tm, nj),
        in_specs=in_specs,
        out_specs=pl.BlockSpec((tm, tn), lambda i, j: (row_blk0 + i, j)),
        out_shape=jax.ShapeDtypeStruct((m_rows, n), BF16),
        scratch_shapes=[pltpu.VMEM((tm, d), BF16)],
        input_output_aliases=aliases,
        compiler_params=_cparams(2),
        name="ev_inproj_ctx" if is_ctx else "ev_inproj_lat",
    )(*args)


def _softmax_numerators(qs, k_ref, kcols, s_scr, p_scr, want_sum):
    n_keys = k_ref.shape[0]
    chunks = [slice(c, min(c + KEY_CHUNK, n_keys)) for c in range(0, n_keys, KEY_CHUNK)]
    maxes = []
    for i, (q, cols) in enumerate(zip(qs, kcols)):
        m = None
        for ks in chunks:
            s = _dot_nt(q, k_ref[ks, cols])
            s_scr[i, :, ks] = s
            mc = jnp.max(s, axis=-1, keepdims=True)
            m = mc if m is None else jnp.maximum(m, mc)
        maxes.append(m)
    totals = []
    for i, m in enumerate(maxes):
        total = None
        for ks in chunks:
            e = jnp.exp2(s_scr[i, :, ks] - m)
            if want_sum:
                part = jnp.sum(e, axis=-1, keepdims=True)
                total = part if total is None else total + part
            p_scr[i, :, ks] = e.astype(BF16)
        totals.append(total)
    return totals


def _diff_attn_kernel(lam_ref, q_ref, *rest, lam_init, seg_rows):
    n_seg = len(seg_rows)
    k_refs, v_refs, g_ref = rest[:n_seg], rest[n_seg:2 * n_seg], rest[2 * n_seg]
    o_ref, k_scr, v_scr, s_scr, p_scr = rest[-5:]
    hd = DIFF_HEAD_DIM

    @pl.when(pl.program_id(2) == 0)
    def _():
        r0 = 0
        for k_ref, v_ref, n in zip(k_refs, v_refs, seg_rows):
            k_scr[r0:r0 + n] = k_ref[...]
            v_scr[r0:r0 + n] = v_ref[...]
            r0 += n

    lv = lam_ref[...]
    lam = (jnp.exp(jnp.sum(lv[0:1] * lv[1:2], axis=-1, keepdims=True))
           - jnp.exp(jnp.sum(lv[2:3] * lv[3:4], axis=-1, keepdims=True)) + lam_init)
    q = q_ref[...]
    cols = [slice(c * hd, (c + 1) * hd) for c in range(2)]
    totals = _softmax_numerators([q[:, c] for c in cols], k_scr, cols, s_scr, p_scr, True)
    outs = [_dot(p_scr[c], v_scr[...]) * (1.0 / totals[c]) for c in range(2)]
    o = outs[0] - lam * outs[1]
    o_ref[...] = (_rms(o, g_ref[...]) * (1.0 - lam_init)).astype(BF16)


def _diff_attn(qkvf, prev, lam_vec, subln_g, lam_init, dims, tq, latent):
    m_rows = qkvf.shape[0]
    batch, seq, ctx = dims["batch"], dims["seq"], dims["ctx"]
    hw = 2 * DIFF_HEAD_DIM
    width = qkvf.shape[1] // 4
    heads = width // hw
    ctx_blk0 = batch * seq // ctx
    qcol, kcol, vcol = width // hw, 2 * width // hw, 3 * width // hw
    if latent:
        nq, q_blk0, seg_rows = seq // tq, 0, (ctx, seq)
    else:
        assert tq == ctx
        nq, q_blk0, seg_rows = 1, ctx_blk0, (ctx,)
    n_keys = sum(seg_rows)

    def kv_specs(col):
        specs = [pl.BlockSpec((ctx, hw), lambda b, h, i: (ctx_blk0 + b, col + h))]
        if latent:
            specs.append(pl.BlockSpec((seq, hw), lambda b, h, i: (b, col + h)))
        return specs

    in_specs = ([pl.BlockSpec((4, DIFF_HEAD_DIM), lambda b, h, i: (0, 0)),
                 pl.BlockSpec((tq, hw), lambda b, h, i: (q_blk0 + b * nq + i, qcol + h))]
                + kv_specs(kcol) + kv_specs(vcol)
                + [pl.BlockSpec((1, hw), lambda b, h, i: (0, 0))])
    args = [lam_vec, qkvf] + [qkvf] * (2 * len(seg_rows)) + [subln_g.reshape(1, hw)]
    aliases = {}
    if prev is not None:
        in_specs.append(pl.BlockSpec(memory_space=pl.ANY))
        args.append(prev)
        aliases = {len(args) - 1: 0}
    kern = functools.partial(_diff_attn_kernel, lam_init=lam_init, seg_rows=seg_rows)
    return pl.pallas_call(
        kern,
        grid=(batch, heads, nq),
        in_specs=in_specs,
        out_specs=pl.BlockSpec((tq, hw), lambda b, h, i: (q_blk0 + b * nq + i, h)),
        out_shape=jax.ShapeDtypeStruct((m_rows, width), BF16),
        scratch_shapes=[pltpu.VMEM((n_keys, hw), BF16), pltpu.VMEM((n_keys, hw), BF16),
                        pltpu.VMEM((2, tq, n_keys), F32), pltpu.VMEM((2, tq, n_keys), BF16)],
        input_output_aliases=aliases,
        compiler_params=_cparams(3),
        name="ev_diff_attn_lat" if latent else "ev_diff_attn_ctx",
    )(*args)


def _fourier_kernel(u_ref, cc_ref, sc_ref, cs_ref, *rest, n, norm):
    o_ref, ab_ref = rest[-2], rest[-1]
    u = u_ref[...]
    ab_ref[0:n] = _dot(u, cc_ref[...]).astype(BF16)
    ab_ref[n:] = _dot(u, sc_ref[...]).astype(BF16)
    o_ref[...] = (_dot(cs_ref[...], ab_ref[...]) * norm).astype(BF16)


def _fourier(qkvf, prev, n, row_blk0, batch, width, cc, sc, cs, name):
    m_rows = qkvf.shape[0]
    gw = width // FOURIER_GROUPS
    kern = functools.partial(_fourier_kernel, n=n, norm=1.0 / math.sqrt(n * gw))
    in_specs = [pl.BlockSpec((n, gw), lambda b, g: (row_blk0 + b, g)),
                pl.BlockSpec((gw, gw), lambda b, g: (0, 0)),
                pl.BlockSpec((gw, gw), lambda b, g: (0, 0)),
                pl.BlockSpec((n, 2 * n), lambda b, g: (0, 0), pipeline_mode=pl.Buffered(1))]
    args = [qkvf, cc, sc, cs]
    aliases = {}
    if prev is not None:
        in_specs.append(pl.BlockSpec(memory_space=pl.ANY))
        args.append(prev)
        aliases = {4: 0}
    return pl.pallas_call(
        kern,
        grid=(batch, FOURIER_GROUPS),
        in_specs=in_specs,
        out_specs=pl.BlockSpec((n, gw), lambda b, g: (row_blk0 + b, g)),
        out_shape=jax.ShapeDtypeStruct((m_rows, width), BF16),
        scratch_shapes=[pltpu.VMEM((2 * n, gw), BF16)],
        input_output_aliases=aliases,
        compiler_params=_cparams(2),
        name=name,
    )(*args)


def _mm_res_kernel(*refs, n_a, gate_idx):
    a_refs, w_refs = refs[:n_a], refs[n_a:2 * n_a]
    x_ref, mod_ref, o_ref = refs[2 * n_a], refs[2 * n_a + 1], refs[-1]
    acc = _dot(a_refs[0][...], w_refs[0][...].astype(BF16))
    for a_ref, w_ref in zip(a_refs[1:], w_refs[1:]):
        acc = acc + _dot(a_ref[...], w_ref[...].astype(BF16))
    m = mod_ref[0]
    o_ref[...] = x_ref[...] + m[gate_idx:gate_idx + 1] * acc


def _mm_res(a_list, w, x, mod, gate_idx, dims, rows, tm, tn, name, out_rows=None, prev=None):
    d = w.shape[1]
    seq_tiles = dims["seq"] // tm
    n_a = len(a_list)
    ctx_only = prev is not None
    row_blk0 = dims["lat"] // tm if ctx_only else 0
    out_rows = prev.shape[0] if ctx_only else (out_rows or rows)

    def sel(i):
        return dims["batch"] if ctx_only else jnp.minimum(i // seq_tiles, dims["batch"])

    in_specs, w_args, k0 = [], [], 0
    for a in a_list:
        in_specs.append(pl.BlockSpec((tm, a.shape[1]), lambda i, j: (row_blk0 + i, 0)))
    for a in a_list:
        ka = a.shape[1]
        assert k0 % ka == 0
        in_specs.append(pl.BlockSpec((ka, tn), lambda i, j, kb=k0 // ka: (kb, j)))
        w_args.append(w)
        k0 += ka
    assert k0 == w.shape[0]
    in_specs += [pl.BlockSpec((tm, tn), lambda i, j: (i, j)),
                 pl.BlockSpec((1, 6, tn), lambda i, j: (sel(i), 0, j))]
    args = [*a_list, *w_args, x, mod]
    aliases = {}
    if prev is not None:
        in_specs.append(pl.BlockSpec(memory_space=pl.ANY))
        args.append(prev)
        aliases = {len(args) - 1: 0}
    return pl.pallas_call(
        functools.partial(_mm_res_kernel, n_a=n_a, gate_idx=gate_idx),
        grid=(rows // tm, d // tn),
        in_specs=in_specs,
        out_specs=pl.BlockSpec((tm, tn), lambda i, j: (row_blk0 + i, j)),
        out_shape=jax.ShapeDtypeStruct((out_rows, d), F32),
        input_output_aliases=aliases,
        compiler_params=_cparams(2),
        name=name,
    )(*args)


def _glu_kernel(x_ref, g_ref, mod_ref, wg_ref, wu_ref, o_ref, h_ref):
    @pl.when(pl.program_id(1) == 0)
    def _():
        m = mod_ref[0]
        h_ref[...] = _norm_mod(x_ref[...], g_ref[...], m[3:4], m[4:5]).astype(BF16)

    h = h_ref[...]
    gate = _dot(h, wg_ref[...].astype(BF16))
    up = _dot(h, wu_ref[...].astype(BF16))
    o_ref[...] = (_silu(gate) * up).astype(BF16)


def _glu(x, g, mod, wg, wu, dims, tm, tf):
    m_rows, d = x.shape
    f = wg.shape[1]
    seq_tiles = dims["seq"] // tm

    def sel(i):
        return jnp.minimum(i // seq_tiles, dims["batch"])

    return pl.pallas_call(
        _glu_kernel,
        grid=(m_rows // tm, f // tf),
        in_specs=[pl.BlockSpec((tm, d), lambda i, j: (i, 0)),
                  pl.BlockSpec((1, d), lambda i, j: (0, 0)),
                  pl.BlockSpec((1, 6, d), lambda i, j: (sel(i), 0, 0)),
                  pl.BlockSpec((d, tf), lambda i, j: (0, j)),
                  pl.BlockSpec((d, tf), lambda i, j: (0, j))],
        out_specs=pl.BlockSpec((tm, tf), lambda i, j: (i, j)),
        out_shape=jax.ShapeDtypeStruct((m_rows, f), BF16),
        scratch_shapes=[pltpu.VMEM((tm, d), BF16)],
        compiler_params=_cparams(2),
        name="ffn_glu",
    )(x, g.reshape(1, d), mod, wg, wu)


def _mla_proj_kernel(x_ref, g_ref, mod_ref, wd_ref, qg_ref, kvg_ref, wuq_ref, wukv_ref,
                     tq_ref, tk_ref, q_ref, kv_ref, kr_ref, *, n_lat_tiles, q_lora, kv_lora, heads):
    i = pl.program_id(0)
    m = mod_ref[0]
    h = _norm_mod(x_ref[...], g_ref[...], m[0:1], m[1:2]).astype(BF16)
    t = _dot(h, wd_ref[...])
    ckv = _rms(t[:, q_lora:q_lora + kv_lora], kvg_ref[...]).astype(BF16)
    kv_ref[...] = _dot(ckv, wukv_ref[...]).astype(BF16)
    kr = t[:, q_lora + kv_lora:]
    kr_ref[...] = _apply_rope(kr, tk_ref, MLA_ROPE // 4).astype(BF16)

    @pl.when(i < n_lat_tiles)
    def _():
        cq = _rms(t[:, :q_lora], qg_ref[...]).astype(BF16)
        q = _dot(cq, wuq_ref[...])
        scale = (MLA_NOPE + MLA_ROPE) ** -0.5 * LOG2E
        for hh in range(heads):
            c0 = hh * 2 * LANES
            q_ref[:, c0:c0 + LANES] = (q[:, c0:c0 + LANES] * scale).astype(BF16)
            q_ref[:, c0 + LANES:c0 + 2 * LANES] = _apply_rope(
                q[:, c0 + LANES:c0 + 2 * LANES], tq_ref, MLA_ROPE // 4).astype(BF16)


def _mla_proj(x, g, mod, wd, qg, kvg, wuq, wukv, tab_q, tab_k, dims, tm):
    m_rows, d = x.shape
    seq, lat = dims["seq"], dims["lat"]
    q_lora, kv_lora = qg.shape[0], kvg.shape[0]
    heads = wukv.shape[1] // (MLA_NOPE + MLA_V)
    lat_tiles, seq_tiles = lat // tm, seq // tm

    def sel(i):
        return jnp.minimum(i // seq_tiles, dims["batch"])

    def tab_map(i):
        return (0, jnp.where(i < lat_tiles, i % seq_tiles, seq_tiles), 0)

    kern = functools.partial(_mla_proj_kernel, n_lat_tiles=lat_tiles, q_lora=q_lora, kv_lora=kv_lora,
                             heads=heads)
    const = lambda i: (0, 0)
    return pl.pallas_call(
        kern,
        grid=(m_rows // tm,),
        in_specs=[pl.BlockSpec((tm, d), lambda i: (i, 0)),
                  pl.BlockSpec((1, d), const),
                  pl.BlockSpec((1, 6, d), lambda i: (sel(i), 0, 0)),
                  pl.BlockSpec(wd.shape, const),
                  pl.BlockSpec((1, q_lora), const),
                  pl.BlockSpec((1, kv_lora), const),
                  pl.BlockSpec(wuq.shape, const),
                  pl.BlockSpec(wukv.shape, const),
                  pl.BlockSpec((3, tm, LANES), tab_map),
                  pl.BlockSpec((3, tm, LANES), tab_map)],
        out_specs=[pl.BlockSpec((tm, wuq.shape[1]), lambda i: (jnp.minimum(i, lat_tiles - 1), 0)),
                   pl.BlockSpec((tm, wukv.shape[1]), lambda i: (i, 0)),
                   pl.BlockSpec((tm, LANES), lambda i: (i, 0))],
        out_shape=[jax.ShapeDtypeStruct((lat, wuq.shape[1]), BF16),
                   jax.ShapeDtypeStruct((m_rows, wukv.shape[1]), BF16),
                   jax.ShapeDtypeStruct((m_rows, LANES), BF16)],
        compiler_params=_cparams(1),
        name="od_mla_proj",
    )(x, g.reshape(1, d), mod, wd, qg.reshape(1, q_lora), kvg.reshape(1, kv_lora), wuq, wukv, tab_q, tab_k)


def _mla_attn_kernel(q_ref, knc_ref, knl_ref, krc_ref, krl_ref, vc_ref, vl_ref, o_ref,
                     k_scr, v_scr, s_scr, p_scr, *, n_ctx):
    @pl.when(pl.program_id(2) == 0)
    def _():
        k_scr[0:n_ctx, 0:LANES] = knc_ref[...]
        k_scr[n_ctx:, 0:LANES] = knl_ref[...]
        k_scr[0:n_ctx, LANES:] = krc_ref[...]
        k_scr[n_ctx:, LANES:] = krl_ref[...]
        v_scr[0:n_ctx, 0:LANES] = vc_ref[...]
        v_scr[n_ctx:, 0:LANES] = vl_ref[...]
        v_scr[:, LANES:] = jnp.ones((v_scr.shape[0], LANES), BF16)

    half = q_ref.shape[0] // 2
    halves = [slice(0, half), slice(half, 2 * half)]
    _softmax_numerators([q_ref[r] for r in halves], k_scr, [slice(None)] * 2, s_scr, p_scr, False)
    for i, r in enumerate(halves):
        acc = _dot(p_scr[i], v_scr[...])
        o_ref[r] = (acc[:, :LANES] / acc[:, LANES:LANES + 1]).astype(BF16)


def _mla_attn(q, kv, kr, dims, tq):
    batch, seq, ctx, lat = dims["batch"], dims["seq"], dims["ctx"], dims["lat"]
    heads = q.shape[1] // (2 * LANES)
    nq = seq // tq
    ctx_blk0 = lat // ctx
    return pl.pallas_call(
        functools.partial(_mla_attn_kernel, n_ctx=ctx),
        grid=(batch, heads, nq),
        in_specs=[pl.BlockSpec((tq, 2 * LANES), lambda b, h, i: (b * nq + i, h)),
                  pl.BlockSpec((ctx, LANES), lambda b, h, i: (ctx_blk0 + b, 2 * h)),
                  pl.BlockSpec((seq, LANES), lambda b, h, i: (b, 2 * h)),
                  pl.BlockSpec((ctx, LANES), lambda b, h, i: (ctx_blk0 + b, 0)),
                  pl.BlockSpec((seq, LANES), lambda b, h, i: (b, 0)),
                  pl.BlockSpec((ctx, LANES), lambda b, h, i: (ctx_blk0 + b, 2 * h + 1)),
                  pl.BlockSpec((seq, LANES), lambda b, h, i: (b, 2 * h + 1))],
        out_specs=pl.BlockSpec((tq, LANES), lambda b, h, i: (b * nq + i, h)),
        out_shape=jax.ShapeDtypeStruct((lat, heads * LANES), BF16),
        scratch_shapes=[pltpu.VMEM((ctx + seq, 2 * LANES), BF16), pltpu.VMEM((ctx + seq, 2 * LANES), BF16),
                        pltpu.VMEM((2, tq // 2, ctx + seq), F32), pltpu.VMEM((2, tq // 2, ctx + seq), BF16)],
        compiler_params=_cparams(3),
        name="od_mla_attn",
    )(q, kv, kv, kr, kr, kv, kv)


def _router_kernel(x_ref, g_ref, mod_ref, r_ref, h_ref, idx_ref, w_ref):
    m = mod_ref[0]
    h = _norm_mod(x_ref[...], g_ref[...], m[3:4], m[4:5])
    h_ref[...] = h
    r = r_ref[...]
    h_hi = h.astype(BF16)
    h_lo = (h - h_hi.astype(F32)).astype(BF16)
    r_hi = r.astype(BF16)
    r_lo = (r - r_hi.astype(F32)).astype(BF16)
    logits = _dot(h_hi, r_hi) + (_dot(h_lo, r_hi) + _dot(h_hi, r_lo))
    lane = lax.broadcasted_iota(jnp.int32, logits.shape, 1)
    lane_f = lane.astype(F32)
    neg = jnp.float32(-jnp.inf)
    logits = jnp.where(lane < N_EXPERTS, logits, neg)
    m1 = jnp.max(logits, axis=-1, keepdims=True)
    i1 = jnp.min(jnp.where(logits == m1, lane_f, float(LANES)), axis=-1, keepdims=True)
    rest = jnp.where(lane_f == i1, neg, logits)
    m2 = jnp.max(rest, axis=-1, keepdims=True)
    i2 = jnp.min(jnp.where(rest == m2, lane_f, float(LANES)), axis=-1, keepdims=True)
    e2 = jnp.exp(m2 - m1)
    w1 = 1.0 / (1.0 + e2)
    w2 = e2 / (1.0 + e2)
    idx_ref[...] = jnp.where(lane == 0, i1, jnp.where(lane == 1, i2, 0.0)).astype(jnp.int32)
    w_ref[...] = jnp.where(lane == 0, w1, jnp.where(lane == 1, w2, 0.0))


def _router(x, g, mod, router_pad, dims, tm):
    lat, d = x.shape
    seq_tiles = dims["seq"] // tm
    return pl.pallas_call(
        _router_kernel,
        grid=(lat // tm,),
        in_specs=[pl.BlockSpec((tm, d), lambda i: (i, 0)),
                  pl.BlockSpec((1, d), lambda i: (0, 0)),
                  pl.BlockSpec((1, 6, d), lambda i: (i // seq_tiles, 0, 0)),
                  pl.BlockSpec((d, LANES), lambda i: (0, 0))],
        out_specs=[pl.BlockSpec((tm, d), lambda i: (i, 0)),
                   pl.BlockSpec((tm, LANES), lambda i: (i, 0)),
                   pl.BlockSpec((tm, LANES), lambda i: (i, 0))],
        out_shape=[jax.ShapeDtypeStruct((lat, d), F32),
                   jax.ShapeDtypeStruct((lat, LANES), jnp.int32),
                   jax.ShapeDtypeStruct((lat, LANES), F32)],
        compiler_params=_cparams(1),
        name="moe_router",
    )(x, g.reshape(1, d), mod, router_pad)


def _moe_plan(top_idx, n_items):
    e_flat = top_idx.reshape(-1)
    n_assign = e_flat.shape[0]
    onehot = (e_flat[:, None] == jnp.arange(N_EXPERTS, dtype=jnp.int32)[None, :]).astype(jnp.int32)
    csum = jnp.cumsum(onehot, axis=0)
    counts = csum[-1]
    rank = jnp.sum(csum * onehot, axis=1) - 1
    blocks = (counts + MOE_BLOCK - 1) // MOE_BLOCK
    blk_end = jnp.cumsum(blocks)
    blk_start = blk_end - blocks
    total = blk_end[-1]
    dest = blk_start[e_flat] * MOE_BLOCK + rank
    row_tok = jnp.zeros((n_items * MOE_BLOCK,), jnp.int32).at[dest].set(
        jnp.arange(n_assign, dtype=jnp.int32) // TOP_K)
    p = jnp.arange(n_items, dtype=jnp.int32)
    pc = jnp.minimum(p, total - 1)
    item_e = jnp.sum((pc[:, None] >= blk_end[None, :]).astype(jnp.int32), axis=1)
    rows_left = counts[item_e] - (pc - blk_start[item_e]) * MOE_BLOCK
    nact = jnp.clip((rows_left + MOE_SUB - 1) // MOE_SUB, 0, MOE_BLOCK // MOE_SUB)
    nact = jnp.where(p < total, nact, 0).astype(jnp.int32)
    out_blk = jnp.where(p < total, p, n_items).astype(jnp.int32)
    items = (item_e.astype(jnp.int32), pc.astype(jnp.int32), out_blk, nact)
    return dest.astype(jnp.int32), row_tok, items, (blocks, blk_start, blk_end, total)


def _moe_steps(items, runs, n_items, nj):
    _, _, _, nact = items
    blocks, blk_start, blk_end, total = runs
    s = jnp.arange(n_items * nj, dtype=jnp.int32)
    live = s < total * nj
    sc = jnp.minimum(s, total * nj - 1)
    e = jnp.sum((sc[:, None] >= (blk_end * nj)[None, :]).astype(jnp.int32), axis=1)
    t = sc - blk_start[e] * nj
    j = t // blocks[e]
    r = t % blocks[e]
    blk = blk_start[e] + jnp.where(j % 2 == 0, r, blocks[e] - 1 - r)
    out_blk = jnp.where(live, blk, n_items)
    out_j = jnp.where(live, j, 0)
    step_nact = jnp.where(live, nact[blk], 0)
    prev_e = jnp.concatenate([jnp.full((1,), -1, jnp.int32), e[:-1]])
    prev_j = jnp.concatenate([jnp.full((1,), -1, jnp.int32), j[:-1]])
    first = jnp.logical_and(live, jnp.logical_or(e != prev_e, j != prev_j))
    slot = (jnp.cumsum(first.astype(jnp.int32)) - 1) % 2
    ids = jnp.arange(N_EXPERTS, dtype=jnp.int32)
    later = jnp.logical_and(ids[None, :] > ids[:, None], blocks[None, :] > 0)
    next_expert = jnp.min(jnp.where(later, ids[None, :], N_EXPERTS), axis=1)
    wraps = j + 1 >= nj
    next_e = jnp.where(wraps, next_expert[e], e)
    next_j = jnp.where(wraps, 0, j + 1)
    has_next = jnp.logical_and(first, next_e < N_EXPERTS)
    next_e = jnp.minimum(next_e, N_EXPERTS - 1)
    return tuple(a.astype(jnp.int32) for a in (e, blk, j, out_blk, out_j, step_nact,
                                                first, slot, next_e, next_j, has_next))


def _gather_kernel(tok_ref, nact_ref, h_ref, o_ref, buf, sem):
    p = pl.program_id(0)
    last = pl.num_programs(0) - 1
    nact = nact_ref[p]
    slot = p % 2

    def copy(to_slot, r, t):
        return pltpu.make_async_copy(h_ref.at[pl.ds(t, 1)], buf.at[to_slot, pl.ds(r, 1)], sem.at[to_slot])

    def issue(item, to_slot):
        def body(grp, c):
            for u in range(GATHER_UNROLL):
                r = grp * GATHER_UNROLL + u
                copy(to_slot, r, tok_ref[item * MOE_BLOCK + r]).start()
            return c

        lax.fori_loop(0, nact_ref[item] * (MOE_SUB // GATHER_UNROLL), body, 0)

    def drain(grp, c):
        for u in range(GATHER_UNROLL):
            copy(slot, grp * GATHER_UNROLL + u, 0).wait()
        return c

    @pl.when(p == 0)
    def _():
        issue(0, 0)

    @pl.when(p < last)
    def _():
        issue(jnp.minimum(p + 1, last), 1 - slot)

    lax.fori_loop(0, nact * (MOE_SUB // GATHER_UNROLL), drain, 0)
    for s in range(MOE_BLOCK // MOE_SUB):
        rows = pl.ds(s * MOE_SUB, MOE_SUB)

        @pl.when(s < nact)
        def _():
            o_ref[rows] = buf[slot, rows].astype(BF16)

        @pl.when(s >= nact)
        def _():
            o_ref[rows] = jnp.zeros((MOE_SUB, o_ref.shape[1]), BF16)


def _moe_gather(h, row_tok, nact, n_items):
    d = h.shape[1]
    return pl.pallas_call(
        _gather_kernel,
        grid_spec=pltpu.PrefetchScalarGridSpec(
            num_scalar_prefetch=2,
            grid=(n_items,),
            in_specs=[pl.BlockSpec(memory_space=pl.ANY)],
            out_specs=pl.BlockSpec((MOE_BLOCK, d), lambda p, tok, na: (p, 0)),
            scratch_shapes=[pltpu.VMEM((2, MOE_BLOCK, d), F32), pltpu.SemaphoreType.DMA((2,))]),
        out_shape=jax.ShapeDtypeStruct((n_items * MOE_BLOCK, d), BF16),
        compiler_params=_cparams(1),
        name="moe_gather",
    )(row_tok, nact, h)


def _for_active_rows(nact, in_ref, o_ref, fn):
    n_sub = MOE_BLOCK // MOE_SUB
    for k in range(n_sub + 1):
        @pl.when(nact == k)
        def _():
            if k > 0:
                o_ref[0:k * MOE_SUB] = fn(in_ref[0:k * MOE_SUB])
            if k < n_sub:
                o_ref[k * MOE_SUB:] = jnp.zeros((MOE_BLOCK - k * MOE_SUB, o_ref.shape[1]), o_ref.dtype)


def _moe_glu_kernel(e_ref, blk_ref, j_ref, oblk_ref, oj_ref, nact_ref, first_ref, slot_ref, ne_ref, nj_ref,
                    more_ref, h_ref, wg_ref, wu_ref, o_ref, wbuf, sem):
    s = pl.program_id(0)
    nact = nact_ref[s]
    slot = slot_ref[s]
    tf = wbuf.shape[3]

    def tile_copies(e, j, to_slot):
        cols = pl.ds(pl.multiple_of(j * tf, tf), tf)
        return [pltpu.make_async_copy(w_ref.at[e, :, cols], wbuf.at[to_slot, i], sem.at[to_slot])
                for i, w_ref in enumerate((wg_ref, wu_ref))]

    @pl.when(s == 0)
    def _():
        for cp in tile_copies(e_ref[0], j_ref[0], 0):
            cp.start()

    @pl.when(first_ref[s] == 1)
    def _():
        for cp in tile_copies(e_ref[s], j_ref[s], slot):
            cp.wait()

        @pl.when(more_ref[s] == 1)
        def _():
            for cp in tile_copies(ne_ref[s], nj_ref[s], 1 - slot):
                cp.start()

    def run(h):
        gate = _dot(h, wbuf[slot, 0].astype(BF16))
        return (_silu(gate) * _dot(h, wbuf[slot, 1].astype(BF16))).astype(BF16)

    _for_active_rows(nact, h_ref, o_ref, run)


def _moe_glu(hs, wg, wu, steps, n_items, tf):
    d, f = wg.shape[1], wg.shape[2]
    nj = f // tf
    n_pre = len(steps)
    return pl.pallas_call(
        _moe_glu_kernel,
        grid_spec=pltpu.PrefetchScalarGridSpec(
            num_scalar_prefetch=n_pre,
            grid=(n_items * nj,),
            in_specs=[pl.BlockSpec((MOE_BLOCK, d), lambda s, *pre: (pre[1][s], 0)),
                      pl.BlockSpec(memory_space=pl.ANY),
                      pl.BlockSpec(memory_space=pl.ANY)],
            out_specs=pl.BlockSpec((MOE_BLOCK, tf), lambda s, *pre: (pre[3][s], pre[4][s])),
            scratch_shapes=[pltpu.VMEM((2, 2, d, tf), F32), pltpu.SemaphoreType.DMA((2,))]),
        out_shape=jax.ShapeDtypeStruct(((n_items + 1) * MOE_BLOCK, f), BF16),
        compiler_params=_cparams(1),
        name="moe_glu",
    )(*steps, hs, wg, wu)


def _moe_down_kernel(e_ref, blk_ref, oblk_ref, nact_ref, a_ref, wd_ref, o_ref):
    nact = nact_ref[pl.program_id(0)]
    _for_active_rows(nact, a_ref, o_ref, lambda a: _dot(a, wd_ref[0].astype(BF16)))


def _moe_down(a, wd, plan, n_items, tn):
    item_e, in_blk, out_blk, nact = plan
    f, d = wd.shape[1], wd.shape[2]
    nj = d // tn

    def w_map(p, j, e, b, ob, na):
        return (e[p], 0, jnp.where(na[p] > 0, j, nj - 1))

    def o_map(p, j, e, b, ob, na):
        return (ob[p], jnp.where(na[p] > 0, j, 0))

    return pl.pallas_call(
        _moe_down_kernel,
        grid_spec=pltpu.PrefetchScalarGridSpec(
            num_scalar_prefetch=4,
            grid=(n_items, nj),
            in_specs=[pl.BlockSpec((MOE_BLOCK, f), lambda p, j, e, b, ob, na: (b[p], 0)),
                      pl.BlockSpec((1, f, tn), w_map)],
            out_specs=pl.BlockSpec((MOE_BLOCK, tn), o_map)),
        out_shape=jax.ShapeDtypeStruct(((n_items + 1) * MOE_BLOCK, d), F32),
        compiler_params=_cparams(2),
        name="moe_down",
    )(item_e, in_blk, out_blk, nact, a, wd)


def _combine_kernel(dest_ref, x_ref, mod_ref, w_ref, fg_ref, y_ref, o_ref, buf, sem, *, tm):
    i = pl.program_id(0)
    last = pl.num_programs(0) - 1
    slot = i % 2

    def copy(to_slot, r, k, row):
        return pltpu.make_async_copy(y_ref.at[pl.ds(row, 1)], buf.at[to_slot, k, pl.ds(r, 1)],
                                     sem.at[to_slot])

    def issue(tile, to_slot):
        def body(grp, c):
            for u in range(GATHER_UNROLL):
                r = grp * GATHER_UNROLL + u
                for k in range(TOP_K):
                    copy(to_slot, r, k, dest_ref[(tile * tm + r) * TOP_K + k]).start()
            return c

        lax.fori_loop(0, tm // GATHER_UNROLL, body, 0)

    def drain(grp, c):
        for u in range(GATHER_UNROLL):
            for k in range(TOP_K):
                copy(slot, grp * GATHER_UNROLL + u, k, 0).wait()
        return c

    @pl.when(i == 0)
    def _():
        issue(0, 0)

    @pl.when(i < last)
    def _():
        issue(jnp.minimum(i + 1, last), 1 - slot)

    lax.fori_loop(0, tm // GATHER_UNROLL, drain, 0)
    w = w_ref[...]
    moe = w[:, 0:1] * buf[slot, 0] + w[:, 1:2] * buf[slot, 1]
    m = mod_ref[0]
    o_ref[...] = _rms(x_ref[...] + m[5:6] * moe, fg_ref[...])


def _moe_combine(dest, x, mod, top_w, final_g, ys, dims, tm):
    lat, d = x.shape
    seq_tiles = dims["seq"] // tm
    return pl.pallas_call(
        functools.partial(_combine_kernel, tm=tm),
        grid_spec=pltpu.PrefetchScalarGridSpec(
            num_scalar_prefetch=1,
            grid=(lat // tm,),
            in_specs=[pl.BlockSpec((tm, d), lambda i, dr: (i, 0)),
                      pl.BlockSpec((1, 6, d), lambda i, dr: (i // seq_tiles, 0, 0)),
                      pl.BlockSpec((tm, LANES), lambda i, dr: (i, 0)),
                      pl.BlockSpec((1, d), lambda i, dr: (0, 0)),
                      pl.BlockSpec(memory_space=pl.ANY)],
            out_specs=pl.BlockSpec((tm, d), lambda i, dr: (i, 0)),
            scratch_shapes=[pltpu.VMEM((2, TOP_K, tm, d), F32), pltpu.SemaphoreType.DMA((2,))]),
        out_shape=jax.ShapeDtypeStruct((lat, d), F32),
        compiler_params=_cparams(1),
        name="moe_combine",
    )(dest, x, mod, top_w, final_g.reshape(1, d), ys)


def kernel(x, c, ctx, c_ctx, ada_w, ada_b, norm1_g, norm2_g, ev_w_in, ev_w_out, ev_lambda, ev_subln_g, od_w_dq, od_q_norm_g, od_w_uq, od_w_dkv, od_kv_norm_g, od_w_ukv, od_w_o, ffn_w_gate, ffn_w_up, ffn_w_down, moe_router, moe_w_gate, moe_w_up, moe_w_down, final_norm_g):
    batch, seq, d = x.shape
    n_ctx = ctx.shape[1]
    depth = ada_w.shape[0]
    assert depth == 2 and batch < MOD_ROWS and seq % GRID_W == 0
    lat = batch * seq
    dims = dict(batch=batch, seq=seq, ctx=n_ctx, lat=lat)
    tm = 1024
    assert seq % tm == 0 and (batch * n_ctx) % tm == 0

    cond = jnp.concatenate([c, c_ctx[None, :], jnp.zeros((MOD_ROWS - batch - 1, d), F32)], axis=0)
    mod = _ada(cond, ada_w, ada_b).reshape(depth, MOD_ROWS, 6, d)
    x_lat, x_ctx = x.reshape(lat, d), ctx.reshape(batch * n_ctx, d)
    m_rows = lat + batch * n_ctx

    lam_init = 0.8 - 0.6 * math.exp(-0.3 * 0)
    fw = ev_w_in.shape[2] // 4
    tabs_ev = jnp.asarray(np.stack([
        _rope_tables(seq, tm, DIFF_HEAD_DIM // 4, DIFF_HEAD_DIM ** -0.5 * LOG2E),
        _rope_tables(seq, tm, DIFF_HEAD_DIM // 4, 1.0)]))
    w_in = ev_w_in[0].astype(BF16)
    qkvf = _inproj(x_lat, None, norm1_g[0], mod[0], w_in, tabs_ev, dims, tm, 1024)
    qkvf = _inproj(x_ctx, qkvf, norm1_g[0], mod[0], w_in, tabs_ev, dims, tm, 1024)
    o_attn = _diff_attn(qkvf, None, ev_lambda[0], ev_subln_g[0], lam_init, dims, 1024, True)
    o_attn = _diff_attn(qkvf, o_attn, ev_lambda[0], ev_subln_g[0], lam_init, dims, n_ctx, False)

    gw = fw // FOURIER_GROUPS
    cc_np, sc_np = _dft_cos_sin(gw)
    cc = jnp.asarray(cc_np.astype(np.float32)).astype(BF16)
    sc = jnp.asarray(sc_np.astype(np.float32)).astype(BF16)
    cn_np, sn_np = _dft_cos_sin(seq)
    cs_lat = jnp.asarray(np.concatenate([cn_np, -sn_np], axis=1).astype(np.float32)).astype(BF16)
    cx_np, sx_np = _dft_cos_sin(n_ctx)
    cs_ctx = jnp.asarray(np.concatenate([cx_np, -sx_np], axis=1).astype(np.float32)).astype(BF16)
    fm = _fourier(qkvf, None, seq, 0, batch, fw, cc, sc, cs_lat, "ev_fourier_lat")
    fm = _fourier(qkvf, fm, n_ctx, lat // n_ctx, batch, fw, cc, sc, cs_ctx, "ev_fourier_ctx")

    w_out = ev_w_out[0].astype(BF16)
    xs = _mm_res([fm, o_attn], w_out, x_lat, mod[0], 2, dims, lat, tm, 1024, "ev_outproj_lat", out_rows=m_rows)
    xs = _mm_res([fm, o_attn], w_out, x_ctx, mod[0], 2, dims, batch * n_ctx, tm, 1024, "ev_outproj_ctx", prev=xs)
    act = _glu(xs, norm2_g[0], mod[0], ffn_w_gate[0], ffn_w_up[0], dims, tm, 512)
    xs = _mm_res([act], ffn_w_down[0].astype(BF16), xs, mod[0], 5, dims, m_rows, tm, 512, "ffn_down")

    heads = od_w_ukv.shape[2] // (MLA_NOPE + MLA_V)
    q_lora = od_w_dq.shape[2]
    wd_cat = jnp.concatenate(
        [od_w_dq[0], od_w_dkv[0], jnp.zeros((d, LANES - MLA_ROPE), F32)], axis=1).astype(BF16)
    wuq = jnp.pad(od_w_uq[0].reshape(q_lora, heads, MLA_NOPE + MLA_ROPE),
                  ((0, 0), (0, 0), (0, 2 * LANES - MLA_NOPE - MLA_ROPE))).reshape(q_lora, heads * 2 * LANES)
    tm_mla = 256
    tab_q = jnp.asarray(_rope_tables(seq, tm_mla, MLA_ROPE // 4, (MLA_NOPE + MLA_ROPE) ** -0.5 * LOG2E))
    tab_k = jnp.asarray(_rope_tables(seq, tm_mla, MLA_ROPE // 4, 1.0))
    q, kv, kr = _mla_proj(xs, norm1_g[1], mod[1], wd_cat, od_q_norm_g[0], od_kv_norm_g[0],
                          wuq.astype(BF16), od_w_ukv[0].astype(BF16), tab_q, tab_k, dims, tm_mla)
    o_mla = _mla_attn(q, kv, kr, dims, 2048)
    xl = _mm_res([o_mla], od_w_o[0].astype(BF16), xs, mod[1], 2, dims, lat, tm, 1024, "od_outproj")

    router_pad = jnp.pad(moe_router[0], ((0, 0), (0, LANES - N_EXPERTS)))
    h2, top_idx, top_w = _router(xl, norm2_g[1], mod[1], router_pad, dims, 512)
    n_items = lat * TOP_K // MOE_BLOCK + N_EXPERTS
    dest, row_tok, items, runs = _moe_plan(top_idx[:, :TOP_K], n_items)
    tf = 512
    steps = _moe_steps(items, runs, n_items, moe_w_gate.shape[3] // tf)
    hs = _moe_gather(h2, row_tok, items[3], n_items)
    act = _moe_glu(hs, moe_w_gate[0], moe_w_up[0], steps, n_items, tf)
    ys = _moe_down(act, moe_w_down[0], items, n_items, 256)
    out = _moe_combine(dest, xl, mod[1], top_w, final_norm_g, ys, dims, 256)
    return out.reshape(batch, seq, d)
```

```python
import functools
import math

import numpy as np
import jax
import jax.numpy as jnp
from jax import lax
from jax.experimental import pallas as pl
from jax.experimental.pallas import tpu as pltpu

F32 = jnp.float32
BF16 = jnp.bfloat16

GRID_W = 64
NORM_EPS = 1e-6
ROPE_BASE = 10000.0
FOURIER_GROUPS = 4
DIFF_HEAD_DIM = 128
MLA_NOPE = 128
MLA_ROPE = 64
MLA_V = 128
N_EXPERTS = 8
TOP_K = 2

LANES = 128
MOD_ROWS = 8
VMEM_LIMIT = 56 * 1024 * 1024
MOE_BLOCK = 1024
MOE_SUB = 128
GATHER_UNROLL = 8
KEY_CHUNK = 256
LOG2E = math.log2(math.e)


def _cparams(n_axes):
    return pltpu.CompilerParams(dimension_semantics=("arbitrary",) * n_axes,
                                vmem_limit_bytes=VMEM_LIMIT)


def _rms(x, g):
    return x * lax.rsqrt(jnp.mean(x * x, axis=-1, keepdims=True) + NORM_EPS) * g


def _norm_mod(x, g, shift, scale):
    return _rms(x, g) * (1.0 + scale) + shift


def _silu(x):
    return x * (1.0 / (1.0 + jnp.exp(-x)))


def _dot(a, b):
    return jnp.dot(a, b, preferred_element_type=F32)


def _dot_nt(a, b):
    return lax.dot_general(a, b, (((1,), (1,)), ((), ())), preferred_element_type=F32)


def _rope_tables(seq, extra_rows, chunk, scale):
    n = np.arange(seq)
    row, col = n // GRID_W, n % GRID_W
    lane = np.arange(LANES)
    a = 2 * chunk
    inv = ROPE_BASE ** (-np.arange(0, a, 2, dtype=np.float64) / a)
    used = lane < 4 * chunk
    freq = inv[lane % chunk]
    pos = np.where(lane[None, :] < 2 * chunk, row[:, None], col[:, None]).astype(np.float64)
    ang = pos * freq[None, :]
    first = (lane // chunk) % 2 == 0
    cos = np.where(used[None, :], np.cos(ang), 0.0)
    sin = np.where(used[None, :], np.sin(ang), 0.0)
    s1 = np.where(first[None, :], -sin, 0.0)
    s2 = np.where(first[None, :], 0.0, sin)
    ident = np.zeros((3, extra_rows, LANES))
    ident[0] = used[None, :].astype(np.float64)
    tab = np.concatenate([np.stack([cos, s1, s2]), ident], axis=1) * scale
    return tab.astype(np.float32)


def _apply_rope(x, tab_ref, chunk):
    return (x * tab_ref[0] + pltpu.roll(x, LANES - chunk, 1) * tab_ref[1]
            + pltpu.roll(x, chunk, 1) * tab_ref[2])


def _dft_cos_sin(n):
    k = np.arange(n)
    ang = 2.0 * np.pi * ((k[:, None] * k[None, :]) % n) / n
    return np.cos(ang), np.sin(ang)


def _ada_kernel(s_ref, w_ref, b_ref, o_ref):
    s = _silu(s_ref[...]).astype(BF16)
    o_ref[0] = _dot(s, w_ref[0].astype(BF16)) + b_ref[0]


def _ada(cond, ada_w, ada_b):
    depth, d, n = ada_w.shape
    tn = 2048
    return pl.pallas_call(
        _ada_kernel,
        grid=(depth, n // tn),
        in_specs=[pl.BlockSpec((MOD_ROWS, d), lambda i, j: (0, 0)),
                  pl.BlockSpec((1, d, tn), lambda i, j: (i, 0, j)),
                  pl.BlockSpec((1, 1, tn), lambda i, j: (i, 0, j))],
        out_specs=pl.BlockSpec((1, MOD_ROWS, tn), lambda i, j: (i, 0, j)),
        out_shape=jax.ShapeDtypeStruct((depth, MOD_ROWS, n), F32),
        compiler_params=_cparams(2),
        name="ada",
    )(cond, ada_w, ada_b.reshape(depth, 1, n))


def _inproj_kernel(x_ref, g_ref, mod_ref, w_ref, tab_ref, *rest):
    o_ref, h_ref = rest[-2:]
    j = pl.program_id(1)
    quarter = pl.num_programs(1) // 4

    @pl.when(j == 0)
    def _():
        m = mod_ref[0]
        h_ref[...] = _norm_mod(x_ref[...], g_ref[...], m[0:1], m[1:2]).astype(BF16)

    is_rope = jnp.logical_and(j >= quarter, j < 3 * quarter)

    @pl.when(is_rope)
    def _():
        res = _dot(h_ref[...], w_ref[...].astype(BF16))
        for c in range(res.shape[1] // LANES):
            sl = slice(c * LANES, (c + 1) * LANES)
            o_ref[:, sl] = _apply_rope(res[:, sl], tab_ref.at[0], DIFF_HEAD_DIM // 4).astype(BF16)

    @pl.when(jnp.logical_not(is_rope))
    def _():
        o_ref[...] = _dot(h_ref[...], w_ref[...].astype(BF16)).astype(BF16)


def _inproj(x, prev, g, mod, w, tabs, dims, tm, tn):
    rows, d = x.shape
    seq, lat, batch = dims["seq"], dims["lat"], dims["batch"]
    m_rows = lat + batch * dims["ctx"]
    n = w.shape[1]
    nj = n // tn
    assert nj % 4 == 0
    lat_tiles, seq_tiles = lat // tm, seq // tm
    is_ctx = prev is not None
    row_blk0 = lat_tiles if is_ctx else 0

    def tab_map(i, j):
        return (jnp.where(j >= nj // 2, 1, 0), 0, seq_tiles if is_ctx else i % seq_tiles, 0)

    in_specs = [pl.BlockSpec((tm, d), lambda i, j: (i, 0)),
                pl.BlockSpec((1, d), lambda i, j: (0, 0)),
                pl.BlockSpec((1, 6, d), lambda i, j: (batch if is_ctx else i // seq_tiles, 0, 0)),
                pl.BlockSpec((d, tn), lambda i, j: (0, j)),
                pl.BlockSpec((1, 3, tm, LANES), tab_map)]
    args = [x, g.reshape(1, d), mod, w, tabs]
    aliases = {}
    if is_ctx:
        in_specs.append(pl.BlockSpec(memory_space=pl.ANY))
        args.append(prev)
        aliases = {len(args) - 1: 0}
    return pl.pallas_call(
        _inproj_kernel,
        grid=(rows // tm, nj),
        in_specs=in_specs,
        out_specs=pl.BlockSpec((tm, tn), lambda i, j: (row_blk0 + i, j)),
        out_shape=jax.ShapeDtypeStruct((m_rows, n), BF16),
        scratch_shapes=[pltpu.VMEM((tm, d), BF16)],
        input_output_aliases=aliases,
        compiler_params=_cparams(2),
        name="ev_inproj_ctx" if is_ctx else "ev_inproj_lat",
    )(*args)


def _softmax_numerators(qs, k_ref, kcols, s_scr, p_scr, want_sum):
    n_keys = k_ref.shape[0]
    chunks = [slice(c, min(c + KEY_CHUNK, n_keys)) for c in range(0, n_keys, KEY_CHUNK)]
    maxes = []
    for i, (q, cols) in enumerate(zip(qs, kcols)):
        m = None
        for ks in chunks:
            s = _dot_nt(q, k_ref[ks, cols])
            s_scr[i, :, ks] = s
            mc = jnp.max(s, axis=-1, keepdims=True)
            m = mc if m is None else jnp.maximum(m, mc)
        maxes.append(m)
    totals = []
    for i, m in enumerate(maxes):
        total = None
        for ks in chunks:
            e = jnp.exp2(s_scr[i, :, ks] - m)
            if want_sum:
                part = jnp.sum(e, axis=-1, keepdims=True)
                total = part if total is None else total + part
            p_scr[i, :, ks] = e.astype(BF16)
        totals.append(total)
    return totals


def _diff_attn_kernel(lam_ref, q_ref, *rest, lam_init, seg_rows):
    n_seg = len(seg_rows)
    k_refs, v_refs, g_ref = rest[:n_seg], rest[n_seg:2 * n_seg], rest[2 * n_seg]
    o_ref, k_scr, v_scr, s_scr, p_scr = rest[-5:]
    hd = DIFF_HEAD_DIM

    @pl.when(pl.program_id(2) == 0)
    def _():
        r0 = 0
        for k_ref, v_ref, n in zip(k_refs, v_refs, seg_rows):
            k_scr[r0:r0 + n] = k_ref[...]
            v_scr[r0:r0 + n] = v_ref[...]
            r0 += n

    lv = lam_ref[...]
    lam = (jnp.exp(jnp.sum(lv[0:1] * lv[1:2], axis=-1, keepdims=True))
           - jnp.exp(jnp.sum(lv[2:3] * lv[3:4], axis=-1, keepdims=True)) + lam_init)
    q = q_ref[...]
    cols = [slice(c * hd, (c + 1) * hd) for c in range(2)]
    totals = _softmax_numerators([q[:, c] for c in cols], k_scr, cols, s_scr, p_scr, True)
    outs = [_dot(p_scr[c], v_scr[...]) * (1.0 / totals[c]) for c in range(2)]
    o = outs[0] - lam * outs[1]
    o_ref[...] = (_rms(o, g_ref[...]) * (1.0 - lam_init)).astype(BF16)


def _diff_attn(qkvf, prev, lam_vec, subln_g, lam_init, dims, tq, latent):
    m_rows = qkvf.shape[0]
    batch, seq, ctx = dims["batch"], dims["seq"], dims["ctx"]
    hw = 2 * DIFF_HEAD_DIM
    width = qkvf.shape[1] // 4
    heads = width // hw
    ctx_blk0 = batch * seq // ctx
    qcol, kcol, vcol = width // hw, 2 * width // hw, 3 * width // hw
    if latent:
        nq, q_blk0, seg_rows = seq // tq, 0, (ctx, seq)
    else:
        assert tq == ctx
        nq, q_blk0, seg_rows = 1, ctx_blk0, (ctx,)
    n_keys = sum(seg_rows)

    def kv_specs(col):
        specs = [pl.BlockSpec((ctx, hw), lambda b, h, i: (ctx_blk0 + b, col + h))]
        if latent:
            specs.append(pl.BlockSpec((seq, hw), lambda b, h, i: (b, col + h)))
        return specs

    in_specs = ([pl.BlockSpec((4, DIFF_HEAD_DIM), lambda b, h, i: (0, 0)),
                 pl.BlockSpec((tq, hw), lambda b, h, i: (q_blk0 + b * nq + i, qcol + h))]
                + kv_specs(kcol) + kv_specs(vcol)
                + [pl.BlockSpec((1, hw), lambda b, h, i: (0, 0))])
    args = [lam_vec, qkvf] + [qkvf] * (2 * len(seg_rows)) + [subln_g.reshape(1, hw)]
    aliases = {}
    if prev is not None:
        in_specs.append(pl.BlockSpec(memory_space=pl.ANY))
        args.append(prev)
        aliases = {len(args) - 1: 0}
    kern = functools.partial(_diff_attn_kernel, lam_init=lam_init, seg_rows=seg_rows)
    return pl.pallas_call(
        kern,
        grid=(batch, heads, nq),
        in_specs=in_specs,
        out_specs=pl.BlockSpec((tq, hw), lambda b, h, i: (q_blk0 + b * nq + i, h)),
        out_shape=jax.ShapeDtypeStruct((m_rows, width), BF16),
        scratch_shapes=[pltpu.VMEM((n_keys, hw), BF16), pltpu.VMEM((n_keys, hw), BF16),
                        pltpu.VMEM((2, tq, n_keys), F32), pltpu.VMEM((2, tq, n_keys), BF16)],
        input_output_aliases=aliases,
        compiler_params=_cparams(3),
        name="ev_diff_attn_lat" if latent else "ev_diff_attn_ctx",
    )(*args)


def _fourier_kernel(u_ref, cc_ref, sc_ref, cs_ref, *rest, n, norm):
    o_ref, ab_ref = rest[-2], rest[-1]
    u = u_ref[...]
    ab_ref[0:n] = _dot(u, cc_ref[...]).astype(BF16)
    ab_ref[n:] = _dot(u, sc_ref[...]).astype(BF16)
    o_ref[...] = (_dot(cs_ref[...], ab_ref[...]) * norm).astype(BF16)


def _fourier(qkvf, prev, n, row_blk0, batch, width, cc, sc, cs, name):
    m_rows = qkvf.shape[0]
    gw = width // FOURIER_GROUPS
    kern = functools.partial(_fourier_kernel, n=n, norm=1.0 / math.sqrt(n * gw))
    in_specs = [pl.BlockSpec((n, gw), lambda b, g: (row_blk0 + b, g)),
                pl.BlockSpec((gw, gw), lambda b, g: (0, 0)),
                pl.BlockSpec((gw, gw), lambda b, g: (0, 0)),
                pl.BlockSpec((n, 2 * n), lambda b, g: (0, 0), pipeline_mode=pl.Buffered(1))]
    args = [qkvf, cc, sc, cs]
    aliases = {}
    if prev is not None:
        in_specs.append(pl.BlockSpec(memory_space=pl.ANY))
        args.append(prev)
        aliases = {4: 0}
    return pl.pallas_call(
        kern,
        grid=(batch, FOURIER_GROUPS),
        in_specs=in_specs,
        out_specs=pl.BlockSpec((n, gw), lambda b, g: (row_blk0 + b, g)),
        out_shape=jax.ShapeDtypeStruct((m_rows, width), BF16),
        scratch_shapes=[pltpu.VMEM((2 * n, gw), BF16)],
        input_output_aliases=aliases,
        compiler_params=_cparams(2),
        name=name,
    )(*args)


def _mm_res_kernel(*refs, n_a, gate_idx):
    a_refs, w_refs = refs[:n_a], refs[n_a:2 * n_a]
    x_ref, mod_ref, o_ref = refs[2 * n_a], refs[2 * n_a + 1], refs[-1]
    acc = _dot(a_refs[0][...], w_refs[0][...].astype(BF16))
    for a_ref, w_ref in zip(a_refs[1:], w_refs[1:]):
        acc = acc + _dot(a_ref[...], w_ref[...].astype(BF16))
    m = mod_ref[0]
    o_ref[...] = x_ref[...] + m[gate_idx:gate_idx + 1] * acc


def _mm_res(a_list, w, x, mod, gate_idx, dims, rows, tm, tn, name, out_rows=None, prev=None):
    d = w.shape[1]
    seq_tiles = dims["seq"] // tm
    n_a = len(a_list)
    ctx_only = prev is not None
    row_blk0 = dims["lat"] // tm if ctx_only else 0
    out_rows = prev.shape[0] if ctx_only else (out_rows or rows)

    def sel(i):
        return dims["batch"] if ctx_only else jnp.minimum(i // seq_tiles, dims["batch"])

    in_specs, w_args, k0 = [], [], 0
    for a in a_list:
        in_specs.append(pl.BlockSpec((tm, a.shape[1]), lambda i, j: (row_blk0 + i, 0)))
    for a in a_list:
        ka = a.shape[1]
        assert k0 % ka == 0
        in_specs.append(pl.BlockSpec((ka, tn), lambda i, j, kb=k0 // ka: (kb, j)))
        w_args.append(w)
        k0 += ka
    assert k0 == w.shape[0]
    in_specs += [pl.BlockSpec((tm, tn), lambda i, j: (i, j)),
                 pl.BlockSpec((1, 6, tn), lambda i, j: (sel(i), 0, j))]
    args = [*a_list, *w_args, x, mod]
    aliases = {}
    if prev is not None:
        in_specs.append(pl.BlockSpec(memory_space=pl.ANY))
        args.append(prev)
        aliases = {len(args) - 1: 0}
    return pl.pallas_call(
        functools.partial(_mm_res_kernel, n_a=n_a, gate_idx=gate_idx),
        grid=(rows // tm, d // tn),
        in_specs=in_specs,
        out_specs=pl.BlockSpec((tm, tn), lambda i, j: (row_blk0 + i, j)),
        out_shape=jax.ShapeDtypeStruct((out_rows, d), F32),
        input_output_aliases=aliases,
        compiler_params=_cparams(2),
        name=name,
    )(*args)


def _glu_kernel(x_ref, g_ref, mod_ref, wg_ref, wu_ref, o_ref, h_ref):
    @pl.when(pl.program_id(1) == 0)
    def _():
        m = mod_ref[0]
        h_ref[...] = _norm_mod(x_ref[...], g_ref[...], m[3:4], m[4:5]).astype(BF16)

    h = h_ref[...]
    gate = _dot(h, wg_ref[...].astype(BF16))
    up = _dot(h, wu_ref[...].astype(BF16))
    o_ref[...] = (_silu(gate) * up).astype(BF16)


def _glu(x, g, mod, wg, wu, dims, tm, tf):
    m_rows, d = x.shape
    f = wg.shape[1]
    seq_tiles = dims["seq"] // tm

    def sel(i):
        return jnp.minimum(i // seq_tiles, dims["batch"])

    return pl.pallas_call(
        _glu_kernel,
        grid=(m_rows // tm, f // tf),
        in_specs=[pl.BlockSpec((tm, d), lambda i, j: (i, 0)),
                  pl.BlockSpec((1, d), lambda i, j: (0, 0)),
                  pl.BlockSpec((1, 6, d), lambda i, j: (sel(i), 0, 0)),
                  pl.BlockSpec((d, tf), lambda i, j: (0, j)),
                  pl.BlockSpec((d, tf), lambda i, j: (0, j))],
        out_specs=pl.BlockSpec((tm, tf), lambda i, j: (i, j)),
        out_shape=jax.ShapeDtypeStruct((m_rows, f), BF16),
        scratch_shapes=[pltpu.VMEM((tm, d), BF16)],
        compiler_params=_cparams(2),
        name="ffn_glu",
    )(x, g.reshape(1, d), mod, wg, wu)


def _mla_proj_kernel(x_ref, g_ref, mod_ref, wd_ref, qg_ref, kvg_ref, wuq_ref, wukv_ref,
                     tq_ref, tk_ref, q_ref, kv_ref, kr_ref, *, n_lat_tiles, q_lora, kv_lora, heads):
    i = pl.program_id(0)
    m = mod_ref[0]
    h = _norm_mod(x_ref[...], g_ref[...], m[0:1], m[1:2]).astype(BF16)
    t = _dot(h, wd_ref[...])
    ckv = _rms(t[:, q_lora:q_lora + kv_lora], kvg_ref[...]).astype(BF16)
    kv_ref[...] = _dot(ckv, wukv_ref[...]).astype(BF16)
    kr = t[:, q_lora + kv_lora:]
    kr_ref[...] = _apply_rope(kr, tk_ref, MLA_ROPE // 4).astype(BF16)

    @pl.when(i < n_lat_tiles)
    def _():
        cq = _rms(t[:, :q_lora], qg_ref[...]).astype(BF16)
        q = _dot(cq, wuq_ref[...])
        scale = (MLA_NOPE + MLA_ROPE) ** -0.5 * LOG2E
        for hh in range(heads):
            c0 = hh * 2 * LANES
            q_ref[:, c0:c0 + LANES] = (q[:, c0:c0 + LANES] * scale).astype(BF16)
            q_ref[:, c0 + LANES:c0 + 2 * LANES] = _apply_rope(
                q[:, c0 + LANES:c0 + 2 * LANES], tq_ref, MLA_ROPE // 4).astype(BF16)


def _mla_proj(x, g, mod, wd, qg, kvg, wuq, wukv, tab_q, tab_k, dims, tm):
    m_rows, d = x.shape
    seq, lat = dims["seq"], dims["lat"]
    q_lora, kv_lora = qg.shape[0], kvg.shape[0]
    heads = wukv.shape[1] // (MLA_NOPE + MLA_V)
    lat_tiles, seq_tiles = lat // tm, seq // tm

    def sel(i):
        return jnp.minimum(i // seq_tiles, dims["batch"])

    def tab_map(i):
        return (0, jnp.where(i < lat_tiles, i % seq_tiles, seq_tiles), 0)

    kern = functools.partial(_mla_proj_kernel, n_lat_tiles=lat_tiles, q_lora=q_lora, kv_lora=kv_lora,
                             heads=heads)
    const = lambda i: (0, 0)
    return pl.pallas_call(
        kern,
        grid=(m_rows // tm,),
        in_specs=[pl.BlockSpec((tm, d), lambda i: (i, 0)),
                  pl.BlockSpec((1, d), const),
                  pl.BlockSpec((1, 6, d), lambda i: (sel(i), 0, 0)),
                  pl.BlockSpec(wd.shape, const),
                  pl.BlockSpec((1, q_lora), const),
                  pl.BlockSpec((1, kv_lora), const),
                  pl.BlockSpec(wuq.shape, const),
                  pl.BlockSpec(wukv.shape, const),
                  pl.BlockSpec((3, tm, LANES), tab_map),
                  pl.BlockSpec((3, tm, LANES), tab_map)],
        out_specs=[pl.BlockSpec((tm, wuq.shape[1]), lambda i: (jnp.minimum(i, lat_tiles - 1), 0)),
                   pl.BlockSpec((tm, wukv.shape[1]), lambda i: (i, 0)),
                   pl.BlockSpec((tm, LANES), lambda i: (i, 0))],
        out_shape=[jax.ShapeDtypeStruct((lat, wuq.shape[1]), BF16),
                   jax.ShapeDtypeStruct((m_rows, wukv.shape[1]), BF16),
                   jax.ShapeDtypeStruct((m_rows, LANES), BF16)],
        compiler_params=_cparams(1),
        name="od_mla_proj",
    )(x, g.reshape(1, d), mod, wd, qg.reshape(1, q_lora), kvg.reshape(1, kv_lora), wuq, wukv, tab_q, tab_k)


def _mla_attn_kernel(q_ref, knc_ref, knl_ref, krc_ref, krl_ref, vc_ref, vl_ref, o_ref,
                     k_scr, v_scr, s_scr, p_scr, *, n_ctx):
    @pl.when(pl.program_id(2) == 0)
    def _():
        k_scr[0:n_ctx, 0:LANES] = knc_ref[...]
        k_scr[n_ctx:, 0:LANES] = knl_ref[...]
        k_scr[0:n_ctx, LANES:] = krc_ref[...]
        k_scr[n_ctx:, LANES:] = krl_ref[...]
        v_scr[0:n_ctx, 0:LANES] = vc_ref[...]
        v_scr[n_ctx:, 0:LANES] = vl_ref[...]
        v_scr[:, LANES:] = jnp.ones((v_scr.shape[0], LANES), BF16)

    half = q_ref.shape[0] // 2
    halves = [slice(0, half), slice(half, 2 * half)]
    _softmax_numerators([q_ref[r] for r in halves], k_scr, [slice(None)] * 2, s_scr, p_scr, False)
    for i, r in enumerate(halves):
        acc = _dot(p_scr[i], v_scr[...])
        o_ref[r] = (acc[:, :LANES] / acc[:, LANES:LANES + 1]).astype(BF16)


def _mla_attn(q, kv, kr, dims, tq):
    batch, seq, ctx, lat = dims["batch"], dims["seq"], dims["ctx"], dims["lat"]
    heads = q.shape[1] // (2 * LANES)
    nq = seq // tq
    ctx_blk0 = lat // ctx
    return pl.pallas_call(
        functools.partial(_mla_attn_kernel, n_ctx=ctx),
        grid=(batch, heads, nq),
        in_specs=[pl.BlockSpec((tq, 2 * LANES), lambda b, h, i: (b * nq + i, h)),
                  pl.BlockSpec((ctx, LANES), lambda b, h, i: (ctx_blk0 + b, 2 * h)),
                  pl.BlockSpec((seq, LANES), lambda b, h, i: (b, 2 * h)),
                  pl.BlockSpec((ctx, LANES), lambda b, h, i: (ctx_blk0 + b, 0)),
                  pl.BlockSpec((seq, LANES), lambda b, h, i: (b, 0)),
                  pl.BlockSpec((ctx, LANES), lambda b, h, i: (ctx_blk0 + b, 2 * h + 1)),
                  pl.BlockSpec((seq, LANES), lambda b, h, i: (b, 2 * h + 1))],
        out_specs=pl.BlockSpec((tq, LANES), lambda b, h, i: (b * nq + i, h)),
        out_shape=jax.ShapeDtypeStruct((lat, heads * LANES), BF16),
        scratch_shapes=[pltpu.VMEM((ctx + seq, 2 * LANES), BF16), pltpu.VMEM((ctx + seq, 2 * LANES), BF16),
                        pltpu.VMEM((2, tq // 2, ctx + seq), F32), pltpu.VMEM((2, tq // 2, ctx + seq), BF16)],
        compiler_params=_cparams(3),
        name="od_mla_attn",
    )(q, kv, kv, kr, kr, kv, kv)


def _router_kernel(x_ref, g_ref, mod_ref, r_ref, h_ref, idx_ref, w_ref):
    m = mod_ref[0]
    h = _norm_mod(x_ref[...], g_ref[...], m[3:4], m[4:5])
    h_ref[...] = h
    r = r_ref[...]
    h_hi = h.astype(BF16)
    h_lo = (h - h_hi.astype(F32)).astype(BF16)
    r_hi = r.astype(BF16)
    r_lo = (r - r_hi.astype(F32)).astype(BF16)
    rows = h.shape[0]
    prod = _dot(jnp.concatenate([h_hi, h_lo], axis=0), jnp.concatenate([r_hi, r_lo], axis=1))
    logits = prod[:rows, :LANES] + (prod[rows:, :LANES] + prod[:rows, LANES:])
    lane = lax.broadcasted_iota(jnp.int32, logits.shape, 1)
    lane_f = lane.astype(F32)
    neg = jnp.float32(-jnp.inf)
    logits = jnp.where(lane < N_EXPERTS, logits, neg)
    m1 = jnp.max(logits, axis=-1, keepdims=True)
    i1 = jnp.min(jnp.where(logits == m1, lane_f, float(LANES)), axis=-1, keepdims=True)
    rest = jnp.where(lane_f == i1, neg, logits)
    m2 = jnp.max(rest, axis=-1, keepdims=True)
    i2 = jnp.min(jnp.where(rest == m2, lane_f, float(LANES)), axis=-1, keepdims=True)
    e2 = jnp.exp(m2 - m1)
    w1 = 1.0 / (1.0 + e2)
    w2 = e2 / (1.0 + e2)
    idx_ref[...] = jnp.where(lane == 0, i1, jnp.where(lane == 1, i2, 0.0)).astype(jnp.int32)
    w_ref[...] = jnp.where(lane == 0, w1, jnp.where(lane == 1, w2, 0.0))


def _router(x, g, mod, router_pad, dims, tm):
    lat, d = x.shape
    seq_tiles = dims["seq"] // tm
    return pl.pallas_call(
        _router_kernel,
        grid=(lat // tm,),
        in_specs=[pl.BlockSpec((tm, d), lambda i: (i, 0)),
                  pl.BlockSpec((1, d), lambda i: (0, 0)),
                  pl.BlockSpec((1, 6, d), lambda i: (i // seq_tiles, 0, 0)),
                  pl.BlockSpec((d, LANES), lambda i: (0, 0))],
        out_specs=[pl.BlockSpec((tm, d), lambda i: (i, 0)),
                   pl.BlockSpec((tm, LANES), lambda i: (i, 0)),
                   pl.BlockSpec((tm, LANES), lambda i: (i, 0))],
        out_shape=[jax.ShapeDtypeStruct((lat, d), F32),
                   jax.ShapeDtypeStruct((lat, LANES), jnp.int32),
                   jax.ShapeDtypeStruct((lat, LANES), F32)],
        compiler_params=_cparams(1),
        name="moe_router",
    )(x, g.reshape(1, d), mod, router_pad)


def _moe_plan(top_idx, n_items):
    e_flat = top_idx.reshape(-1)
    n_assign = e_flat.shape[0]
    onehot = (e_flat[:, None] == jnp.arange(N_EXPERTS, dtype=jnp.int32)[None, :]).astype(jnp.int32)
    csum = jnp.cumsum(onehot, axis=0)
    counts = csum[-1]
    rank = jnp.sum(csum * onehot, axis=1) - 1
    blocks = (counts + MOE_BLOCK - 1) // MOE_BLOCK
    blk_end = jnp.cumsum(blocks)
    blk_start = blk_end - blocks
    total = blk_end[-1]
    dest = blk_start[e_flat] * MOE_BLOCK + rank
    row_tok = jnp.zeros((n_items * MOE_BLOCK,), jnp.int32).at[dest].set(
        jnp.arange(n_assign, dtype=jnp.int32) // TOP_K)
    p = jnp.arange(n_items, dtype=jnp.int32)
    pc = jnp.minimum(p, total - 1)
    item_e = jnp.sum((pc[:, None] >= blk_end[None, :]).astype(jnp.int32), axis=1)
    rows_left = counts[item_e] - (pc - blk_start[item_e]) * MOE_BLOCK
    nact = jnp.clip((rows_left + MOE_SUB - 1) // MOE_SUB, 0, MOE_BLOCK // MOE_SUB)
    nact = jnp.where(p < total, nact, 0).astype(jnp.int32)
    out_blk = jnp.where(p < total, p, n_items).astype(jnp.int32)
    items = (item_e.astype(jnp.int32), pc.astype(jnp.int32), out_blk, nact)
    return dest.astype(jnp.int32), row_tok, items, (blocks, blk_start, blk_end, total)


def _moe_steps(items, runs, n_items, nj):
    _, _, _, nact = items
    blocks, blk_start, blk_end, total = runs
    s = jnp.arange(n_items * nj, dtype=jnp.int32)
    live = s < total * nj
    sc = jnp.minimum(s, total * nj - 1)
    e = jnp.sum((sc[:, None] >= (blk_end * nj)[None, :]).astype(jnp.int32), axis=1)
    t = sc - blk_start[e] * nj
    j = t // blocks[e]
    r = t % blocks[e]
    blk = blk_start[e] + jnp.where(j % 2 == 0, r, blocks[e] - 1 - r)
    out_blk = jnp.where(live, blk, n_items)
    out_j = jnp.where(live, j, 0)
    step_nact = jnp.where(live, nact[blk], 0)
    prev_e = jnp.concatenate([jnp.full((1,), -1, jnp.int32), e[:-1]])
    prev_j = jnp.concatenate([jnp.full((1,), -1, jnp.int32), j[:-1]])
    first = jnp.logical_and(live, jnp.logical_or(e != prev_e, j != prev_j))
    slot = (jnp.cumsum(first.astype(jnp.int32)) - 1) % 2
    ids = jnp.arange(N_EXPERTS, dtype=jnp.int32)
    later = jnp.logical_and(ids[None, :] > ids[:, None], blocks[None, :] > 0)
    next_expert = jnp.min(jnp.where(later, ids[None, :], N_EXPERTS), axis=1)
    wraps = j + 1 >= nj
    next_e = jnp.where(wraps, next_expert[e], e)
    next_j = jnp.where(wraps, 0, j + 1)
    has_next = jnp.logical_and(first, next_e < N_EXPERTS)
    next_e = jnp.minimum(next_e, N_EXPERTS - 1)
    return tuple(a.astype(jnp.int32) for a in (e, blk, j, out_blk, out_j, step_nact,
                                                first, slot, next_e, next_j, has_next))


def _gather_kernel(tok_ref, nact_ref, h_ref, o_ref, buf, sem):
    p = pl.program_id(0)
    last = pl.num_programs(0) - 1
    nact = nact_ref[p]
    slot = p % 2

    def copy(to_slot, r, t):
        return pltpu.make_async_copy(h_ref.at[pl.ds(t, 1)], buf.at[to_slot, pl.ds(r, 1)], sem.at[to_slot])

    def issue(item, to_slot):
        def body(grp, c):
            for u in range(GATHER_UNROLL):
                r = grp * GATHER_UNROLL + u
                copy(to_slot, r, tok_ref[item * MOE_BLOCK + r]).start()
            return c

        lax.fori_loop(0, nact_ref[item] * (MOE_SUB // GATHER_UNROLL), body, 0)

    def drain(grp, c):
        for u in range(GATHER_UNROLL):
            copy(slot, grp * GATHER_UNROLL + u, 0).wait()
        return c

    @pl.when(p == 0)
    def _():
        issue(0, 0)

    @pl.when(p < last)
    def _():
        issue(jnp.minimum(p + 1, last), 1 - slot)

    lax.fori_loop(0, nact * (MOE_SUB // GATHER_UNROLL), drain, 0)
    for s in range(MOE_BLOCK // MOE_SUB):
        rows = pl.ds(s * MOE_SUB, MOE_SUB)

        @pl.when(s < nact)
        def _():
            o_ref[rows] = buf[slot, rows].astype(BF16)

        @pl.when(s >= nact)
        def _():
            o_ref[rows] = jnp.zeros((MOE_SUB, o_ref.shape[1]), BF16)


def _moe_gather(h, row_tok, nact, n_items):
    d = h.shape[1]
    return pl.pallas_call(
        _gather_kernel,
        grid_spec=pltpu.PrefetchScalarGridSpec(
            num_scalar_prefetch=2,
            grid=(n_items,),
            in_specs=[pl.BlockSpec(memory_space=pl.ANY)],
            out_specs=pl.BlockSpec((MOE_BLOCK, d), lambda p, tok, na: (p, 0)),
            scratch_shapes=[pltpu.VMEM((2, MOE_BLOCK, d), F32), pltpu.SemaphoreType.DMA((2,))]),
        out_shape=jax.ShapeDtypeStruct((n_items * MOE_BLOCK, d), BF16),
        compiler_params=_cparams(1),
        name="moe_gather",
    )(row_tok, nact, h)


def _for_active_rows(nact, in_ref, o_ref, fn):
    n_sub = MOE_BLOCK // MOE_SUB
    for k in range(n_sub + 1):
        @pl.when(nact == k)
        def _():
            if k > 0:
                o_ref[0:k * MOE_SUB] = fn(in_ref[0:k * MOE_SUB])
            if k < n_sub:
                o_ref[k * MOE_SUB:] = jnp.zeros((MOE_BLOCK - k * MOE_SUB, o_ref.shape[1]), o_ref.dtype)


def _moe_glu_kernel(e_ref, blk_ref, j_ref, oblk_ref, oj_ref, nact_ref, first_ref, slot_ref, ne_ref, nj_ref,
                    more_ref, h_ref, wg_ref, wu_ref, o_ref, wbuf, sem):
    s = pl.program_id(0)
    nact = nact_ref[s]
    slot = slot_ref[s]
    tf = wbuf.shape[3]

    def tile_copies(e, j, to_slot):
        cols = pl.ds(pl.multiple_of(j * tf, tf), tf)
        return [pltpu.make_async_copy(w_ref.at[e, :, cols], wbuf.at[to_slot, i], sem.at[to_slot])
                for i, w_ref in enumerate((wg_ref, wu_ref))]

    @pl.when(s == 0)
    def _():
        for cp in tile_copies(e_ref[0], j_ref[0], 0):
            cp.start()

    @pl.when(first_ref[s] == 1)
    def _():
        for cp in tile_copies(e_ref[s], j_ref[s], slot):
            cp.wait()

        @pl.when(more_ref[s] == 1)
        def _():
            for cp in tile_copies(ne_ref[s], nj_ref[s], 1 - slot):
                cp.start()

    def run(h):
        gate = _dot(h, wbuf[slot, 0].astype(BF16))
        return (_silu(gate) * _dot(h, wbuf[slot, 1].astype(BF16))).astype(BF16)

    _for_active_rows(nact, h_ref, o_ref, run)


def _moe_glu(hs, wg, wu, steps, n_items, tf):
    d, f = wg.shape[1], wg.shape[2]
    nj = f // tf
    n_pre = len(steps)
    return pl.pallas_call(
        _moe_glu_kernel,
        grid_spec=pltpu.PrefetchScalarGridSpec(
            num_scalar_prefetch=n_pre,
            grid=(n_items * nj,),
            in_specs=[pl.BlockSpec((MOE_BLOCK, d), lambda s, *pre: (pre[1][s], 0)),
                      pl.BlockSpec(memory_space=pl.ANY),
                      pl.BlockSpec(memory_space=pl.ANY)],
            out_specs=pl.BlockSpec((MOE_BLOCK, tf), lambda s, *pre: (pre[3][s], pre[4][s])),
            scratch_shapes=[pltpu.VMEM((2, 2, d, tf), F32), pltpu.SemaphoreType.DMA((2,))]),
        out_shape=jax.ShapeDtypeStruct(((n_items + 1) * MOE_BLOCK, f), BF16),
        compiler_params=_cparams(1),
        name="moe_glu",
    )(*steps, hs, wg, wu)


def _moe_down_kernel(e_ref, blk_ref, oblk_ref, nact_ref, a_ref, wd_ref, o_ref):
    nact = nact_ref[pl.program_id(0)]
    _for_active_rows(nact, a_ref, o_ref, lambda a: _dot(a, wd_ref[0].astype(BF16)))


def _moe_down(a, wd, plan, n_items, tn):
    item_e, in_blk, out_blk, nact = plan
    f, d = wd.shape[1], wd.shape[2]
    nj = d // tn

    def w_map(p, j, e, b, ob, na):
        return (e[p], 0, jnp.where(na[p] > 0, j, nj - 1))

    def o_map(p, j, e, b, ob, na):
        return (ob[p], jnp.where(na[p] > 0, j, 0))

    return pl.pallas_call(
        _moe_down_kernel,
        grid_spec=pltpu.PrefetchScalarGridSpec(
            num_scalar_prefetch=4,
            grid=(n_items, nj),
            in_specs=[pl.BlockSpec((MOE_BLOCK, f), lambda p, j, e, b, ob, na: (b[p], 0)),
                      pl.BlockSpec((1, f, tn), w_map)],
            out_specs=pl.BlockSpec((MOE_BLOCK, tn), o_map)),
        out_shape=jax.ShapeDtypeStruct(((n_items + 1) * MOE_BLOCK, d), F32),
        compiler_params=_cparams(2),
        name="moe_down",
    )(item_e, in_blk, out_blk, nact, a, wd)


def _combine_kernel(dest_ref, x_ref, mod_ref, w_ref, fg_ref, y_ref, o_ref, buf, sem, *, tm):
    i = pl.program_id(0)
    last = pl.num_programs(0) - 1
    slot = i % 2

    def copy(to_slot, r, k, row):
        return pltpu.make_async_copy(y_ref.at[pl.ds(row, 1)], buf.at[to_slot, k, pl.ds(r, 1)],
                                     sem.at[to_slot])

    def issue(tile, to_slot):
        def body(grp, c):
            for u in range(GATHER_UNROLL):
                r = grp * GATHER_UNROLL + u
                for k in range(TOP_K):
                    copy(to_slot, r, k, dest_ref[(tile * tm + r) * TOP_K + k]).start()
            return c

        lax.fori_loop(0, tm // GATHER_UNROLL, body, 0)

    def drain(grp, c):
        for u in range(GATHER_UNROLL):
            for k in range(TOP_K):
                copy(slot, grp * GATHER_UNROLL + u, k, 0).wait()
        return c

    @pl.when(i == 0)
    def _():
        issue(0, 0)

    @pl.when(i < last)
    def _():
        issue(jnp.minimum(i + 1, last), 1 - slot)

    lax.fori_loop(0, tm // GATHER_UNROLL, drain, 0)
    w = w_ref[...]
    moe = w[:, 0:1] * buf[slot, 0] + w[:, 1:2] * buf[slot, 1]
    m = mod_ref[0]
    o_ref[...] = _rms(x_ref[...] + m[5:6] * moe, fg_ref[...])


def _moe_combine(dest, x, mod, top_w, final_g, ys, dims, tm):
    lat, d = x.shape
    seq_tiles = dims["seq"] // tm
    return pl.pallas_call(
        functools.partial(_combine_kernel, tm=tm),
        grid_spec=pltpu.PrefetchScalarGridSpec(
            num_scalar_prefetch=1,
            grid=(lat // tm,),
            in_specs=[pl.BlockSpec((tm, d), lambda i, dr: (i, 0)),
                      pl.BlockSpec((1, 6, d), lambda i, dr: (i // seq_tiles, 0, 0)),
                      pl.BlockSpec((tm, LANES), lambda i, dr: (i, 0)),
                      pl.BlockSpec((1, d), lambda i, dr: (0, 0)),
                      pl.BlockSpec(memory_space=pl.ANY)],
            out_specs=pl.BlockSpec((tm, d), lambda i, dr: (i, 0)),
            scratch_shapes=[pltpu.VMEM((2, TOP_K, tm, d), F32), pltpu.SemaphoreType.DMA((2,))]),
        out_shape=jax.ShapeDtypeStruct((lat, d), F32),
        compiler_params=_cparams(1),
        name="moe_combine",
    )(dest, x, mod, top_w, final_g.reshape(1, d), ys)


def kernel(x, c, ctx, c_ctx, ada_w, ada_b, norm1_g, norm2_g, ev_w_in, ev_w_out, ev_lambda, ev_subln_g, od_w_dq, od_q_norm_g, od_w_uq, od_w_dkv, od_kv_norm_g, od_w_ukv, od_w_o, ffn_w_gate, ffn_w_up, ffn_w_down, moe_router, moe_w_gate, moe_w_up, moe_w_down, final_norm_g):
    batch, seq, d = x.shape
    n_ctx = ctx.shape[1]
    depth = ada_w.shape[0]
    assert depth == 2 and batch < MOD_ROWS and seq % GRID_W == 0
    lat = batch * seq
    dims = dict(batch=batch, seq=seq, ctx=n_ctx, lat=lat)
    tm = 1024
    assert seq % tm == 0 and (batch * n_ctx) % tm == 0

    cond = jnp.concatenate([c, c_ctx[None, :], jnp.zeros((MOD_ROWS - batch - 1, d), F32)], axis=0)
    mod = _ada(cond, ada_w, ada_b).reshape(depth, MOD_ROWS, 6, d)
    x_lat, x_ctx = x.reshape(lat, d), ctx.reshape(batch * n_ctx, d)
    m_rows = lat + batch * n_ctx

    lam_init = 0.8 - 0.6 * math.exp(-0.3 * 0)
    fw = ev_w_in.shape[2] // 4
    tabs_ev = jnp.asarray(np.stack([
        _rope_tables(seq, tm, DIFF_HEAD_DIM // 4, DIFF_HEAD_DIM ** -0.5 * LOG2E),
        _rope_tables(seq, tm, DIFF_HEAD_DIM // 4, 1.0)]))
    w_in = ev_w_in[0].astype(BF16)
    qkvf = _inproj(x_lat, None, norm1_g[0], mod[0], w_in, tabs_ev, dims, tm, 1024)
    qkvf = _inproj(x_ctx, qkvf, norm1_g[0], mod[0], w_in, tabs_ev, dims, tm, 1024)
    o_attn = _diff_attn(qkvf, None, ev_lambda[0], ev_subln_g[0], lam_init, dims, 1024, True)
    o_attn = _diff_attn(qkvf, o_attn, ev_lambda[0], ev_subln_g[0], lam_init, dims, n_ctx, False)

    gw = fw // FOURIER_GROUPS
    cc_np, sc_np = _dft_cos_sin(gw)
    cc = jnp.asarray(cc_np.astype(np.float32)).astype(BF16)
    sc = jnp.asarray(sc_np.astype(np.float32)).astype(BF16)
    cn_np, sn_np = _dft_cos_sin(seq)
    cs_lat = jnp.asarray(np.concatenate([cn_np, -sn_np], axis=1).astype(np.float32)).astype(BF16)
    cx_np, sx_np = _dft_cos_sin(n_ctx)
    cs_ctx = jnp.asarray(np.concatenate([cx_np, -sx_np], axis=1).astype(np.float32)).astype(BF16)
    fm = _fourier(qkvf, None, seq, 0, batch, fw, cc, sc, cs_lat, "ev_fourier_lat")
    fm = _fourier(qkvf, fm, n_ctx, lat // n_ctx, batch, fw, cc, sc, cs_ctx, "ev_fourier_ctx")

    w_out = ev_w_out[0].astype(BF16)
    xs = _mm_res([fm, o_attn], w_out, x_lat, mod[0], 2, dims, lat, tm, 1024, "ev_outproj_lat", out_rows=m_rows)
    xs = _mm_res([fm, o_attn], w_out, x_ctx, mod[0], 2, dims, batch * n_ctx, tm, 1024, "ev_outproj_ctx", prev=xs)
    act = _glu(xs, norm2_g[0], mod[0], ffn_w_gate[0], ffn_w_up[0], dims, tm, 512)
    xs = _mm_res([act], ffn_w_down[0].astype(BF16), xs, mod[0], 5, dims, m_rows, tm, 512, "ffn_down")

    heads = od_w_ukv.shape[2] // (MLA_NOPE + MLA_V)
    q_lora = od_w_dq.shape[2]
    wd_cat = jnp.concatenate(
        [od_w_dq[0], od_w_dkv[0], jnp.zeros((d, LANES - MLA_ROPE), F32)], axis=1).astype(BF16)
    wuq = jnp.pad(od_w_uq[0].reshape(q_lora, heads, MLA_NOPE + MLA_ROPE),
                  ((0, 0), (0, 0), (0, 2 * LANES - MLA_NOPE - MLA_ROPE))).reshape(q_lora, heads * 2 * LANES)
    tm_mla = 256
    tab_q = jnp.asarray(_rope_tables(seq, tm_mla, MLA_ROPE // 4, (MLA_NOPE + MLA_ROPE) ** -0.5 * LOG2E))
    tab_k = jnp.asarray(_rope_tables(seq, tm_mla, MLA_ROPE // 4, 1.0))
    q, kv, kr = _mla_proj(xs, norm1_g[1], mod[1], wd_cat, od_q_norm_g[0], od_kv_norm_g[0],
                          wuq.astype(BF16), od_w_ukv[0].astype(BF16), tab_q, tab_k, dims, tm_mla)
    o_mla = _mla_attn(q, kv, kr, dims, 2048)
    xl = _mm_res([o_mla], od_w_o[0].astype(BF16), xs, mod[1], 2, dims, lat, tm, 1024, "od_outproj")

    router_pad = jnp.pad(moe_router[0], ((0, 0), (0, LANES - N_EXPERTS)))
    h2, top_idx, top_w = _router(xl, norm2_g[1], mod[1], router_pad, dims, 512)
    n_items = lat * TOP_K // MOE_BLOCK + N_EXPERTS
    dest, row_tok, items, runs = _moe_plan(top_idx[:, :TOP_K], n_items)
    tf = 512
    steps = _moe_steps(items, runs, n_items, moe_w_gate.shape[3] // tf)
    hs = _moe_gather(h2, row_tok, items[3], n_items)
    act = _moe_glu(hs, moe_w_gate[0], moe_w_up[0], steps, n_items, tf)
    ys = _moe_down(act, moe_w_down[0], items, n_items, 256)
    out = _moe_combine(dest, xl, mod[1], top_w, final_norm_g, ys, dims, 256)
    return out.reshape(batch, seq, d)
```

```python
import functools
import math

import numpy as np
import jax
import jax.numpy as jnp
from jax import lax
from jax.experimental import pallas as pl
from jax.experimental.pallas import tpu as pltpu

F32 = jnp.float32
BF16 = jnp.bfloat16

GRID_W = 64
NORM_EPS = 1e-6
ROPE_BASE = 10000.0
FOURIER_GROUPS = 4
DIFF_HEAD_DIM = 128
MLA_NOPE = 128
MLA_ROPE = 64
MLA_V = 128
N_EXPERTS = 8
TOP_K = 2

LANES = 128
MOD_ROWS = 8
VMEM_LIMIT = 56 * 1024 * 1024
MOE_BLOCK = 1024
MOE_SUB = 128
GATHER_UNROLL = 8
KEY_CHUNK = 256
LOG2E = math.log2(math.e)


def _cparams(n_axes):
    return pltpu.CompilerParams(dimension_semantics=("arbitrary",) * n_axes,
                                vmem_limit_bytes=VMEM_LIMIT)


def _rms(x, g):
    return x * lax.rsqrt(jnp.mean(x * x, axis=-1, keepdims=True) + NORM_EPS) * g


def _norm_mod(x, g, shift, scale):
    return _rms(x, g) * (1.0 + scale) + shift


def _silu(x):
    return x * (1.0 / (1.0 + jnp.exp(-x)))


def _dot(a, b):
    return jnp.dot(a, b, preferred_element_type=F32)


def _dot_nt(a, b):
    return lax.dot_general(a, b, (((1,), (1,)), ((), ())), preferred_element_type=F32)


def _rope_tables(seq, extra_rows, chunk, scale):
    n = np.arange(seq)
    row, col = n // GRID_W, n % GRID_W
    lane = np.arange(LANES)
    a = 2 * chunk
    inv = ROPE_BASE ** (-np.arange(0, a, 2, dtype=np.float64) / a)
    used = lane < 4 * chunk
    freq = inv[lane % chunk]
    pos = np.where(lane[None, :] < 2 * chunk, row[:, None], col[:, None]).astype(np.float64)
    ang = pos * freq[None, :]
    first = (lane // chunk) % 2 == 0
    cos = np.where(used[None, :], np.cos(ang), 0.0)
    sin = np.where(used[None, :], np.sin(ang), 0.0)
    s1 = np.where(first[None, :], -sin, 0.0)
    s2 = np.where(first[None, :], 0.0, sin)
    ident = np.zeros((3, extra_rows, LANES))
    ident[0] = used[None, :].astype(np.float64)
    tab = np.concatenate([np.stack([cos, s1, s2]), ident], axis=1) * scale
    return tab.astype(np.float32)


def _apply_rope(x, tab_ref, chunk):
    return (x * tab_ref[0] + pltpu.roll(x, LANES - chunk, 1) * tab_ref[1]
            + pltpu.roll(x, chunk, 1) * tab_ref[2])


def _dft_cos_sin(n):
    k = np.arange(n)
    ang = 2.0 * np.pi * ((k[:, None] * k[None, :]) % n) / n
    return np.cos(ang), np.sin(ang)


def _ada_kernel(s_ref, w_ref, b_ref, o_ref):
    s = _silu(s_ref[...]).astype(BF16)
    o_ref[0] = _dot(s, w_ref[0].astype(BF16)) + b_ref[0]


def _ada(cond, ada_w, ada_b):
    depth, d, n = ada_w.shape
    tn = 2048
    return pl.pallas_call(
        _ada_kernel,
        grid=(depth, n // tn),
        in_specs=[pl.BlockSpec((MOD_ROWS, d), lambda i, j: (0, 0)),
                  pl.BlockSpec((1, d, tn), lambda i, j: (i, 0, j)),
                  pl.BlockSpec((1, 1, tn), lambda i, j: (i, 0, j))],
        out_specs=pl.BlockSpec((1, MOD_ROWS, tn), lambda i, j: (i, 0, j)),
        out_shape=jax.ShapeDtypeStruct((depth, MOD_ROWS, n), F32),
        compiler_params=_cparams(2),
        name="ada",
    )(cond, ada_w, ada_b.reshape(depth, 1, n))


def _inproj_kernel(x_ref, g_ref, mod_ref, w_ref, tab_ref, *rest):
    o_ref, h_ref = rest[-2:]
    j = pl.program_id(1)
    quarter = pl.num_programs(1) // 4

    @pl.when(j == 0)
    def _():
        m = mod_ref[0]
        h_ref[...] = _norm_mod(x_ref[...], g_ref[...], m[0:1], m[1:2]).astype(BF16)

    is_rope = jnp.logical_and(j >= quarter, j < 3 * quarter)

    @pl.when(is_rope)
    def _():
        res = _dot(h_ref[...], w_ref[...].astype(BF16))
        for c in range(res.shape[1] // LANES):
            sl = slice(c * LANES, (c + 1) * LANES)
            o_ref[:, sl] = _apply_rope(res[:, sl], tab_ref.at[0], DIFF_HEAD_DIM // 4).astype(BF16)

    @pl.when(jnp.logical_not(is_rope))
    def _():
        o_ref[...] = _dot(h_ref[...], w_ref[...].astype(BF16)).astype(BF16)


def _inproj(x, prev, g, mod, w, tabs, dims, tm, tn):
    rows, d = x.shape
    seq, lat, batch = dims["seq"], dims["lat"], dims["batch"]
    m_rows = lat + batch * dims["ctx"]
    n = w.shape[1]
    nj = n // tn
    assert nj % 4 == 0
    lat_tiles, seq_tiles = lat // tm, seq // tm
    is_ctx = prev is not None
    row_blk0 = lat_tiles if is_ctx else 0

    def tab_map(i, j):
        return (jnp.where(j >= nj // 2, 1, 0), 0, seq_tiles if is_ctx else i % seq_tiles, 0)

    in_specs = [pl.BlockSpec((tm, d), lambda i, j: (i, 0)),
                pl.BlockSpec((1, d), lambda i, j: (0, 0)),
                pl.BlockSpec((1, 6, d), lambda i, j: (batch if is_ctx else i // seq_tiles, 0, 0)),
                pl.BlockSpec((d, tn), lambda i, j: (0, j)),
                pl.BlockSpec((1, 3, tm, LANES), tab_map)]
    args = [x, g.reshape(1, d), mod, w, tabs]
    aliases = {}
    if is_ctx:
        in_specs.append(pl.BlockSpec(memory_space=pl.ANY))
        args.append(prev)
        aliases = {len(args) - 1: 0}
    return pl.pallas_call(
        _inproj_kernel,
        grid=(rows // tm, nj),
        in_specs=in_specs,
        out_specs=pl.BlockSpec((tm, tn), lambda i, j: (row_blk0 + i, j)),
        out_shape=jax.ShapeDtypeStruct((m_rows, n), BF16),
        scratch_shapes=[pltpu.VMEM((tm, d), BF16)],
        input_output_aliases=aliases,
        compiler_params=_cparams(2),
        name="ev_inproj_ctx" if is_ctx else "ev_inproj_lat",
    )(*args)


def _softmax_numerators(qs, k_ref, kcols, s_scr, p_scr, want_sum):
    n_keys = k_ref.shape[0]
    chunks = [slice(c, min(c + KEY_CHUNK, n_keys)) for c in range(0, n_keys, KEY_CHUNK)]
    maxes = []
    for i, (q, cols) in enumerate(zip(qs, kcols)):
        m = None
        for ks in chunks:
            s = _dot_nt(q, k_ref[ks, cols])
            s_scr[i, :, ks] = s
            mc = jnp.max(s, axis=-1, keepdims=True)
            m = mc if m is None else jnp.maximum(m, mc)
        maxes.append(m)
    totals = []
    for i, m in enumerate(maxes):
        total = None
        for ks in chunks:
            e = jnp.exp2(s_scr[i, :, ks] - m)
            if want_sum:
                part = jnp.sum(e, axis=-1, keepdims=True)
                total = part if total is None else total + part
            p_scr[i, :, ks] = e.astype(BF16)
        totals.append(total)
    return totals


def _diff_attn_kernel(lam_ref, q_ref, *rest, lam_init, seg_rows):
    n_seg = len(seg_rows)
    k_refs, v_refs, g_ref = rest[:n_seg], rest[n_seg:2 * n_seg], rest[2 * n_seg]
    o_ref, k_scr, v_scr, s_scr, p_scr = rest[-5:]
    hd = DIFF_HEAD_DIM

    @pl.when(pl.program_id(2) == 0)
    def _():
        r0 = 0
        for k_ref, v_ref, n in zip(k_refs, v_refs, seg_rows):
            k_scr[r0:r0 + n] = k_ref[...]
            v_scr[r0:r0 + n] = v_ref[...]
            r0 += n

    lv = lam_ref[...]
    lam = (jnp.exp(jnp.sum(lv[0:1] * lv[1:2], axis=-1, keepdims=True))
           - jnp.exp(jnp.sum(lv[2:3] * lv[3:4], axis=-1, keepdims=True)) + lam_init)
    q = q_ref[...]
    cols = [slice(c * hd, (c + 1) * hd) for c in range(2)]
    totals = _softmax_numerators([q[:, c] for c in cols], k_scr, cols, s_scr, p_scr, True)
    outs = [_dot(p_scr[c], v_scr[...]) * (1.0 / totals[c]) for c in range(2)]
    o = outs[0] - lam * outs[1]
    o_ref[...] = (_rms(o, g_ref[...]) * (1.0 - lam_init)).astype(BF16)


def _diff_attn(qkvf, prev, lam_vec, subln_g, lam_init, dims, tq, latent):
    m_rows = qkvf.shape[0]
    batch, seq, ctx = dims["batch"], dims["seq"], dims["ctx"]
    hw = 2 * DIFF_HEAD_DIM
    width = qkvf.shape[1] // 4
    heads = width // hw
    ctx_blk0 = batch * seq // ctx
    qcol, kcol, vcol = width // hw, 2 * width // hw, 3 * width // hw
    if latent:
        nq, q_blk0, seg_rows = seq // tq, 0, (ctx, seq)
    else:
        assert tq == ctx
        nq, q_blk0, seg_rows = 1, ctx_blk0, (ctx,)
    n_keys = sum(seg_rows)

    def kv_specs(col):
        specs = [pl.BlockSpec((ctx, hw), lambda b, h, i: (ctx_blk0 + b, col + h))]
        if latent:
            specs.append(pl.BlockSpec((seq, hw), lambda b, h, i: (b, col + h)))
        return specs

    in_specs = ([pl.BlockSpec((4, DIFF_HEAD_DIM), lambda b, h, i: (0, 0)),
                 pl.BlockSpec((tq, hw), lambda b, h, i: (q_blk0 + b * nq + i, qcol + h))]
                + kv_specs(kcol) + kv_specs(vcol)
                + [pl.BlockSpec((1, hw), lambda b, h, i: (0, 0))])
    args = [lam_vec, qkvf] + [qkvf] * (2 * len(seg_rows)) + [subln_g.reshape(1, hw)]
    aliases = {}
    if prev is not None:
        in_specs.append(pl.BlockSpec(memory_space=pl.ANY))
        args.append(prev)
        aliases = {len(args) - 1: 0}
    kern = functools.partial(_diff_attn_kernel, lam_init=lam_init, seg_rows=seg_rows)
    return pl.pallas_call(
        kern,
        grid=(batch, heads, nq),
        in_specs=in_specs,
        out_specs=pl.BlockSpec((tq, hw), lambda b, h, i: (q_blk0 + b * nq + i, h)),
        out_shape=jax.ShapeDtypeStruct((m_rows, width), BF16),
        scratch_shapes=[pltpu.VMEM((n_keys, hw), BF16), pltpu.VMEM((n_keys, hw), BF16),
                        pltpu.VMEM((2, tq, n_keys), F32), pltpu.VMEM((2, tq, n_keys), BF16)],
        input_output_aliases=aliases,
        compiler_params=_cparams(3),
        name="ev_diff_attn_lat" if latent else "ev_diff_attn_ctx",
    )(*args)


def _fourier_kernel(u_ref, cc_ref, sc_ref, cs_ref, *rest, n, norm):
    o_ref, ab_ref = rest[-2], rest[-1]
    u = u_ref[...]
    ab_ref[0:n] = _dot(u, cc_ref[...]).astype(BF16)
    ab_ref[n:] = _dot(u, sc_ref[...]).astype(BF16)
    o_ref[...] = (_dot(cs_ref[...], ab_ref[...]) * norm).astype(BF16)


def _fourier(qkvf, prev, n, row_blk0, batch, width, cc, sc, cs, name):
    m_rows = qkvf.shape[0]
    gw = width // FOURIER_GROUPS
    kern = functools.partial(_fourier_kernel, n=n, norm=1.0 / math.sqrt(n * gw))
    in_specs = [pl.BlockSpec((n, gw), lambda b, g: (row_blk0 + b, g)),
                pl.BlockSpec((gw, gw), lambda b, g: (0, 0)),
                pl.BlockSpec((gw, gw), lambda b, g: (0, 0)),
                pl.BlockSpec((n, 2 * n), lambda b, g: (0, 0), pipeline_mode=pl.Buffered(1))]
    args = [qkvf, cc, sc, cs]
    aliases = {}
    if prev is not None:
        in_specs.append(pl.BlockSpec(memory_space=pl.ANY))
        args.append(prev)
        aliases = {4: 0}
    return pl.pallas_call(
        kern,
        grid=(batch, FOURIER_GROUPS),
        in_specs=in_specs,
        out_specs=pl.BlockSpec((n, gw), lambda b, g: (row_blk0 + b, g)),
        out_shape=jax.ShapeDtypeStruct((m_rows, width), BF16),
        scratch_shapes=[pltpu.VMEM((2 * n, gw), BF16)],
        input_output_aliases=aliases,
        compiler_params=_cparams(2),
        name=name,
    )(*args)


def _mm_res_kernel(*refs, n_a, gate_idx):
    a_refs, w_refs = refs[:n_a], refs[n_a:2 * n_a]
    x_ref, mod_ref, o_ref = refs[2 * n_a], refs[2 * n_a + 1], refs[-1]
    acc = _dot(a_refs[0][...], w_refs[0][...].astype(BF16))
    for a_ref, w_ref in zip(a_refs[1:], w_refs[1:]):
        acc = acc + _dot(a_ref[...], w_ref[...].astype(BF16))
    m = mod_ref[0]
    o_ref[...] = x_ref[...] + m[gate_idx:gate_idx + 1] * acc


def _mm_res(a_list, w, x, mod, gate_idx, dims, rows, tm, tn, name, out_rows=None, prev=None):
    d = w.shape[1]
    seq_tiles = dims["seq"] // tm
    n_a = len(a_list)
    ctx_only = prev is not None
    row_blk0 = dims["lat"] // tm if ctx_only else 0
    out_rows = prev.shape[0] if ctx_only else (out_rows or rows)

    def sel(i):
        return dims["batch"] if ctx_only else jnp.minimum(i // seq_tiles, dims["batch"])

    in_specs, w_args, k0 = [], [], 0
    for a in a_list:
        in_specs.append(pl.BlockSpec((tm, a.shape[1]), lambda i, j: (row_blk0 + i, 0)))
    for a in a_list:
        ka = a.shape[1]
        assert k0 % ka == 0
        in_specs.append(pl.BlockSpec((ka, tn), lambda i, j, kb=k0 // ka: (kb, j)))
        w_args.append(w)
        k0 += ka
    assert k0 == w.shape[0]
    in_specs += [pl.BlockSpec((tm, tn), lambda i, j: (i, j)),
                 pl.BlockSpec((1, 6, tn), lambda i, j: (sel(i), 0, j))]
    args = [*a_list, *w_args, x, mod]
    aliases = {}
    if prev is not None:
        in_specs.append(pl.BlockSpec(memory_space=pl.ANY))
        args.append(prev)
        aliases = {len(args) - 1: 0}
    return pl.pallas_call(
        functools.partial(_mm_res_kernel, n_a=n_a, gate_idx=gate_idx),
        grid=(rows // tm, d // tn),
        in_specs=in_specs,
        out_specs=pl.BlockSpec((tm, tn), lambda i, j: (row_blk0 + i, j)),
        out_shape=jax.ShapeDtypeStruct((out_rows, d), F32),
        input_output_aliases=aliases,
        compiler_params=_cparams(2),
        name=name,
    )(*args)


def _glu_kernel(x_ref, g_ref, mod_ref, wg_ref, wu_ref, o_ref, h_ref):
    @pl.when(pl.program_id(1) == 0)
    def _():
        m = mod_ref[0]
        h_ref[...] = _norm_mod(x_ref[...], g_ref[...], m[3:4], m[4:5]).astype(BF16)

    h = h_ref[...]
    gate = _dot(h, wg_ref[...].astype(BF16))
    up = _dot(h, wu_ref[...].astype(BF16))
    o_ref[...] = (_silu(gate) * up).astype(BF16)


def _glu(x, g, mod, wg, wu, dims, tm, tf):
    m_rows, d = x.shape
    f = wg.shape[1]
    seq_tiles = dims["seq"] // tm

    def sel(i):
        return jnp.minimum(i // seq_tiles, dims["batch"])

    return pl.pallas_call(
        _glu_kernel,
        grid=(m_rows // tm, f // tf),
        in_specs=[pl.BlockSpec((tm, d), lambda i, j: (i, 0)),
                  pl.BlockSpec((1, d), lambda i, j: (0, 0)),
                  pl.BlockSpec((1, 6, d), lambda i, j: (sel(i), 0, 0)),
                  pl.BlockSpec((d, tf), lambda i, j: (0, j)),
                  pl.BlockSpec((d, tf), lambda i, j: (0, j))],
        out_specs=pl.BlockSpec((tm, tf), lambda i, j: (i, j)),
        out_shape=jax.ShapeDtypeStruct((m_rows, f), BF16),
        scratch_shapes=[pltpu.VMEM((tm, d), BF16)],
        compiler_params=_cparams(2),
        name="ffn_glu",
    )(x, g.reshape(1, d), mod, wg, wu)


def _mla_proj_kernel(x_ref, g_ref, mod_ref, wd_ref, qg_ref, kvg_ref, wuq_ref, wukv_ref,
                     tq_ref, tk_ref, q_ref, kv_ref, kr_ref, *, n_lat_tiles, q_lora, kv_lora, heads):
    i = pl.program_id(0)
    m = mod_ref[0]
    h = _norm_mod(x_ref[...], g_ref[...], m[0:1], m[1:2]).astype(BF16)
    t = _dot(h, wd_ref[...])
    ckv = _rms(t[:, q_lora:q_lora + kv_lora], kvg_ref[...]).astype(BF16)
    kv_ref[...] = _dot(ckv, wukv_ref[...]).astype(BF16)
    kr = t[:, q_lora + kv_lora:]
    kr_ref[...] = _apply_rope(kr, tk_ref, MLA_ROPE // 4).astype(BF16)

    @pl.when(i < n_lat_tiles)
    def _():
        cq = _rms(t[:, :q_lora], qg_ref[...]).astype(BF16)
        q = _dot(cq, wuq_ref[...])
        scale = (MLA_NOPE + MLA_ROPE) ** -0.5 * LOG2E
        for hh in range(heads):
            c0 = hh * 2 * LANES
            q_ref[:, c0:c0 + LANES] = (q[:, c0:c0 + LANES] * scale).astype(BF16)
            q_ref[:, c0 + LANES:c0 + 2 * LANES] = _apply_rope(
                q[:, c0 + LANES:c0 + 2 * LANES], tq_ref, MLA_ROPE // 4).astype(BF16)


def _mla_proj(x, g, mod, wd, qg, kvg, wuq, wukv, tab_q, tab_k, dims, tm):
    m_rows, d = x.shape
    seq, lat = dims["seq"], dims["lat"]
    q_lora, kv_lora = qg.shape[0], kvg.shape[0]
    heads = wukv.shape[1] // (MLA_NOPE + MLA_V)
    lat_tiles, seq_tiles = lat // tm, seq // tm

    def sel(i):
        return jnp.minimum(i // seq_tiles, dims["batch"])

    def tab_map(i):
        return (0, jnp.where(i < lat_tiles, i % seq_tiles, seq_tiles), 0)

    kern = functools.partial(_mla_proj_kernel, n_lat_tiles=lat_tiles, q_lora=q_lora, kv_lora=kv_lora,
                             heads=heads)
    const = lambda i: (0, 0)
    return pl.pallas_call(
        kern,
        grid=(m_rows // tm,),
        in_specs=[pl.BlockSpec((tm, d), lambda i: (i, 0)),
                  pl.BlockSpec((1, d), const),
                  pl.BlockSpec((1, 6, d), lambda i: (sel(i), 0, 0)),
                  pl.BlockSpec(wd.shape, const),
                  pl.BlockSpec((1, q_lora), const),
                  pl.BlockSpec((1, kv_lora), const),
                  pl.BlockSpec(wuq.shape, const),
                  pl.BlockSpec(wukv.shape, const),
                  pl.BlockSpec((3, tm, LANES), tab_map),
                  pl.BlockSpec((3, tm, LANES), tab_map)],
        out_specs=[pl.BlockSpec((tm, wuq.shape[1]), lambda i: (jnp.minimum(i, lat_tiles - 1), 0)),
                   pl.BlockSpec((tm, wukv.shape[1]), lambda i: (i, 0)),
                   pl.BlockSpec((tm, LANES), lambda i: (i, 0))],
        out_shape=[jax.ShapeDtypeStruct((lat, wuq.shape[1]), BF16),
                   jax.ShapeDtypeStruct((m_rows, wukv.shape[1]), BF16),
                   jax.ShapeDtypeStruct((m_rows, LANES), BF16)],
        compiler_params=_cparams(1),
        name="od_mla_proj",
    )(x, g.reshape(1, d), mod, wd, qg.reshape(1, q_lora), kvg.reshape(1, kv_lora), wuq, wukv, tab_q, tab_k)


def _mla_attn_kernel(q_ref, knc_ref, knl_ref, krc_ref, krl_ref, vc_ref, vl_ref, o_ref,
                     k_scr, v_scr, s_scr, p_scr, *, n_ctx):
    @pl.when(pl.program_id(2) == 0)
    def _():
        k_scr[0:n_ctx, 0:LANES] = knc_ref[...]
        k_scr[n_ctx:, 0:LANES] = knl_ref[...]
        k_scr[0:n_ctx, LANES:] = krc_ref[...]
        k_scr[n_ctx:, LANES:] = krl_ref[...]
        v_scr[0:n_ctx, 0:LANES] = vc_ref[...]
        v_scr[n_ctx:, 0:LANES] = vl_ref[...]
        v_scr[:, LANES:] = jnp.ones((v_scr.shape[0], LANES), BF16)

    half = q_ref.shape[0] // 2
    halves = [slice(0, half), slice(half, 2 * half)]
    _softmax_numerators([q_ref[r] for r in halves], k_scr, [slice(None)] * 2, s_scr, p_scr, False)
    for i, r in enumerate(halves):
        acc = _dot(p_scr[i], v_scr[...])
        o_ref[r] = (acc[:, :LANES] / acc[:, LANES:LANES + 1]).astype(BF16)


def _mla_attn(q, kv, kr, dims, tq):
    batch, seq, ctx, lat = dims["batch"], dims["seq"], dims["ctx"], dims["lat"]
    heads = q.shape[1] // (2 * LANES)
    nq = seq // tq
    ctx_blk0 = lat // ctx
    return pl.pallas_call(
        functools.partial(_mla_attn_kernel, n_ctx=ctx),
        grid=(batch, heads, nq),
        in_specs=[pl.BlockSpec((tq, 2 * LANES), lambda b, h, i: (b * nq + i, h)),
                  pl.BlockSpec((ctx, LANES), lambda b, h, i: (ctx_blk0 + b, 2 * h)),
                  pl.BlockSpec((seq, LANES), lambda b, h, i: (b, 2 * h)),
                  pl.BlockSpec((ctx, LANES), lambda b, h, i: (ctx_blk0 + b, 0)),
                  pl.BlockSpec((seq, LANES), lambda b, h, i: (b, 0)),
                  pl.BlockSpec((ctx, LANES), lambda b, h, i: (ctx_blk0 + b, 2 * h + 1)),
                  pl.BlockSpec((seq, LANES), lambda b, h, i: (b, 2 * h + 1))],
        out_specs=pl.BlockSpec((tq, LANES), lambda b, h, i: (b * nq + i, h)),
        out_shape=jax.ShapeDtypeStruct((lat, heads * LANES), BF16),
        scratch_shapes=[pltpu.VMEM((ctx + seq, 2 * LANES), BF16), pltpu.VMEM((ctx + seq, 2 * LANES), BF16),
                        pltpu.VMEM((2, tq // 2, ctx + seq), F32), pltpu.VMEM((2, tq // 2, ctx + seq), BF16)],
        compiler_params=_cparams(3),
        name="od_mla_attn",
    )(q, kv, kv, kr, kr, kv, kv)


def _router_kernel(x_ref, g_ref, mod_ref, r_ref, h_ref, idx_ref, w_ref):
    m = mod_ref[0]
    h = _norm_mod(x_ref[...], g_ref[...], m[3:4], m[4:5])
    h_ref[...] = h
    r = r_ref[...]
    h_hi = h.astype(BF16)
    h_lo = (h - h_hi.astype(F32)).astype(BF16)
    r_hi = r.astype(BF16)
    r_lo = (r - r_hi.astype(F32)).astype(BF16)
    rows = h.shape[0]
    prod = _dot(jnp.concatenate([h_hi, h_lo], axis=0), jnp.concatenate([r_hi, r_lo], axis=1))
    logits = prod[:rows, :LANES] + (prod[rows:, :LANES] + prod[:rows, LANES:])
    lane = lax.broadcasted_iota(jnp.int32, logits.shape, 1)
    lane_f = lane.astype(F32)
    neg = jnp.float32(-jnp.inf)
    logits = jnp.where(lane < N_EXPERTS, logits, neg)
    m1 = jnp.max(logits, axis=-1, keepdims=True)
    i1 = jnp.min(jnp.where(logits == m1, lane_f, float(LANES)), axis=-1, keepdims=True)
    rest = jnp.where(lane_f == i1, neg, logits)
    m2 = jnp.max(rest, axis=-1, keepdims=True)
    i2 = jnp.min(jnp.where(rest == m2, lane_f, float(LANES)), axis=-1, keepdims=True)
    e2 = jnp.exp(m2 - m1)
    w1 = 1.0 / (1.0 + e2)
    w2 = e2 / (1.0 + e2)
    idx_ref[...] = jnp.where(lane == 0, i1, jnp.where(lane == 1, i2, 0.0)).astype(jnp.int32)
    w_ref[...] = jnp.where(lane == 0, w1, jnp.where(lane == 1, w2, 0.0))


def _router(x, g, mod, router_pad, dims, tm):
    lat, d = x.shape
    seq_tiles = dims["seq"] // tm
    return pl.pallas_call(
        _router_kernel,
        grid=(lat // tm,),
        in_specs=[pl.BlockSpec((tm, d), lambda i: (i, 0)),
                  pl.BlockSpec((1, d), lambda i: (0, 0)),
                  pl.BlockSpec((1, 6, d), lambda i: (i // seq_tiles, 0, 0)),
                  pl.BlockSpec((d, LANES), lambda i: (0, 0))],
        out_specs=[pl.BlockSpec((tm, d), lambda i: (i, 0)),
                   pl.BlockSpec((tm, LANES), lambda i: (i, 0)),
                   pl.BlockSpec((tm, LANES), lambda i: (i, 0))],
        out_shape=[jax.ShapeDtypeStruct((lat, d), F32),
                   jax.ShapeDtypeStruct((lat, LANES), jnp.int32),
                   jax.ShapeDtypeStruct((lat, LANES), F32)],
        compiler_params=_cparams(1),
        name="moe_router",
    )(x, g.reshape(1, d), mod, router_pad)


def _moe_plan(top_idx, n_items):
    e_flat = top_idx.reshape(-1)
    n_assign = e_flat.shape[0]
    onehot = (e_flat[:, None] == jnp.arange(N_EXPERTS, dtype=jnp.int32)[None, :]).astype(jnp.int32)
    csum = jnp.cumsum(onehot, axis=0)
    counts = csum[-1]
    rank = jnp.sum(csum * onehot, axis=1) - 1
    blocks = (counts + MOE_BLOCK - 1) // MOE_BLOCK
    blk_end = jnp.cumsum(blocks)
    blk_start = blk_end - blocks
    total = blk_end[-1]
    dest = blk_start[e_flat] * MOE_BLOCK + rank
    row_tok = jnp.zeros((n_items * MOE_BLOCK,), jnp.int32).at[dest].set(
        jnp.arange(n_assign, dtype=jnp.int32) // TOP_K)
    p = jnp.arange(n_items, dtype=jnp.int32)
    pc = jnp.minimum(p, total - 1)
    item_e = jnp.sum((pc[:, None] >= blk_end[None, :]).astype(jnp.int32), axis=1)
    rows_left = counts[item_e] - (pc - blk_start[item_e]) * MOE_BLOCK
    nact = jnp.clip((rows_left + MOE_SUB - 1) // MOE_SUB, 0, MOE_BLOCK // MOE_SUB)
    nact = jnp.where(p < total, nact, 0).astype(jnp.int32)
    out_blk = jnp.where(p < total, p, n_items).astype(jnp.int32)
    items = (item_e.astype(jnp.int32), pc.astype(jnp.int32), out_blk, nact)
    return dest.astype(jnp.int32), row_tok, items, (blocks, blk_start, blk_end, total)


def _moe_steps(items, runs, n_items, nj):
    _, _, _, nact = items
    blocks, blk_start, blk_end, total = runs
    s = jnp.arange(n_items * nj, dtype=jnp.int32)
    live = s < total * nj
    sc = jnp.minimum(s, total * nj - 1)
    e = jnp.sum((sc[:, None] >= (blk_end * nj)[None, :]).astype(jnp.int32), axis=1)
    t = sc - blk_start[e] * nj
    j = t // blocks[e]
    r = t % blocks[e]
    blk = blk_start[e] + jnp.where(j % 2 == 0, r, blocks[e] - 1 - r)
    out_blk = jnp.where(live, blk, n_items)
    out_j = jnp.where(live, j, 0)
    step_nact = jnp.where(live, nact[blk], 0)
    prev_e = jnp.concatenate([jnp.full((1,), -1, jnp.int32), e[:-1]])
    prev_j = jnp.concatenate([jnp.full((1,), -1, jnp.int32), j[:-1]])
    first = jnp.logical_and(live, jnp.logical_or(e != prev_e, j != prev_j))
    slot = (jnp.cumsum(first.astype(jnp.int32)) - 1) % 2
    ids = jnp.arange(N_EXPERTS, dtype=jnp.int32)
    later = jnp.logical_and(ids[None, :] > ids[:, None], blocks[None, :] > 0)
    next_expert = jnp.min(jnp.where(later, ids[None, :], N_EXPERTS), axis=1)
    wraps = j + 1 >= nj
    next_e = jnp.where(wraps, next_expert[e], e)
    next_j = jnp.where(wraps, 0, j + 1)
    has_next = jnp.logical_and(first, next_e < N_EXPERTS)
    next_e = jnp.minimum(next_e, N_EXPERTS - 1)
    return tuple(a.astype(jnp.int32) for a in (e, blk, j, out_blk, out_j, step_nact,
                                                first, slot, next_e, next_j, has_next))


def _gather_kernel(tok_ref, nact_ref, h_ref, o_ref, buf, sem):
    p = pl.program_id(0)
    last = pl.num_programs(0) - 1
    nact = nact_ref[p]
    slot = p % 2

    def copy(to_slot, r, t):
        return pltpu.make_async_copy(h_ref.at[pl.ds(t, 1)], buf.at[to_slot, pl.ds(r, 1)], sem.at[to_slot])

    def issue(item, to_slot):
        def body(grp, c):
            for u in range(GATHER_UNROLL):
                r = grp * GATHER_UNROLL + u
                copy(to_slot, r, tok_ref[item * MOE_BLOCK + r]).start()
            return c

        lax.fori_loop(0, nact_ref[item] * (MOE_SUB // GATHER_UNROLL), body, 0)

    def drain(grp, c):
        for u in range(GATHER_UNROLL):
            copy(slot, grp * GATHER_UNROLL + u, 0).wait()
        return c

    @pl.when(p == 0)
    def _():
        issue(0, 0)

    @pl.when(p < last)
    def _():
        issue(jnp.minimum(p + 1, last), 1 - slot)

    lax.fori_loop(0, nact * (MOE_SUB // GATHER_UNROLL), drain, 0)
    for s in range(MOE_BLOCK // MOE_SUB):
        rows = pl.ds(s * MOE_SUB, MOE_SUB)

        @pl.when(s < nact)
        def _():
            o_ref[rows] = buf[slot, rows].astype(BF16)

        @pl.when(s >= nact)
        def _():
            o_ref[rows] = jnp.zeros((MOE_SUB, o_ref.shape[1]), BF16)


def _moe_gather(h, row_tok, nact, n_items):
    d = h.shape[1]
    return pl.pallas_call(
        _gather_kernel,
        grid_spec=pltpu.PrefetchScalarGridSpec(
            num_scalar_prefetch=2,
            grid=(n_items,),
            in_specs=[pl.BlockSpec(memory_space=pl.ANY)],
            out_specs=pl.BlockSpec((MOE_BLOCK, d), lambda p, tok, na: (p, 0)),
            scratch_shapes=[pltpu.VMEM((2, MOE_BLOCK, d), F32), pltpu.SemaphoreType.DMA((2,))]),
        out_shape=jax.ShapeDtypeStruct((n_items * MOE_BLOCK, d), BF16),
        compiler_params=_cparams(1),
        name="moe_gather",
    )(row_tok, nact, h)


def _for_active_rows(nact, in_ref, o_ref, fn):
    n_sub = MOE_BLOCK // MOE_SUB
    for k in range(n_sub + 1):
        @pl.when(nact == k)
        def _():
            if k > 0:
                o_ref[0:k * MOE_SUB] = fn(in_ref[0:k * MOE_SUB])
            if k < n_sub:
                o_ref[k * MOE_SUB:] = jnp.zeros((MOE_BLOCK - k * MOE_SUB, o_ref.shape[1]), o_ref.dtype)


def _moe_glu_kernel(e_ref, blk_ref, j_ref, oblk_ref, oj_ref, nact_ref, first_ref, slot_ref, ne_ref, nj_ref,
                    more_ref, h_ref, wg_ref, wu_ref, o_ref, wbuf, sem):
    s = pl.program_id(0)
    nact = nact_ref[s]
    slot = slot_ref[s]
    tf = wbuf.shape[3]

    def tile_copies(e, j, to_slot):
        cols = pl.ds(pl.multiple_of(j * tf, tf), tf)
        return [pltpu.make_async_copy(w_ref.at[e, :, cols], wbuf.at[to_slot, i], sem.at[to_slot])
                for i, w_ref in enumerate((wg_ref, wu_ref))]

    @pl.when(s == 0)
    def _():
        for cp in tile_copies(e_ref[0], j_ref[0], 0):
            cp.start()

    @pl.when(first_ref[s] == 1)
    def _():
        for cp in tile_copies(e_ref[s], j_ref[s], slot):
            cp.wait()

        @pl.when(more_ref[s] == 1)
        def _():
            for cp in tile_copies(ne_ref[s], nj_ref[s], 1 - slot):
                cp.start()

    def run(h):
        gate = _dot(h, wbuf[slot, 0].astype(BF16))
        return (_silu(gate) * _dot(h, wbuf[slot, 1].astype(BF16))).astype(BF16)

    _for_active_rows(nact, h_ref, o_ref, run)


def _moe_glu(hs, wg, wu, steps, n_items, tf):
    d, f = wg.shape[1], wg.shape[2]
    nj = f // tf
    n_pre = len(steps)
    return pl.pallas_call(
        _moe_glu_kernel,
        grid_spec=pltpu.PrefetchScalarGridSpec(
            num_scalar_prefetch=n_pre,
            grid=(n_items * nj,),
            in_specs=[pl.BlockSpec((MOE_BLOCK, d), lambda s, *pre: (pre[1][s], 0)),
                      pl.BlockSpec(memory_space=pl.ANY),
                      pl.BlockSpec(memory_space=pl.ANY)],
            out_specs=pl.BlockSpec((MOE_BLOCK, tf), lambda s, *pre: (pre[3][s], pre[4][s])),
            scratch_shapes=[pltpu.VMEM((2, 2, d, tf), F32), pltpu.SemaphoreType.DMA((2,))]),
        out_shape=jax.ShapeDtypeStruct(((n_items + 1) * MOE_BLOCK, f), BF16),
        compiler_params=_cparams(1),
        name="moe_glu",
    )(*steps, hs, wg, wu)


def _moe_down_kernel(e_ref, blk_ref, oblk_ref, nact_ref, a_ref, wd_ref, o_ref):
    nact = nact_ref[pl.program_id(0)]
    _for_active_rows(nact, a_ref, o_ref, lambda a: _dot(a, wd_ref[0].astype(BF16)))


def _moe_down(a, wd, plan, n_items, tn):
    item_e, in_blk, out_blk, nact = plan
    f, d = wd.shape[1], wd.shape[2]
    nj = d // tn

    def w_map(p, j, e, b, ob, na):
        return (e[p], 0, jnp.where(na[p] > 0, j, nj - 1))

    def o_map(p, j, e, b, ob, na):
        return (ob[p], jnp.where(na[p] > 0, j, 0))

    return pl.pallas_call(
        _moe_down_kernel,
        grid_spec=pltpu.PrefetchScalarGridSpec(
            num_scalar_prefetch=4,
            grid=(n_items, nj),
            in_specs=[pl.BlockSpec((MOE_BLOCK, f), lambda p, j, e, b, ob, na: (b[p], 0)),
                      pl.BlockSpec((1, f, tn), w_map)],
            out_specs=pl.BlockSpec((MOE_BLOCK, tn), o_map)),
        out_shape=jax.ShapeDtypeStruct(((n_items + 1) * MOE_BLOCK, d), F32),
        compiler_params=_cparams(2),
        name="moe_down",
    )(item_e, in_blk, out_blk, nact, a, wd)


def _combine_kernel(dest_ref, x_ref, mod_ref, w_ref, fg_ref, y_ref, o_ref, buf, sem, *, tm):
    i = pl.program_id(0)
    last = pl.num_programs(0) - 1
    slot = i % 2

    def copy(to_slot, r, k, row):
        return pltpu.make_async_copy(y_ref.at[pl.ds(row, 1)], buf.at[to_slot, k, pl.ds(r, 1)],
                                     sem.at[to_slot])

    def issue(tile, to_slot):
        def body(grp, c):
            for u in range(GATHER_UNROLL):
                r = grp * GATHER_UNROLL + u
                for k in range(TOP_K):
                    copy(to_slot, r, k, dest_ref[(tile * tm + r) * TOP_K + k]).start()
            return c

        lax.fori_loop(0, tm // GATHER_UNROLL, body, 0)

    def drain(grp, c):
        for u in range(GATHER_UNROLL):
            for k in range(TOP_K):
                copy(slot, grp * GATHER_UNROLL + u, k, 0).wait()
        return c

    @pl.when(i == 0)
    def _():
        issue(0, 0)

    @pl.when(i < last)
    def _():
        issue(jnp.minimum(i + 1, last), 1 - slot)

    lax.fori_loop(0, tm // GATHER_UNROLL, drain, 0)
    w = w_ref[...]
    moe = w[:, 0:1] * buf[slot, 0] + w[:, 1:2] * buf[slot, 1]
    m = mod_ref[0]
    o_ref[...] = _rms(x_ref[...] + m[5:6] * moe, fg_ref[...])


def _moe_combine(dest, x, mod, top_w, final_g, ys, dims, tm):
    lat, d = x.shape
    seq_tiles = dims["seq"] // tm
    return pl.pallas_call(
        functools.partial(_combine_kernel, tm=tm),
        grid_spec=pltpu.PrefetchScalarGridSpec(
            num_scalar_prefetch=1,
            grid=(lat // tm,),
            in_specs=[pl.BlockSpec((tm, d), lambda i, dr: (i, 0)),
                      pl.BlockSpec((1, 6, d), lambda i, dr: (i // seq_tiles, 0, 0)),
                      pl.BlockSpec((tm, LANES), lambda i, dr: (i, 0)),
                      pl.BlockSpec((1, d), lambda i, dr: (0, 0)),
                      pl.BlockSpec(memory_space=pl.ANY)],
            out_specs=pl.BlockSpec((tm, d), lambda i, dr: (i, 0)),
            scratch_shapes=[pltpu.VMEM((2, TOP_K, tm, d), F32), pltpu.SemaphoreType.DMA((2,))]),
        out_shape=jax.ShapeDtypeStruct((lat, d), F32),
        compiler_params=_cparams(1),
        name="moe_combine",
    )(dest, x, mod, top_w, final_g.reshape(1, d), ys)


def kernel(x, c, ctx, c_ctx, ada_w, ada_b, norm1_g, norm2_g, ev_w_in, ev_w_out, ev_lambda, ev_subln_g, od_w_dq, od_q_norm_g, od_w_uq, od_w_dkv, od_kv_norm_g, od_w_ukv, od_w_o, ffn_w_gate, ffn_w_up, ffn_w_down, moe_router, moe_w_gate, moe_w_up, moe_w_down, final_norm_g):
    batch, seq, d = x.shape
    n_ctx = ctx.shape[1]
    depth = ada_w.shape[0]
    assert depth == 2 and batch < MOD_ROWS and seq % GRID_W == 0
    lat = batch * seq
    dims = dict(batch=batch, seq=seq, ctx=n_ctx, lat=lat)
    tm = 1024
    assert seq % tm == 0 and (batch * n_ctx) % tm == 0

    cond = jnp.concatenate([c, c_ctx[None, :], jnp.zeros((MOD_ROWS - batch - 1, d), F32)], axis=0)
    mod = _ada(cond, ada_w, ada_b).reshape(depth, MOD_ROWS, 6, d)
    x_lat, x_ctx = x.reshape(lat, d), ctx.reshape(batch * n_ctx, d)
    m_rows = lat + batch * n_ctx

    lam_init = 0.8 - 0.6 * math.exp(-0.3 * 0)
    fw = ev_w_in.shape[2] // 4
    tabs_ev = jnp.asarray(np.stack([
        _rope_tables(seq, tm, DIFF_HEAD_DIM // 4, DIFF_HEAD_DIM ** -0.5 * LOG2E),
        _rope_tables(seq, tm, DIFF_HEAD_DIM // 4, 1.0)]))
    w_in = ev_w_in[0].astype(BF16)
    qkvf = _inproj(x_lat, None, norm1_g[0], mod[0], w_in, tabs_ev, dims, tm, 1024)
    qkvf = _inproj(x_ctx, qkvf, norm1_g[0], mod[0], w_in, tabs_ev, dims, tm, 1024)
    o_attn = _diff_attn(qkvf, None, ev_lambda[0], ev_subln_g[0], lam_init, dims, 1024, True)
    o_attn = _diff_attn(qkvf, o_attn, ev_lambda[0], ev_subln_g[0], lam_init, dims, n_ctx, False)

    gw = fw // FOURIER_GROUPS
    cc_np, sc_np = _dft_cos_sin(gw)
    cc = jnp.asarray(cc_np.astype(np.float32)).astype(BF16)
    sc = jnp.asarray(sc_np.astype(np.float32)).astype(BF16)
    cn_np, sn_np = _dft_cos_sin(seq)
    cs_lat = jnp.asarray(np.concatenate([cn_np, -sn_np], axis=1).astype(np.float32)).astype(BF16)
    cx_np, sx_np = _dft_cos_sin(n_ctx)
    cs_ctx = jnp.asarray(np.concatenate([cx_np, -sx_np], axis=1).astype(np.float32)).astype(BF16)
    fm = _fourier(qkvf, None, seq, 0, batch, fw, cc, sc, cs_lat, "ev_fourier_lat")
    fm = _fourier(qkvf, fm, n_ctx, lat // n_ctx, batch, fw, cc, sc, cs_ctx, "ev_fourier_ctx")

    w_out = ev_w_out[0].astype(BF16)
    xs = _mm_res([fm, o_attn], w_out, x_lat, mod[0], 2, dims, lat, tm, 1024, "ev_outproj_lat", out_rows=m_rows)
    xs = _mm_res([fm, o_attn], w_out, x_ctx, mod[0], 2, dims, batch * n_ctx, tm, 1024, "ev_outproj_ctx", prev=xs)
    act = _glu(xs, norm2_g[0], mod[0], ffn_w_gate[0], ffn_w_up[0], dims, tm, 512)
    xs = _mm_res([act], ffn_w_down[0].astype(BF16), xs, mod[0], 5, dims, m_rows, tm, 512, "ffn_down")

    heads = od_w_ukv.shape[2] // (MLA_NOPE + MLA_V)
    q_lora = od_w_dq.shape[2]
    wd_cat = jnp.concatenate(
        [od_w_dq[0], od_w_dkv[0], jnp.zeros((d, LANES - MLA_ROPE), F32)], axis=1).astype(BF16)
    wuq = jnp.pad(od_w_uq[0].reshape(q_lora, heads, MLA_NOPE + MLA_ROPE),
                  ((0, 0), (0, 0), (0, 2 * LANES - MLA_NOPE - MLA_ROPE))).reshape(q_lora, heads * 2 * LANES)
    tm_mla = 256
    tab_q = jnp.asarray(_rope_tables(seq, tm_mla, MLA_ROPE // 4, (MLA_NOPE + MLA_ROPE) ** -0.5 * LOG2E))
    tab_k = jnp.asarray(_rope_tables(seq, tm_mla, MLA_ROPE // 4, 1.0))
    q, kv, kr = _mla_proj(xs, norm1_g[1], mod[1], wd_cat, od_q_norm_g[0], od_kv_norm_g[0],
                          wuq.astype(BF16), od_w_ukv[0].astype(BF16), tab_q, tab_k, dims, tm_mla)
    o_mla = _mla_attn(q, kv, kr, dims, 2048)
    xl = _mm_res([o_mla], od_w_o[0].astype(BF16), xs, mod[1], 2, dims, lat, tm, 1024, "od_outproj")

    router_pad = jnp.pad(moe_router[0], ((0, 0), (0, LANES - N_EXPERTS)))
    h2, top_idx, top_w = _router(xl, norm2_g[1], mod[1], router_pad, dims, 1024)
    n_items = lat * TOP_K // MOE_BLOCK + N_EXPERTS
    dest, row_tok, items, runs = _moe_plan(top_idx[:, :TOP_K], n_items)
    tf = 512
    steps = _moe_steps(items, runs, n_items, moe_w_gate.shape[3] // tf)
    hs = _moe_gather(h2, row_tok, items[3], n_items)
    act = _moe_glu(hs, moe_w_gate[0], moe_w_up[0], steps, n_items, tf)
    ys = _moe_down(act, moe_w_down[0], items, n_items, 256)
    out = _moe_combine(dest, xl, mod[1], top_w, final_norm_g, ys, dims, 512)
    return out.reshape(batch, seq, d)
```
